```python
import jax, jax.numpy as jnp
from jax import lax
import numpy as np

D_MODEL = 1024
BATCH = 4
SEQ = 8192
DEPTH = 2

CHUNK = 64
Q_BLOCK = 128
NORM_EPS = 1e-6

SB_HEADS = 8
SB_HEAD_DIM = 64
SB_WIDTH = SB_HEADS * SB_HEAD_DIM
POOL_WINDOWS = (2, 4, 8, 16)
POOL_GROUPS = len(POOL_WINDOWS)
POOL_WIDTH = D_MODEL - SB_WIDTH
POOL_GROUP_DIM = POOL_WIDTH // POOL_GROUPS
EVEN_IN_WIDTH = 3 * SB_WIDTH + POOL_WIDTH
EVEN_MIX_WIDTH = SB_WIDTH + POOL_WIDTH

GLA_HEADS = 4
GLA_KEY_WIDTH = D_MODEL // 2
GLA_VALUE_WIDTH = D_MODEL
GLA_KEY_DIM = GLA_KEY_WIDTH // GLA_HEADS
GLA_VALUE_DIM = GLA_VALUE_WIDTH // GLA_HEADS
GLA_GATE_RANK = 16
GLA_TAU = 16.0
ODD_IN_WIDTH = 2 * GLA_KEY_WIDTH + 2 * GLA_VALUE_WIDTH + GLA_GATE_RANK

FFN_HIDDEN = 2816
CONV_WIDTH = 3

N_EVEN = (DEPTH + 1) // 2
N_ODD = DEPTH // 2

kernel_name = "hybrid_stickbreak_pool_gla_convffn"


def rms_norm(x, gain):
    xf = x.astype(jnp.float32)
    y = xf * lax.rsqrt(jnp.mean(xf * xf, axis=-1, keepdims=True) + NORM_EPS)
    return (y * gain.astype(jnp.float32)).astype(x.dtype)


def stick_breaking_attention(q, k, v):
    b, s, h, dh = q.shape
    qh = jnp.transpose(q, (0, 2, 1, 3))
    kh = jnp.transpose(k, (0, 2, 1, 3))
    vh = jnp.transpose(v, (0, 2, 1, 3))
    scale = dh ** -0.5
    key_pos = jnp.arange(s)

    def block(i):
        start = i * Q_BLOCK
        q_blk = lax.dynamic_slice_in_dim(qh, start, Q_BLOCK, axis=2)
        z = jnp.einsum('bhqd,bhkd->bhqk', q_blk, kh,
                       preferred_element_type=jnp.float32) * scale
        q_pos = start + jnp.arange(Q_BLOCK)
        visible = key_pos[None, :] < q_pos[:, None]
        log_fail = jnp.where(visible, jax.nn.log_sigmoid(-z), 0.0)
        cum_rev = lax.cumsum(log_fail, axis=3, reverse=True)
        later = jnp.concatenate([cum_rev[..., 1:], jnp.zeros_like(cum_rev[..., :1])], axis=-1)
        weights = jnp.where(visible, jnp.exp(jax.nn.log_sigmoid(z) + later), 0.0)
        return jnp.einsum('bhqk,bhkd->bhqd', weights.astype(vh.dtype), vh)

    out = lax.map(block, jnp.arange(s // Q_BLOCK))
    return jnp.transpose(out, (1, 0, 3, 2, 4)).reshape(b, s, h * dh)


def multiscale_pool(xb, pool_w, pool_scale):
    b, s, _ = xb.shape
    groups = xb.reshape(b, s, POOL_GROUPS, POOL_GROUP_DIM)
    prefix = jnp.cumsum(groups.astype(jnp.float32), axis=1)
    prefix = jnp.pad(prefix, ((0, 0), (1, 0), (0, 0), (0, 0)))
    pos = jnp.arange(1, s + 1, dtype=jnp.float32)
    outs = []
    for g, w in enumerate(POOL_WINDOWS):
        pg = jnp.pad(prefix[:, :, g], ((0, 0), (w - 1, 0), (0, 0)))
        window_sum = pg[:, w:] - pg[:, :s]
        count = jnp.minimum(pos, float(w))[None, :, None]
        pooled = window_sum / count - groups[:, :, g].astype(jnp.float32)
        outs.append(jnp.einsum('bsc,cd->bsd', pooled.astype(xb.dtype), pool_w[g]))
    return jnp.concatenate(outs, axis=-1) * pool_scale


def gated_linear_attention(q, k, v, log_alpha):
    b, s, h, dk = q.shape
    dv = v.shape[-1]
    n = s // CHUNK

    def chunks(t):
        return jnp.moveaxis(t.reshape(b, n, CHUNK, h, t.shape[-1]), 1, 0)

    qc = chunks(q.astype(jnp.float32))
    kc = chunks(k.astype(jnp.float32))
    vc = chunks(v.astype(jnp.float32))
    cum = jnp.cumsum(chunks(log_alpha), axis=2)
    total = cum[:, :, -1]
    k_dec = kc * jnp.exp(total[:, :, None] - cum)
    chunk_decay = jnp.exp(total)

    def step(state, inp):
        q_c, k_c, v_c, decay_c = inp
        state = decay_c[..., None] * state + jnp.einsum('bchk,bchv->bhkv', k_c, v_c)
        return state, jnp.einsum('bchk,bhkv->bchv', q_c, state)

    state0 = jnp.zeros((b, h, dk, dv), jnp.float32)
    _, out = lax.scan(step, state0, (qc, k_dec, vc, chunk_decay))
    return jnp.moveaxis(out, 0, 1).reshape(b, s, h, dv)


def even_mixer(h, w_in, q_gain, k_gain, pool_w, pool_scale, w_out):
    b, s, _ = h.shape
    proj = h @ w_in
    q, k, v, xb = jnp.split(proj, [SB_WIDTH, 2 * SB_WIDTH, 3 * SB_WIDTH], axis=-1)
    q = rms_norm(q.reshape(b, s, SB_HEADS, SB_HEAD_DIM), q_gain)
    k = rms_norm(k.reshape(b, s, SB_HEADS, SB_HEAD_DIM), k_gain)
    v = v.reshape(b, s, SB_HEADS, SB_HEAD_DIM)
    o_a = stick_breaking_attention(q, k, v)
    o_b = multiscale_pool(xb, pool_w, pool_scale)
    return jnp.concatenate([o_a.astype(h.dtype), o_b.astype(h.dtype)], axis=-1) @ w_out


def odd_mixer(h, w_in, w_a2, b_a, o_gain, w_out):
    b, s, _ = h.shape
    proj = h @ w_in
    q, k, v, r, a_low = jnp.split(
        proj, [GLA_KEY_WIDTH, 2 * GLA_KEY_WIDTH, 2 * GLA_KEY_WIDTH + GLA_VALUE_WIDTH,
               2 * GLA_KEY_WIDTH + 2 * GLA_VALUE_WIDTH], axis=-1)
    q = q.reshape(b, s, GLA_HEADS, GLA_KEY_DIM) * (GLA_KEY_DIM ** -0.5)
    k = k.reshape(b, s, GLA_HEADS, GLA_KEY_DIM)
    v = v.reshape(b, s, GLA_HEADS, GLA_VALUE_DIM)
    gate_logit = (a_low @ w_a2 + b_a).astype(jnp.float32).reshape(b, s, GLA_HEADS, GLA_KEY_DIM)
    log_alpha = jax.nn.log_sigmoid(gate_logit) / GLA_TAU
    o = gated_linear_attention(q, k, v, log_alpha)
    o = rms_norm(o, o_gain).reshape(b, s, GLA_VALUE_WIDTH).astype(h.dtype)
    return (o * jax.nn.silu(r)) @ w_out


def conv_ffn(h, w_up, conv_w, conv_b, w_down):
    s = h.shape[1]
    u = h @ w_up
    up = jnp.pad(u, ((0, 0), (CONV_WIDTH - 1, 0), (0, 0)))
    conv = conv_b
    for j in range(CONV_WIDTH):
        conv = conv + conv_w[j] * up[:, j:j + s]
    a, g = jnp.split(conv, 2, axis=-1)
    return (jax.nn.silu(a) * g) @ w_down


def setup_inputs(seed: int = 0) -> dict:
    key = jax.random.key(seed)
    ks = jax.random.split(key, 20)
    f32 = jnp.float32

    def nrm(k, shape, scale):
        return jax.random.normal(k, shape, f32) * scale

    return {
        "x": nrm(ks[0], (BATCH, SEQ, D_MODEL), 1.0),
        "mix_norm_even": 1.0 + nrm(ks[1], (N_EVEN, D_MODEL), 0.02),
        "w_in_even": nrm(ks[2], (N_EVEN, D_MODEL, EVEN_IN_WIDTH), D_MODEL ** -0.5),
        "sb_q_gain": 1.0 + nrm(ks[3], (N_EVEN, SB_HEAD_DIM), 0.02),
        "sb_k_gain": 1.0 + nrm(ks[4], (N_EVEN, SB_HEAD_DIM), 0.02),
        "pool_w": nrm(ks[5], (N_EVEN, POOL_GROUPS, POOL_GROUP_DIM, POOL_GROUP_DIM), POOL_GROUP_DIM ** -0.5),
        "pool_scale": 1.0 + nrm(ks[6], (N_EVEN, POOL_WIDTH), 0.1),
        "w_out_even": nrm(ks[7], (N_EVEN, EVEN_MIX_WIDTH, D_MODEL), EVEN_MIX_WIDTH ** -0.5),
        "mix_norm_odd": 1.0 + nrm(ks[8], (N_ODD, D_MODEL), 0.02),
        "w_in_odd": nrm(ks[9], (N_ODD, D_MODEL, ODD_IN_WIDTH), D_MODEL ** -0.5),
        "gla_w_a2": nrm(ks[10], (N_ODD, GLA_GATE_RANK, GLA_KEY_WIDTH), GLA_GATE_RANK ** -0.5),
        "gla_b_a": nrm(ks[11], (N_ODD, GLA_KEY_WIDTH), 0.01),
        "gla_o_gain": 1.0 + nrm(ks[12], (N_ODD, GLA_VALUE_DIM), 0.02),
        "w_out_odd": nrm(ks[13], (N_ODD, GLA_VALUE_WIDTH, D_MODEL), GLA_VALUE_WIDTH ** -0.5),
        "ffn_norm": 1.0 + nrm(ks[14], (DEPTH, D_MODEL), 0.02),
        "ffn_w_up": nrm(ks[15], (DEPTH, D_MODEL, 2 * FFN_HIDDEN), D_MODEL ** -0.5),
        "ffn_conv_w": nrm(ks[16], (DEPTH, CONV_WIDTH, 2 * FFN_HIDDEN), CONV_WIDTH ** -0.5),
        "ffn_conv_b": nrm(ks[17], (DEPTH, 2 * FFN_HIDDEN), 0.01),
        "ffn_w_down": nrm(ks[18], (DEPTH, FFN_HIDDEN, D_MODEL), FFN_HIDDEN ** -0.5),
    }


def reference(x, mix_norm_even, w_in_even, sb_q_gain, sb_k_gain, pool_w, pool_scale, w_out_even,
              mix_norm_odd, w_in_odd, gla_w_a2, gla_b_a, gla_o_gain, w_out_odd,
              ffn_norm, ffn_w_up, ffn_conv_w, ffn_conv_b, ffn_w_down):
    for layer in range(DEPTH):
        if layer % 2 == 0:
            i = layer // 2
            x = x + even_mixer(rms_norm(x, mix_norm_even[i]), w_in_even[i], sb_q_gain[i],
                               sb_k_gain[i], pool_w[i], pool_scale[i], w_out_even[i])
        else:
            i = layer // 2
            x = x + odd_mixer(rms_norm(x, mix_norm_odd[i]), w_in_odd[i], gla_w_a2[i],
                              gla_b_a[i], gla_o_gain[i], w_out_odd[i])
        x = x + conv_ffn(rms_norm(x, ffn_norm[layer]), ffn_w_up[layer], ffn_conv_w[layer],
                         ffn_conv_b[layer], ffn_w_down[layer])
    return x
```

```python
import functools

import jax
import jax.numpy as jnp
from jax import lax
from jax.experimental import pallas as pl
from jax.experimental.pallas import tpu as pltpu

F32 = jnp.float32
BF16 = jnp.bfloat16

D_MODEL = 1024
NORM_EPS = 1e-6

SB_HEADS = 8
SB_HEAD_DIM = 64
SB_WIDTH = SB_HEADS * SB_HEAD_DIM
POOL_WINDOWS = (2, 4, 8, 16)
POOL_GROUP_DIM = 128
POOL_WIDTH = 512
POOL_HALO = 16

GLA_HEADS = 4
GLA_KEY_WIDTH = 512
GLA_VALUE_WIDTH = 1024
GLA_KEY_DIM = 128
GLA_VALUE_DIM = 256
GLA_GATE_RANK = 16
GLA_TAU = 16.0
GLA_CHUNK = 64
GLA_GATE_PAD = 128

FFN_HIDDEN = 2816
FFN_CHUNK = 256
FFN_NCHUNK = FFN_HIDDEN // FFN_CHUNK

IN_TM = 512
ATT_TQ = 256
ATT_TK = 128
OUT_TM = 512
FFN_TM = 512
GLA_TM = 256

VMEM_LIMIT = 56 * 1024 * 1024


def _rms(x, gain):
    ms = jnp.mean(x * x, axis=-1, keepdims=True)
    return x * lax.rsqrt(ms + NORM_EPS) * gain


def _log1pexp_neg_abs(z):
    return jnp.log(1.0 + jnp.exp(jnp.minimum(z, -z)))


def _split_bf16(x):
    hi = x.astype(BF16)
    lo = (x - hi.astype(F32)).astype(BF16)
    return hi, lo


def _even_in_kernel(x_ref, g_ref, w_ref, qg_ref, kg_ref, bd_ref,
                    q_ref, k_ref, v_ref, xb_ref):
    h = _rms(x_ref[...], g_ref[...]).astype(BF16)
    proj = jnp.dot(h, w_ref[...], preferred_element_type=F32)

    def head_norm(t, gain):
        ss = jnp.dot((t * t).astype(BF16), bd_ref[...], preferred_element_type=F32)
        return t * lax.rsqrt(ss * (1.0 / SB_HEAD_DIM) + NORM_EPS) * gain

    q = head_norm(proj[:, 0:SB_WIDTH], qg_ref[...])
    q_ref[...] = (q * (SB_HEAD_DIM ** -0.5)).astype(BF16)
    k_ref[...] = head_norm(proj[:, SB_WIDTH:2 * SB_WIDTH], kg_ref[...]).astype(BF16)
    v_ref[...] = proj[:, 2 * SB_WIDTH:3 * SB_WIDTH].astype(BF16)
    xb_ref[...] = proj[:, 3 * SB_WIDTH:]


def _even_in(x2d, gain, w_in, q_gain, k_gain):
    n = x2d.shape[0]
    tm = min(IN_TM, n)
    head = jnp.arange(SB_WIDTH) // SB_HEAD_DIM
    blockdiag = (head[:, None] == head[None, :]).astype(BF16)
    width = w_in.shape[1]
    const = lambda i: (0, 0)
    tile = lambda i: (i, 0)
    return pl.pallas_call(
        _even_in_kernel,
        grid=(n // tm,),
        in_specs=[
            pl.BlockSpec((tm, D_MODEL), tile),
            pl.BlockSpec((1, D_MODEL), const),
            pl.BlockSpec((D_MODEL, width), const),
            pl.BlockSpec((1, SB_WIDTH), const),
            pl.BlockSpec((1, SB_WIDTH), const),
            pl.BlockSpec((SB_WIDTH, SB_WIDTH), const),
        ],
        out_specs=[
            pl.BlockSpec((tm, SB_WIDTH), tile),
            pl.BlockSpec((tm, SB_WIDTH), tile),
            pl.BlockSpec((tm, SB_WIDTH), tile),
            pl.BlockSpec((tm, POOL_WIDTH), tile),
        ],
        out_shape=[
            jax.ShapeDtypeStruct((n, SB_WIDTH), BF16),
            jax.ShapeDtypeStruct((n, SB_WIDTH), BF16),
            jax.ShapeDtypeStruct((n, SB_WIDTH), BF16),
            jax.ShapeDtypeStruct((n, POOL_WIDTH), F32),
        ],
        compiler_params=pltpu.CompilerParams(
            dimension_semantics=("arbitrary",), vmem_limit_bytes=VMEM_LIMIT),
        name="even_in",
    )(x2d, gain.reshape(1, D_MODEL), w_in.astype(BF16),
      jnp.tile(q_gain, SB_HEADS).reshape(1, SB_WIDTH),
      jnp.tile(k_gain, SB_HEADS).reshape(1, SB_WIDTH), blockdiag)


def _attn_kernel(q_ref, k_ref, v_ref, m2_ref, o_ref, *, tq, tk):
    i = pl.program_id(2)
    q = q_ref[0]
    lane = lax.broadcasted_iota(jnp.int32, (tk, 2 * SB_HEAD_DIM), 1)
    first_head = lane < SB_HEAD_DIM
    row = lax.broadcasted_iota(jnp.int32, (tq, tk), 0)
    col = lax.broadcasted_iota(jnp.int32, (tq, tk), 1)
    zero = jnp.zeros((), BF16)

    def stack_heads(blk):
        return jnp.concatenate(
            [jnp.where(first_head, blk, zero), jnp.where(first_head, zero, blk)], axis=0)

    def step(j, carry, masked):
        c0, c1, acc = carry
        start = pl.multiple_of(j * tk, tk)
        kst = stack_heads(k_ref[0, pl.ds(start, tk), :])
        vst = stack_heads(v_ref[0, pl.ds(start, tk), :])
        z2 = lax.dot_general(q, kst, (((1,), (1,)), ((), ())),
                             preferred_element_type=F32)
        if masked:
            visible = (start + col) < (i * tq + row)
        ws, cs = [], []
        for hh, c in ((0, c0), (1, c1)):
            z = z2[:, hh * tk:(hh + 1) * tk]
            l = _log1pexp_neg_abs(z)
            log_fail = jnp.minimum(-z, 0.0) - l
            log_beta = jnp.minimum(z, 0.0) - l
            if masked:
                log_fail = jnp.where(visible, log_fail, 0.0)
            hi, lo = _split_bf16(log_fail)
            cum = jnp.dot(jnp.concatenate([hi, lo], axis=1), m2_ref[...],
                          preferred_element_type=F32)
            w = jnp.exp(log_beta + cum[:, :tk] + c)
            if masked:
                w = jnp.where(visible, w, 0.0)
            ws.append(w.astype(BF16))
            cs.append(c + cum[:, tk:])
        acc = acc + jnp.dot(jnp.concatenate(ws, axis=1), vst, preferred_element_type=F32)
        return cs[0], cs[1], acc

    ratio = tq // tk
    carry = (jnp.zeros((tq, tk), F32), jnp.zeros((tq, tk), F32),
             jnp.zeros((tq, 2 * SB_HEAD_DIM), F32))
    for d in range(ratio):
        carry = step(i * ratio + (ratio - 1 - d), carry, True)

    def body(jj, carry):
        return step(i * ratio - 1 - jj, carry, False)

    carry = lax.fori_loop(0, i * ratio, body, carry)
    o_ref[0] = carry[2].astype(BF16)


def _attention(q, k, v, batch, seq):
    tq = min(ATT_TQ, seq)
    tk = min(ATT_TK, tq)
    q3 = q.reshape(batch, seq, SB_WIDTH)
    k3 = k.reshape(batch, seq, SB_WIDTH)
    v3 = v.reshape(batch, seq, SB_WIDTH)
    r = jnp.arange(2 * tk) % tk
    ccol = jnp.arange(2 * tk)
    m2 = jnp.where(ccol[None, :] < tk, r[:, None] > ccol[None, :], True).astype(BF16)
    pair = 2 * SB_HEAD_DIM
    return pl.pallas_call(
        functools.partial(_attn_kernel, tq=tq, tk=tk),
        grid=(batch, SB_WIDTH // pair, seq // tq),
        in_specs=[
            pl.BlockSpec((1, tq, pair), lambda b, p, i: (b, i, p)),
            pl.BlockSpec((1, seq, pair), lambda b, p, i: (b, 0, p)),
            pl.BlockSpec((1, seq, pair), lambda b, p, i: (b, 0, p)),
            pl.BlockSpec((2 * tk, 2 * tk), lambda b, p, i: (0, 0)),
        ],
        out_specs=pl.BlockSpec((1, tq, pair), lambda b, p, i: (b, i, p)),
        out_shape=jax.ShapeDtypeStruct((batch, seq, SB_WIDTH), BF16),
        compiler_params=pltpu.CompilerParams(
            dimension_semantics=("arbitrary", "arbitrary", "arbitrary"),
            vmem_limit_bytes=VMEM_LIMIT),
        name="sb_attention",
    )(q3, k3, v3, m2)


def _even_out_kernel(xb_ref, halo_ref, oa_ref, x_ref, pw_ref, ps_ref, wo_ref, o_ref, *, tm):
    i = pl.program_id(1)
    xb = xb_ref[0]
    halo = jnp.where(i > 0, halo_ref[0], 0.0)
    ext = jnp.concatenate([halo, xb], axis=0)
    pos = (i * tm + 1 + lax.broadcasted_iota(jnp.int32, (tm, 1), 0)).astype(F32)
    y = jnp.dot(oa_ref[0], wo_ref[0:SB_WIDTH, :], preferred_element_type=F32)
    pooled_out = []
    for g, w in enumerate(POOL_WINDOWS):
        lanes = slice(g * POOL_GROUP_DIM, (g + 1) * POOL_GROUP_DIM)
        s = ext[:, lanes]
        sh = 1
        while sh < w:
            s = s + pltpu.roll(s, sh, axis=0)
            sh *= 2
        window_sum = s[POOL_HALO:, :]
        pooled = window_sum / jnp.minimum(pos, float(w)) - xb[:, lanes]
        ob = jnp.dot(pooled.astype(BF16), pw_ref[g], preferred_element_type=F32)
        pooled_out.append((ob * ps_ref[:, lanes]).astype(BF16))
    y = y + jnp.dot(jnp.concatenate(pooled_out, axis=1), wo_ref[SB_WIDTH:, :],
                    preferred_element_type=F32)
    o_ref[0] = x_ref[0] + y


def _even_out(xb, o_a, x, pool_w, pool_scale, w_out):
    batch, seq, _ = x.shape
    tm = min(OUT_TM, seq)
    xb3 = xb.reshape(batch, seq, POOL_WIDTH)
    hb = tm // POOL_HALO
    const2 = lambda b, i: (0, 0)
    tile = lambda b, i: (b, i, 0)
    return pl.pallas_call(
        functools.partial(_even_out_kernel, tm=tm),
        grid=(batch, seq // tm),
        in_specs=[
            pl.BlockSpec((1, tm, POOL_WIDTH), tile),
            pl.BlockSpec((1, POOL_HALO, POOL_WIDTH),
                         lambda b, i: (b, jnp.maximum(i * hb - 1, 0), 0)),
            pl.BlockSpec((1, tm, SB_WIDTH), tile),
            pl.BlockSpec((1, tm, D_MODEL), tile),
            pl.BlockSpec((len(POOL_WINDOWS), POOL_GROUP_DIM, POOL_GROUP_DIM),
                         lambda b, i: (0, 0, 0)),
            pl.BlockSpec((1, POOL_WIDTH), const2),
            pl.BlockSpec((D_MODEL, D_MODEL), const2),
        ],
        out_specs=pl.BlockSpec((1, tm, D_MODEL), tile),
        out_shape=jax.ShapeDtypeStruct((batch, seq, D_MODEL), F32),
        compiler_params=pltpu.CompilerParams(
            dimension_semantics=("arbitrary", "arbitrary"), vmem_limit_bytes=VMEM_LIMIT),
        name="even_out",
    )(xb3, xb3, o_a, x, pool_w.astype(BF16), pool_scale.reshape(1, POOL_WIDTH),
      w_out.astype(BF16))


def _ffn_kernel(x_ref, g_ref, wag_ref, cw_ref, wd_ref, o_ref,
                h_ref, carry_ref, ubuf_ref, acc_ref, *, tm):
    i = pl.program_id(1)

    @pl.when(i == 0)
    def _():
        carry_ref[...] = jnp.zeros_like(carry_ref)

    x = x_ref[0]
    h_ref[...] = _rms(x, g_ref[...]).astype(BF16)
    acc_ref[...] = x

    def chunk(c, _):
        u = jnp.dot(h_ref[...], wag_ref[c], preferred_element_type=F32)
        ubuf_ref[0:8, :] = carry_ref[c]
        ubuf_ref[8:8 + tm, :] = u
        carry_ref[c] = u[tm - 8:, :]
        cw = cw_ref[c]
        conv = (cw[3:4, :] + cw[2:3, :] * u
                + cw[1:2, :] * ubuf_ref[7:7 + tm, :]
                + cw[0:1, :] * ubuf_ref[6:6 + tm, :])
        a = conv[:, :FFN_CHUNK]
        gate = conv[:, FFN_CHUNK:]
        act = (a * jax.nn.sigmoid(a) * gate).astype(BF16)
        acc_ref[...] += jnp.dot(act, wd_ref[c], preferred_element_type=F32)
        return 0

    lax.fori_loop(0, FFN_NCHUNK, chunk, 0)
    o_ref[0] = acc_ref[...]


def _ffn(x, gain, w_up, conv_w, conv_b, w_down):
    batch, seq, _ = x.shape
    tm = min(FFN_TM, seq)
    ck, nch = FFN_CHUNK, FFN_NCHUNK

    def chunked(t):
        a = t[..., :FFN_HIDDEN].reshape(t.shape[:-1] + (nch, ck))
        g = t[..., FFN_HIDDEN:].reshape(t.shape[:-1] + (nch, ck))
        return jnp.moveaxis(jnp.concatenate([a, g], axis=-1), -2, 0)

    wag = chunked(w_up.astype(BF16))
    taps = jnp.concatenate([conv_w, conv_b[None, :],
                            jnp.zeros((4, 2 * FFN_HIDDEN), F32)], axis=0)
    cw = chunked(taps)
    wd = w_down.astype(BF16).reshape(nch, ck, D_MODEL)
    tile = lambda b, i: (b, i, 0)
    const3 = lambda b, i: (0, 0, 0)
    return pl.pallas_call(
        functools.partial(_ffn_kernel, tm=tm),
        grid=(batch, seq // tm),
        in_specs=[
            pl.BlockSpec((1, tm, D_MODEL), tile),
            pl.BlockSpec((1, D_MODEL), lambda b, i: (0, 0)),
            pl.BlockSpec((nch, D_MODEL, 2 * ck), const3),
            pl.BlockSpec((nch, 8, 2 * ck), const3),
            pl.BlockSpec((nch, ck, D_MODEL), const3),
        ],
        out_specs=pl.BlockSpec((1, tm, D_MODEL), tile),
        out_shape=jax.ShapeDtypeStruct((batch, seq, D_MODEL), F32),
        scratch_shapes=[
            pltpu.VMEM((tm, D_MODEL), BF16),
            pltpu.VMEM((nch, 8, 2 * ck), F32),
            pltpu.VMEM((tm + 8, 2 * ck), F32),
            pltpu.VMEM((tm, D_MODEL), F32),
        ],
        compiler_params=pltpu.CompilerParams(
            dimension_semantics=("arbitrary", "arbitrary"), vmem_limit_bytes=VMEM_LIMIT),
        name="conv_ffn",
    )(x, gain.reshape(1, D_MODEL), wag, cw, wd)


def _gla_kernel(x_ref, g_ref, win_ref, wa2_ref, ba_ref, og_ref, wout_ref, tri_ref,
                o_ref, state_ref, *, tm):
    i = pl.program_id(1)

    @pl.when(i == 0)
    def _():
        state_ref[...] = jnp.zeros_like(state_ref)

    x = x_ref[0]
    h = _rms(x, g_ref[...]).astype(BF16)
    proj = jnp.dot(h, win_ref[...], preferred_element_type=F32)
    kw, vw = GLA_KEY_WIDTH, GLA_VALUE_WIDTH
    q = proj[:, 0:kw] * (GLA_KEY_DIM ** -0.5)
    k = proj[:, kw:2 * kw]
    v = proj[:, 2 * kw:2 * kw + vw]
    r = proj[:, 2 * kw + vw:2 * kw + 2 * vw]
    a_low = proj[:, 2 * kw + 2 * vw:]
    gate = jnp.dot(a_low.astype(BF16), wa2_ref[...], preferred_element_type=F32) + ba_ref[...]
    log_alpha = (jnp.minimum(gate, 0.0) - _log1pexp_neg_abs(gate)) * (1.0 / GLA_TAU)
    hi, lo = _split_bf16(log_alpha)
    tri = tri_ref[...]
    cum = (jnp.dot(tri, hi, preferred_element_type=F32)
           + jnp.dot(tri, lo, preferred_element_type=F32))

    outs = []
    for c in range(tm // GLA_CHUNK):
        rows = slice(c * GLA_CHUNK, (c + 1) * GLA_CHUNK)
        cum_c = cum[rows, :]
        total = cum_c[GLA_CHUNK - 1:GLA_CHUNK, :]
        k_dec = k[rows, :] * jnp.exp(total - cum_c)
        decay = jnp.broadcast_to(jnp.exp(total), (GLA_CHUNK, kw))
        kd = jnp.concatenate([k_dec, decay], axis=0)
        q_c = q[rows, :].astype(BF16)
        v_c = v[rows, :].astype(BF16)
        heads = []
        for hd in range(GLA_HEADS):
            klanes = slice(hd * GLA_KEY_DIM, (hd + 1) * GLA_KEY_DIM)
            vlanes = slice(hd * GLA_VALUE_DIM, (hd + 1) * GLA_VALUE_DIM)
            kd_t = kd[:, klanes].T
            st = (state_ref[hd] * kd_t[:, GLA_CHUNK:GLA_CHUNK + 1]
                  + jnp.dot(kd_t[:, :GLA_CHUNK].astype(BF16), v_c[:, vlanes],
                            preferred_element_type=F32))
            state_ref[hd] = st
            heads.append(jnp.dot(q_c[:, klanes], st.astype(BF16),
                                 preferred_element_type=F32))
        outs.append(jnp.concatenate(heads, axis=1))
    o = jnp.concatenate(outs, axis=0)
    normed = jnp.concatenate(
        [_rms(o[:, hd * GLA_VALUE_DIM:(hd + 1) * GLA_VALUE_DIM], og_ref[...])
         for hd in range(GLA_HEADS)], axis=1)
    gated = (normed * (r * jax.nn.sigmoid(r))).astype(BF16)
    o_ref[0] = x + jnp.dot(gated, wout_ref[...], preferred_element_type=F32)


def _gla_mixer(x, gain, w_in, w_a2, b_a, o_gain, w_out):
    batch, seq, _ = x.shape
    tm = min(GLA_TM, seq)
    in_width = 2 * GLA_KEY_WIDTH + 2 * GLA_VALUE_WIDTH + GLA_GATE_PAD
    pad = GLA_GATE_PAD - GLA_GATE_RANK
    w_in_p = jnp.pad(w_in, ((0, 0), (0, pad))).astype(BF16)
    w_a2_p = jnp.pad(w_a2, ((0, pad), (0, 0))).astype(BF16)
    t = jnp.arange(tm)
    tri = ((t[:, None] // GLA_CHUNK == t[None, :] // GLA_CHUNK)
           & (t[:, None] >= t[None, :])).astype(BF16)
    tile = lambda b, i: (b, i, 0)
    const2 = lambda b, i: (0, 0)
    return pl.pallas_call(
        functools.partial(_gla_kernel, tm=tm),
        grid=(batch, seq // tm),
        in_specs=[
            pl.BlockSpec((1, tm, D_MODEL), tile),
            pl.BlockSpec((1, D_MODEL), const2),
            pl.BlockSpec((D_MODEL, in_width), const2),
            pl.BlockSpec((GLA_GATE_PAD, GLA_KEY_WIDTH), const2),
            pl.BlockSpec((1, GLA_KEY_WIDTH), const2),
            pl.BlockSpec((1, GLA_VALUE_DIM), const2),
            pl.BlockSpec((GLA_VALUE_WIDTH, D_MODEL), const2),
            pl.BlockSpec((tm, tm), const2),
        ],
        out_specs=pl.BlockSpec((1, tm, D_MODEL), tile),
        out_shape=jax.ShapeDtypeStruct((batch, seq, D_MODEL), F32),
        scratch_shapes=[pltpu.VMEM((GLA_HEADS, GLA_KEY_DIM, GLA_VALUE_DIM), F32)],
        compiler_params=pltpu.CompilerParams(
            dimension_semantics=("arbitrary", "arbitrary"), vmem_limit_bytes=VMEM_LIMIT),
        name="gla_mixer",
    )(x, gain.reshape(1, D_MODEL), w_in_p, w_a2_p, b_a.reshape(1, GLA_KEY_WIDTH),
      o_gain.reshape(1, GLA_VALUE_DIM), w_out.astype(BF16), tri)


def kernel(x, mix_norm_even, w_in_even, sb_q_gain, sb_k_gain, pool_w, pool_scale, w_out_even,
           mix_norm_odd, w_in_odd, gla_w_a2, gla_b_a, gla_o_gain, w_out_odd,
           ffn_norm, ffn_w_up, ffn_conv_w, ffn_conv_b, ffn_w_down):
    batch, seq, _ = x.shape
    depth = ffn_norm.shape[0]
    for layer in range(depth):
        i = layer // 2
        if layer % 2 == 0:
            q, k, v, xb = _even_in(x.reshape(batch * seq, D_MODEL), mix_norm_even[i],
                                   w_in_even[i], sb_q_gain[i], sb_k_gain[i])
            o_a = _attention(q, k, v, batch, seq)
            x = _even_out(xb, o_a, x, pool_w[i], pool_scale[i], w_out_even[i])
        else:
            x = _gla_mixer(x, mix_norm_odd[i], w_in_odd[i], gla_w_a2[i], gla_b_a[i],
                           gla_o_gain[i], w_out_odd[i])
        x = _ffn(x, ffn_norm[layer], ffn_w_up[layer], ffn_conv_w[layer],
                 ffn_conv_b[layer], ffn_w_down[layer])
    return x
```

```python
import functools

import jax
import jax.numpy as jnp
from jax import lax
from jax.experimental import pallas as pl
from jax.experimental.pallas import tpu as pltpu

F32 = jnp.float32
BF16 = jnp.bfloat16

D_MODEL = 1024
NORM_EPS = 1e-6
LOG2E = 1.4426950408889634

SB_HEADS = 8
SB_HEAD_DIM = 64
SB_WIDTH = SB_HEADS * SB_HEAD_DIM
POOL_WINDOWS = (2, 4, 8, 16)
POOL_GROUP_DIM = 128
POOL_WIDTH = 512
POOL_HALO = 16

GLA_HEADS = 4
GLA_KEY_WIDTH = 512
GLA_VALUE_WIDTH = 1024
GLA_KEY_DIM = 128
GLA_VALUE_DIM = 256
GLA_GATE_RANK = 16
GLA_TAU = 16.0
GLA_CHUNK = 64
GLA_GATE_PAD = 128

FFN_HIDDEN = 2816
FFN_CHUNK = 256
FFN_NCHUNK = FFN_HIDDEN // FFN_CHUNK

IN_TM = 512
ATT_TQ = 512
ATT_TK = 128
ATT_NB = 2
OUT_TM = 512
FFN_TM = 512
GLA_TM = 256

VMEM_LIMIT = 56 * 1024 * 1024


def _rms(x, gain):
    ms = jnp.mean(x * x, axis=-1, keepdims=True)
    return x * lax.rsqrt(ms + NORM_EPS) * gain


def _log1pexp_neg_abs(z):
    return jnp.log(1.0 + jnp.exp(jnp.minimum(z, -z)))


def _split_bf16(x):
    hi = x.astype(BF16)
    lo = (x - hi.astype(F32)).astype(BF16)
    return hi, lo


def _even_in_kernel(x_ref, g_ref, w_ref, qg_ref, kg_ref, bd_ref,
                    q_ref, k_ref, v_ref, xb_ref):
    h = _rms(x_ref[...], g_ref[...]).astype(BF16)
    proj = jnp.dot(h, w_ref[...], preferred_element_type=F32)

    def head_norm(t, gain):
        ss = jnp.dot((t * t).astype(BF16), bd_ref[...], preferred_element_type=F32)
        return t * lax.rsqrt(ss * (1.0 / SB_HEAD_DIM) + NORM_EPS) * gain

    q = head_norm(proj[:, 0:SB_WIDTH], qg_ref[...])
    q_ref[...] = (q * (SB_HEAD_DIM ** -0.5 * LOG2E)).astype(BF16)
    k_ref[...] = head_norm(proj[:, SB_WIDTH:2 * SB_WIDTH], kg_ref[...]).astype(BF16)
    v_ref[...] = proj[:, 2 * SB_WIDTH:3 * SB_WIDTH].astype(BF16)
    xb_ref[...] = proj[:, 3 * SB_WIDTH:]


def _even_in(x2d, gain, w_in, q_gain, k_gain):
    n = x2d.shape[0]
    tm = min(IN_TM, n)
    head = jnp.arange(SB_WIDTH) // SB_HEAD_DIM
    blockdiag = (head[:, None] == head[None, :]).astype(BF16)
    width = w_in.shape[1]
    const = lambda i: (0, 0)
    tile = lambda i: (i, 0)
    return pl.pallas_call(
        _even_in_kernel,
        grid=(n // tm,),
        in_specs=[
            pl.BlockSpec((tm, D_MODEL), tile),
            pl.BlockSpec((1, D_MODEL), const),
            pl.BlockSpec((D_MODEL, width), const),
            pl.BlockSpec((1, SB_WIDTH), const),
            pl.BlockSpec((1, SB_WIDTH), const),
            pl.BlockSpec((SB_WIDTH, SB_WIDTH), const),
        ],
        out_specs=[
            pl.BlockSpec((tm, SB_WIDTH), tile),
            pl.BlockSpec((tm, SB_WIDTH), tile),
            pl.BlockSpec((tm, SB_WIDTH), tile),
            pl.BlockSpec((tm, POOL_WIDTH), tile),
        ],
        out_shape=[
            jax.ShapeDtypeStruct((n, SB_WIDTH), BF16),
            jax.ShapeDtypeStruct((n, SB_WIDTH), BF16),
            jax.ShapeDtypeStruct((n, SB_WIDTH), BF16),
            jax.ShapeDtypeStruct((n, POOL_WIDTH), F32),
        ],
        compiler_params=pltpu.CompilerParams(
            dimension_semantics=("arbitrary",), vmem_limit_bytes=VMEM_LIMIT),
        name="even_in",
    )(x2d, gain.reshape(1, D_MODEL), w_in.astype(BF16),
      jnp.tile(q_gain, SB_HEADS).reshape(1, SB_WIDTH),
      jnp.tile(k_gain, SB_HEADS).reshape(1, SB_WIDTH), blockdiag)


def _attn_kernel(q_ref, k_ref, v_ref, m2_ref, o_ref, za_ref, zb_ref, wa_ref, wb_ref,
                 *, tq, tk, nb):
    i = pl.program_id(2)
    q = q_ref[0]
    lane = lax.broadcasted_iota(jnp.int32, (tk, 2 * SB_HEAD_DIM), 1)
    first_head = lane < SB_HEAD_DIM
    zero = jnp.zeros((), BF16)

    def stack_heads(blk):
        return jnp.concatenate(
            [jnp.where(first_head, blk, zero), jnp.where(first_head, zero, blk)], axis=0)

    def stacked(ref, start, n):
        return jnp.concatenate(
            [stack_heads(ref[0, pl.ds(start + u * tk, tk), :]) for u in range(n)], axis=0)

    def scores(q_rows, start, n):
        return lax.dot_general(q_rows, stacked(k_ref, start, n), (((1,), (1,)), ((), ())),
                               preferred_element_type=F32)

    def attend(w, start, n):
        return jnp.dot(w, stacked(v_ref, start, n), preferred_element_type=F32)

    def block_weights(z2, c0, c1, visible):
        ws, cs = [], []
        for hh, c in ((0, c0), (1, c1)):
            z = z2[:, hh * tk:(hh + 1) * tk]
            softplus = jnp.maximum(z, 0.0) + jnp.log2(1.0 + jnp.exp2(jnp.minimum(z, -z)))
            if visible is not None:
                softplus = jnp.where(visible, softplus, 0.0)
            hi, lo = _split_bf16(softplus)
            cum = jnp.dot(jnp.concatenate([hi, lo], axis=1), m2_ref[...],
                          preferred_element_type=F32)
            w = jnp.exp2(z + cum[:, :tk] + c)
            if visible is not None:
                w = jnp.where(visible, w, 0.0)
            ws.append(w.astype(BF16))
            cs.append(c + cum[:, tk:])
        return jnp.concatenate(ws, axis=1), cs[0], cs[1]

    ratio = tq // tk
    c0 = jnp.zeros((tq, tk), F32)
    c1 = jnp.zeros((tq, tk), F32)
    acc = jnp.zeros((tq, 2 * SB_HEAD_DIM), F32)
    for u in reversed(range(ratio)):
        r0 = u * tk
        start = pl.multiple_of(i * tq + r0, tk)
        row = lax.broadcasted_iota(jnp.int32, (tq - r0, tk), 0)
        col = lax.broadcasted_iota(jnp.int32, (tq - r0, tk), 1)
        w, p0, p1 = block_weights(scores(q[r0:, :], start, 1), c0[r0:, :], c1[r0:, :], col < row)
        pacc = acc[r0:, :] + attend(w, start, 1)
        if r0:
            c0 = jnp.concatenate([c0[:r0, :], p0], axis=0)
            c1 = jnp.concatenate([c1[:r0, :], p1], axis=0)
            acc = jnp.concatenate([acc[:r0, :], pacc], axis=0)
        else:
            c0, c1, acc = p0, p1, pacc

    ngroups = i * (ratio // nb)

    def group_start(g):
        return pl.multiple_of(jnp.maximum(i * ratio - (g + 1) * nb, 0) * tk, tk)

    def stage(g, z_in, z_out, w_in, w_out, carry):
        c0, c1, acc = carry
        acc = acc + attend(w_in[...], group_start(g - 1), nb)
        z_out[...] = scores(q, group_start(g + 1), nb)
        for u in reversed(range(nb)):
            cols = slice(u * 2 * tk, (u + 1) * 2 * tk)
            w, c0, c1 = block_weights(z_in[:, cols], c0, c1, None)
            w_out[:, cols] = w
        return c0, c1, acc

    wb_ref[...] = jnp.zeros_like(wb_ref)
    za_ref[...] = scores(q, group_start(0), nb)

    def body(t, carry):
        carry = stage(2 * t, za_ref, zb_ref, wb_ref, wa_ref, carry)
        return stage(2 * t + 1, zb_ref, za_ref, wa_ref, wb_ref, carry)

    c0, c1, acc = lax.fori_loop(0, ngroups // 2, body, (c0, c1, acc))
    acc = acc + attend(wb_ref[...], group_start(ngroups - 1), nb)
    o_ref[0] = acc.astype(BF16)


def _attention(q, k, v, batch, seq):
    tq = min(ATT_TQ, seq)
    tk = min(ATT_TK, tq)
    q3 = q.reshape(batch, seq, SB_WIDTH)
    k3 = k.reshape(batch, seq, SB_WIDTH)
    v3 = v.reshape(batch, seq, SB_WIDTH)
    nb = ATT_NB
    assert seq % tq == 0 and (tq // tk) % (2 * nb) == 0
    r = jnp.arange(2 * tk) % tk
    ccol = jnp.arange(2 * tk)
    m2 = -jnp.where(ccol[None, :] < tk, r[:, None] >= ccol[None, :], True).astype(BF16)
    pair = 2 * SB_HEAD_DIM
    return pl.pallas_call(
        functools.partial(_attn_kernel, tq=tq, tk=tk, nb=nb),
        grid=(batch, SB_WIDTH // pair, seq // tq),
        scratch_shapes=[
            pltpu.VMEM((tq, nb * 2 * tk), F32), pltpu.VMEM((tq, nb * 2 * tk), F32),
            pltpu.VMEM((tq, nb * 2 * tk), BF16), pltpu.VMEM((tq, nb * 2 * tk), BF16),
        ],
        in_specs=[
            pl.BlockSpec((1, tq, pair), lambda b, p, i: (b, i, p)),
            pl.BlockSpec((1, seq, pair), lambda b, p, i: (b, 0, p)),
            pl.BlockSpec((1, seq, pair), lambda b, p, i: (b, 0, p)),
            pl.BlockSpec((2 * tk, 2 * tk), lambda b, p, i: (0, 0)),
        ],
        out_specs=pl.BlockSpec((1, tq, pair), lambda b, p, i: (b, i, p)),
        out_shape=jax.ShapeDtypeStruct((batch, seq, SB_WIDTH), BF16),
        compiler_params=pltpu.CompilerParams(
            dimension_semantics=("arbitrary", "arbitrary", "arbitrary"),
            vmem_limit_bytes=VMEM_LIMIT),
        name="sb_attention",
    )(q3, k3, v3, m2)


def _even_out_kernel(xb_ref, halo_ref, oa_ref, x_ref, pw_ref, ps_ref, wo_ref, o_ref, *, tm):
    i = pl.program_id(1)
    xb = xb_ref[0]
    halo = jnp.where(i > 0, halo_ref[0], 0.0)
    ext = jnp.concatenate([halo, xb], axis=0)
    pos = (i * tm + 1 + lax.broadcasted_iota(jnp.int32, (tm, 1), 0)).astype(F32)
    y = jnp.dot(oa_ref[0], wo_ref[0:SB_WIDTH, :], preferred_element_type=F32)
    pooled_out = []
    for g, w in enumerate(POOL_WINDOWS):
        lanes = slice(g * POOL_GROUP_DIM, (g + 1) * POOL_GROUP_DIM)
        s = ext[:, lanes]
        sh = 1
        while sh < w:
            s = s + pltpu.roll(s, sh, axis=0)
            sh *= 2
        window_sum = s[POOL_HALO:, :]
        pooled = window_sum / jnp.minimum(pos, float(w)) - xb[:, lanes]
        ob = jnp.dot(pooled.astype(BF16), pw_ref[g], preferred_element_type=F32)
        pooled_out.append((ob * ps_ref[:, lanes]).astype(BF16))
    y = y + jnp.dot(jnp.concatenate(pooled_out, axis=1), wo_ref[SB_WIDTH:, :],
                    preferred_element_type=F32)
    o_ref[0] = x_ref[0] + y


def _even_out(xb, o_a, x, pool_w, pool_scale, w_out):
    batch, seq, _ = x.shape
    tm = min(OUT_TM, seq)
    xb3 = xb.reshape(batch, seq, POOL_WIDTH)
    hb = tm // POOL_HALO
    const2 = lambda b, i: (0, 0)
    tile = lambda b, i: (b, i, 0)
    return pl.pallas_call(
        functools.partial(_even_out_kernel, tm=tm),
        grid=(batch, seq // tm),
        in_specs=[
            pl.BlockSpec((1, tm, POOL_WIDTH), tile),
            pl.BlockSpec((1, POOL_HALO, POOL_WIDTH),
                         lambda b, i: (b, jnp.maximum(i * hb - 1, 0), 0)),
            pl.BlockSpec((1, tm, SB_WIDTH), tile),
            pl.BlockSpec((1, tm, D_MODEL), tile),
            pl.BlockSpec((len(POOL_WINDOWS), POOL_GROUP_DIM, POOL_GROUP_DIM),
                         lambda b, i: (0, 0, 0)),
            pl.BlockSpec((1, POOL_WIDTH), const2),
            pl.BlockSpec((D_MODEL, D_MODEL), const2),
        ],
        out_specs=pl.BlockSpec((1, tm, D_MODEL), tile),
        out_shape=jax.ShapeDtypeStruct((batch, seq, D_MODEL), F32),
        compiler_params=pltpu.CompilerParams(
            dimension_semantics=("arbitrary", "arbitrary"), vmem_limit_bytes=VMEM_LIMIT),
        name="even_out",
    )(xb3, xb3, o_a, x, pool_w.astype(BF16), pool_scale.reshape(1, POOL_WIDTH),
      w_out.astype(BF16))


def _ffn_kernel(x_ref, g_ref, wag_ref, cw_ref, wd_ref, o_ref,
                h_ref, carry_ref, ubuf_ref, acc_ref, *, tm):
    i = pl.program_id(1)

    @pl.when(i == 0)
    def _():
        carry_ref[...] = jnp.zeros_like(carry_ref)

    x = x_ref[0]
    h_ref[...] = _rms(x, g_ref[...]).astype(BF16)
    acc_ref[...] = x

    def chunk(c, _):
        u = jnp.dot(h_ref[...], wag_ref[c], preferred_element_type=F32)
        ubuf_ref[0:8, :] = carry_ref[c]
        ubuf_ref[8:8 + tm, :] = u
        carry_ref[c] = u[tm - 8:, :]
        cw = cw_ref[c]
        conv = (cw[3:4, :] + cw[2:3, :] * u
                + cw[1:2, :] * ubuf_ref[7:7 + tm, :]
                + cw[0:1, :] * ubuf_ref[6:6 + tm, :])
        a = conv[:, :FFN_CHUNK]
        gate = conv[:, FFN_CHUNK:]
        act = (a * jax.nn.sigmoid(a) * gate).astype(BF16)
        acc_ref[...] += jnp.dot(act, wd_ref[c], preferred_element_type=F32)
        return 0

    lax.fori_loop(0, FFN_NCHUNK, chunk, 0)
    o_ref[0] = acc_ref[...]


def _ffn(x, gain, w_up, conv_w, conv_b, w_down):
    batch, seq, _ = x.shape
    tm = min(FFN_TM, seq)
    ck, nch = FFN_CHUNK, FFN_NCHUNK

    def chunked(t):
        a = t[..., :FFN_HIDDEN].reshape(t.shape[:-1] + (nch, ck))
        g = t[..., FFN_HIDDEN:].reshape(t.shape[:-1] + (nch, ck))
        return jnp.moveaxis(jnp.concatenate([a, g], axis=-1), -2, 0)

    wag = chunked(w_up.astype(BF16))
    taps = jnp.concatenate([conv_w, conv_b[None, :],
                            jnp.zeros((4, 2 * FFN_HIDDEN), F32)], axis=0)
    cw = chunked(taps)
    wd = w_down.astype(BF16).reshape(nch, ck, D_MODEL)
    tile = lambda b, i: (b, i, 0)
    const3 = lambda b, i: (0, 0, 0)
    return pl.pallas_call(
        functools.partial(_ffn_kernel, tm=tm),
        grid=(batch, seq // tm),
        in_specs=[
            pl.BlockSpec((1, tm, D_MODEL), tile),
            pl.BlockSpec((1, D_MODEL), lambda b, i: (0, 0)),
            pl.BlockSpec((nch, D_MODEL, 2 * ck), const3),
            pl.BlockSpec((nch, 8, 2 * ck), const3),
            pl.BlockSpec((nch, ck, D_MODEL), const3),
        ],
        out_specs=pl.BlockSpec((1, tm, D_MODEL), tile),
        out_shape=jax.ShapeDtypeStruct((batch, seq, D_MODEL), F32),
        scratch_shapes=[
            pltpu.VMEM((tm, D_MODEL), BF16),
            pltpu.VMEM((nch, 8, 2 * ck), F32),
            pltpu.VMEM((tm + 8, 2 * ck), F32),
            pltpu.VMEM((tm, D_MODEL), F32),
        ],
        compiler_params=pltpu.CompilerParams(
            dimension_semantics=("arbitrary", "arbitrary"), vmem_limit_bytes=VMEM_LIMIT),
        name="conv_ffn",
    )(x, gain.reshape(1, D_MODEL), wag, cw, wd)


def _gla_kernel(x_ref, g_ref, win_ref, wa2_ref, ba_ref, og_ref, wout_ref, tri_ref,
                o_ref, state_ref, *, tm):
    i = pl.program_id(1)

    @pl.when(i == 0)
    def _():
        state_ref[...] = jnp.zeros_like(state_ref)

    x = x_ref[0]
    h = _rms(x, g_ref[...]).astype(BF16)
    proj = jnp.dot(h, win_ref[...], preferred_element_type=F32)
    kw, vw = GLA_KEY_WIDTH, GLA_VALUE_WIDTH
    q = proj[:, 0:kw] * (GLA_KEY_DIM ** -0.5)
    k = proj[:, kw:2 * kw]
    v = proj[:, 2 * kw:2 * kw + vw]
    r = proj[:, 2 * kw + vw:2 * kw + 2 * vw]
    a_low = proj[:, 2 * kw + 2 * vw:]
    gate = jnp.dot(a_low.astype(BF16), wa2_ref[...], preferred_element_type=F32) + ba_ref[...]
    log_alpha = (jnp.minimum(gate, 0.0) - _log1pexp_neg_abs(gate)) * (1.0 / GLA_TAU)
    hi, lo = _split_bf16(log_alpha)
    tri = tri_ref[...]
    cum = (jnp.dot(tri, hi, preferred_element_type=F32)
           + jnp.dot(tri, lo, preferred_element_type=F32))

    outs = []
    for c in range(tm // GLA_CHUNK):
        rows = slice(c * GLA_CHUNK, (c + 1) * GLA_CHUNK)
        cum_c = cum[rows, :]
        total = cum_c[GLA_CHUNK - 1:GLA_CHUNK, :]
        k_dec = k[rows, :] * jnp.exp(total - cum_c)
        decay = jnp.broadcast_to(jnp.exp(total), (GLA_CHUNK, kw))
        kd = jnp.concatenate([k_dec, decay], axis=0)
        q_c = q[rows, :].astype(BF16)
        v_c = v[rows, :].astype(BF16)
        heads = []
        for hd in range(GLA_HEADS):
            klanes = slice(hd * GLA_KEY_DIM, (hd + 1) * GLA_KEY_DIM)
            vlanes = slice(hd * GLA_VALUE_DIM, (hd + 1) * GLA_VALUE_DIM)
            kd_t = kd[:, klanes].T
            st = (state_ref[hd] * kd_t[:, GLA_CHUNK:GLA_CHUNK + 1]
                  + jnp.dot(kd_t[:, :GLA_CHUNK].astype(BF16), v_c[:, vlanes],
                            preferred_element_type=F32))
            state_ref[hd] = st
            heads.append(jnp.dot(q_c[:, klanes], st.astype(BF16),
                                 preferred_element_type=F32))
        outs.append(jnp.concatenate(heads, axis=1))
    o = jnp.concatenate(outs, axis=0)
    normed = jnp.concatenate(
        [_rms(o[:, hd * GLA_VALUE_DIM:(hd + 1) * GLA_VALUE_DIM], og_ref[...])
         for hd in range(GLA_HEADS)], axis=1)
    gated = (normed * (r * jax.nn.sigmoid(r))).astype(BF16)
    o_ref[0] = x + jnp.dot(gated, wout_ref[...], preferred_element_type=F32)


def _gla_mixer(x, gain, w_in, w_a2, b_a, o_gain, w_out):
    batch, seq, _ = x.shape
    tm = min(GLA_TM, seq)
    in_width = 2 * GLA_KEY_WIDTH + 2 * GLA_VALUE_WIDTH + GLA_GATE_PAD
    pad = GLA_GATE_PAD - GLA_GATE_RANK
    w_in_p = jnp.pad(w_in, ((0, 0), (0, pad))).astype(BF16)
    w_a2_p = jnp.pad(w_a2, ((0, pad), (0, 0))).astype(BF16)
    t = jnp.arange(tm)
    tri = ((t[:, None] // GLA_CHUNK == t[None, :] // GLA_CHUNK)
           & (t[:, None] >= t[None, :])).astype(BF16)
    tile = lambda b, i: (b, i, 0)
    const2 = lambda b, i: (0, 0)
    return pl.pallas_call(
        functools.partial(_gla_kernel, tm=tm),
        grid=(batch, seq // tm),
        in_specs=[
            pl.BlockSpec((1, tm, D_MODEL), tile),
            pl.BlockSpec((1, D_MODEL), const2),
            pl.BlockSpec((D_MODEL, in_width), const2),
            pl.BlockSpec((GLA_GATE_PAD, GLA_KEY_WIDTH), const2),
            pl.BlockSpec((1, GLA_KEY_WIDTH), const2),
            pl.BlockSpec((1, GLA_VALUE_DIM), const2),
            pl.BlockSpec((GLA_VALUE_WIDTH, D_MODEL), const2),
            pl.BlockSpec((tm, tm), const2),
        ],
        out_specs=pl.BlockSpec((1, tm, D_MODEL), tile),
        out_shape=jax.ShapeDtypeStruct((batch, seq, D_MODEL), F32),
        scratch_shapes=[pltpu.VMEM((GLA_HEADS, GLA_KEY_DIM, GLA_VALUE_DIM), F32)],
        compiler_params=pltpu.CompilerParams(
            dimension_semantics=("arbitrary", "arbitrary"), vmem_limit_bytes=VMEM_LIMIT),
        name="gla_mixer",
    )(x, gain.reshape(1, D_MODEL), w_in_p, w_a2_p, b_a.reshape(1, GLA_KEY_WIDTH),
      o_gain.reshape(1, GLA_VALUE_DIM), w_out.astype(BF16), tri)


def kernel(x, mix_norm_even, w_in_even, sb_q_gain, sb_k_gain, pool_w, pool_scale, w_out_even,
           mix_norm_odd, w_in_odd, gla_w_a2, gla_b_a, gla_o_gain, w_out_odd,
           ffn_norm, ffn_w_up, ffn_conv_w, ffn_conv_b, ffn_w_down):
    batch, seq, _ = x.shape
    depth = ffn_norm.shape[0]
    for layer in range(depth):
        i = layer // 2
        if layer % 2 == 0:
            q, k, v, xb = _even_in(x.reshape(batch * seq, D_MODEL), mix_norm_even[i],
                                   w_in_even[i], sb_q_gain[i], sb_k_gain[i])
            o_a = _attention(q, k, v, batch, seq)
            x = _even_out(xb, o_a, x, pool_w[i], pool_scale[i], w_out_even[i])
        else:
            x = _gla_mixer(x, mix_norm_odd[i], w_in_odd[i], gla_w_a2[i], gla_b_a[i],
                           gla_o_gain[i], w_out_odd[i])
        x = _ffn(x, ffn_norm[layer], ffn_w_up[layer], ffn_conv_w[layer],
                 ffn_conv_b[layer], ffn_w_down[layer])
    return x
```

```python
import functools

import jax
import jax.numpy as jnp
from jax import lax
from jax.experimental import pallas as pl
from jax.experimental.pallas import tpu as pltpu

F32 = jnp.float32
BF16 = jnp.bfloat16

D_MODEL = 1024
NORM_EPS = 1e-6
LOG2E = 1.4426950408889634

SB_HEADS = 8
SB_HEAD_DIM = 64
SB_WIDTH = SB_HEADS * SB_HEAD_DIM
POOL_WINDOWS = (2, 4, 8, 16)
POOL_GROUP_DIM = 128
POOL_WIDTH = 512
POOL_HALO = 16

GLA_HEADS = 4
GLA_KEY_WIDTH = 512
GLA_VALUE_WIDTH = 1024
GLA_KEY_DIM = 128
GLA_VALUE_DIM = 256
GLA_GATE_RANK = 16
GLA_TAU = 16.0
GLA_CHUNK = 64
GLA_GATE_PAD = 128

FFN_HIDDEN = 2816
FFN_CHUNK = 256
FFN_NCHUNK = FFN_HIDDEN // FFN_CHUNK
assert FFN_NCHUNK % 2 == 1 and FFN_NCHUNK >= 3

IN_TM = 512
ATT_TQ = 512
ATT_TK = 128
ATT_NB = 2
OUT_TM = 512
FFN_TM = 512
GLA_TM = 256

VMEM_LIMIT = 56 * 1024 * 1024


def _rms(x, gain):
    ms = jnp.mean(x * x, axis=-1, keepdims=True)
    return x * lax.rsqrt(ms + NORM_EPS) * gain


def _log1pexp_neg_abs(z):
    return jnp.log(1.0 + jnp.exp(jnp.minimum(z, -z)))


def _split_bf16(x):
    hi = x.astype(BF16)
    lo = (x - hi.astype(F32)).astype(BF16)
    return hi, lo


def _even_in_kernel(x_ref, g_ref, w_ref, qg_ref, kg_ref, bd_ref,
                    q_ref, k_ref, v_ref, xb_ref):
    h = _rms(x_ref[...], g_ref[...]).astype(BF16)
    proj = jnp.dot(h, w_ref[...], preferred_element_type=F32)

    def head_norm(t, gain):
        ss = jnp.dot((t * t).astype(BF16), bd_ref[...], preferred_element_type=F32)
        return t * lax.rsqrt(ss * (1.0 / SB_HEAD_DIM) + NORM_EPS) * gain

    q = head_norm(proj[:, 0:SB_WIDTH], qg_ref[...])
    q_ref[...] = (q * (SB_HEAD_DIM ** -0.5 * LOG2E)).astype(BF16)
    k_ref[...] = head_norm(proj[:, SB_WIDTH:2 * SB_WIDTH], kg_ref[...]).astype(BF16)
    v_ref[...] = proj[:, 2 * SB_WIDTH:3 * SB_WIDTH].astype(BF16)
    xb_ref[...] = proj[:, 3 * SB_WIDTH:]


def _even_in(x2d, gain, w_in, q_gain, k_gain):
    n = x2d.shape[0]
    tm = min(IN_TM, n)
    head = jnp.arange(SB_WIDTH) // SB_HEAD_DIM
    blockdiag = (head[:, None] == head[None, :]).astype(BF16)
    width = w_in.shape[1]
    const = lambda i: (0, 0)
    tile = lambda i: (i, 0)
    return pl.pallas_call(
        _even_in_kernel,
        grid=(n // tm,),
        in_specs=[
            pl.BlockSpec((tm, D_MODEL), tile),
            pl.BlockSpec((1, D_MODEL), const),
            pl.BlockSpec((D_MODEL, width), const),
            pl.BlockSpec((1, SB_WIDTH), const),
            pl.BlockSpec((1, SB_WIDTH), const),
            pl.BlockSpec((SB_WIDTH, SB_WIDTH), const),
        ],
        out_specs=[
            pl.BlockSpec((tm, SB_WIDTH), tile),
            pl.BlockSpec((tm, SB_WIDTH), tile),
            pl.BlockSpec((tm, SB_WIDTH), tile),
            pl.BlockSpec((tm, POOL_WIDTH), tile),
        ],
        out_shape=[
            jax.ShapeDtypeStruct((n, SB_WIDTH), BF16),
            jax.ShapeDtypeStruct((n, SB_WIDTH), BF16),
            jax.ShapeDtypeStruct((n, SB_WIDTH), BF16),
            jax.ShapeDtypeStruct((n, POOL_WIDTH), F32),
        ],
        compiler_params=pltpu.CompilerParams(
            dimension_semantics=("arbitrary",), vmem_limit_bytes=VMEM_LIMIT),
        name="even_in",
    )(x2d, gain.reshape(1, D_MODEL), w_in.astype(BF16),
      jnp.tile(q_gain, SB_HEADS).reshape(1, SB_WIDTH),
      jnp.tile(k_gain, SB_HEADS).reshape(1, SB_WIDTH), blockdiag)


def _attn_kernel(q_ref, k_ref, v_ref, m2_ref, o_ref, za_ref, zb_ref, wa_ref, wb_ref,
                 *, tq, tk, nb):
    i = pl.program_id(2)
    q = q_ref[0]
    lane = lax.broadcasted_iota(jnp.int32, (tk, 2 * SB_HEAD_DIM), 1)
    first_head = lane < SB_HEAD_DIM
    zero = jnp.zeros((), BF16)

    def stack_heads(blk):
        return jnp.concatenate(
            [jnp.where(first_head, blk, zero), jnp.where(first_head, zero, blk)], axis=0)

    def stacked(ref, start, n):
        return jnp.concatenate(
            [stack_heads(ref[0, pl.ds(start + u * tk, tk), :]) for u in range(n)], axis=0)

    def scores(q_rows, start, n):
        return lax.dot_general(q_rows, stacked(k_ref, start, n), (((1,), (1,)), ((), ())),
                               preferred_element_type=F32)

    def attend(w, start, n):
        return jnp.dot(w, stacked(v_ref, start, n), preferred_element_type=F32)

    def block_weights(z2, c0, c1, visible):
        ws, cs = [], []
        for hh, c in ((0, c0), (1, c1)):
            z = z2[:, hh * tk:(hh + 1) * tk]
            softplus = jnp.maximum(z, 0.0) + jnp.log2(1.0 + jnp.exp2(jnp.minimum(z, -z)))
            if visible is not None:
                softplus = jnp.where(visible, softplus, 0.0)
            hi, lo = _split_bf16(softplus)
            cum = jnp.dot(jnp.concatenate([hi, lo], axis=1), m2_ref[...],
                          preferred_element_type=F32)
            w = jnp.exp2(z + cum[:, :tk] + c)
            if visible is not None:
                w = jnp.where(visible, w, 0.0)
            ws.append(w.astype(BF16))
            cs.append(c + cum[:, tk:])
        return jnp.concatenate(ws, axis=1), cs[0], cs[1]

    ratio = tq // tk
    c0 = jnp.zeros((tq, tk), F32)
    c1 = jnp.zeros((tq, tk), F32)
    acc = jnp.zeros((tq, 2 * SB_HEAD_DIM), F32)
    for u in reversed(range(ratio)):
        r0 = u * tk
        start = pl.multiple_of(i * tq + r0, tk)
        row = lax.broadcasted_iota(jnp.int32, (tq - r0, tk), 0)
        col = lax.broadcasted_iota(jnp.int32, (tq - r0, tk), 1)
        w, p0, p1 = block_weights(scores(q[r0:, :], start, 1), c0[r0:, :], c1[r0:, :], col < row)
        pacc = acc[r0:, :] + attend(w, start, 1)
        if r0:
            c0 = jnp.concatenate([c0[:r0, :], p0], axis=0)
            c1 = jnp.concatenate([c1[:r0, :], p1], axis=0)
            acc = jnp.concatenate([acc[:r0, :], pacc], axis=0)
        else:
            c0, c1, acc = p0, p1, pacc

    ngroups = i * (ratio // nb)

    def group_start(g):
        return pl.multiple_of(jnp.maximum(i * ratio - (g + 1) * nb, 0) * tk, tk)

    def stage(k_next, v_prev, z_in, z_out, w_in, w_out, carry):
        c0, c1, acc = carry
        acc = acc + jnp.dot(w_in[...], v_prev, preferred_element_type=F32)
        z_out[...] = lax.dot_general(q, k_next, (((1,), (1,)), ((), ())),
                                     preferred_element_type=F32)
        for u in reversed(range(nb)):
            cols = slice(u * 2 * tk, (u + 1) * 2 * tk)
            w, c0, c1 = block_weights(z_in[:, cols], c0, c1, None)
            w_out[:, cols] = w
        return c0, c1, acc

    wb_ref[...] = jnp.zeros_like(wb_ref)
    za_ref[...] = scores(q, group_start(0), nb)

    def body(t, carry):
        k1 = stacked(k_ref, group_start(2 * t + 1), nb)
        k2 = stacked(k_ref, group_start(2 * t + 2), nb)
        v0 = stacked(v_ref, group_start(2 * t - 1), nb)
        v1 = stacked(v_ref, group_start(2 * t), nb)
        carry = stage(k1, v0, za_ref, zb_ref, wb_ref, wa_ref, carry)
        return stage(k2, v1, zb_ref, za_ref, wa_ref, wb_ref, carry)

    c0, c1, acc = lax.fori_loop(0, ngroups // 2, body, (c0, c1, acc))
    acc = acc + attend(wb_ref[...], group_start(ngroups - 1), nb)
    o_ref[0] = acc.astype(BF16)


def _attention(q, k, v, batch, seq):
    tq = min(ATT_TQ, seq)
    tk = min(ATT_TK, tq)
    q3 = q.reshape(batch, seq, SB_WIDTH)
    k3 = k.reshape(batch, seq, SB_WIDTH)
    v3 = v.reshape(batch, seq, SB_WIDTH)
    nb = ATT_NB
    assert seq % tq == 0 and (tq // tk) % (2 * nb) == 0
    r = jnp.arange(2 * tk) % tk
    ccol = jnp.arange(2 * tk)
    m2 = -jnp.where(ccol[None, :] < tk, r[:, None] >= ccol[None, :], True).astype(BF16)
    pair = 2 * SB_HEAD_DIM
    return pl.pallas_call(
        functools.partial(_attn_kernel, tq=tq, tk=tk, nb=nb),
        grid=(batch, SB_WIDTH // pair, seq // tq),
        scratch_shapes=[
            pltpu.VMEM((tq, nb * 2 * tk), F32), pltpu.VMEM((tq, nb * 2 * tk), F32),
            pltpu.VMEM((tq, nb * 2 * tk), BF16), pltpu.VMEM((tq, nb * 2 * tk), BF16),
        ],
        in_specs=[
            pl.BlockSpec((1, tq, pair), lambda b, p, i: (b, i, p)),
            pl.BlockSpec((1, seq, pair), lambda b, p, i: (b, 0, p)),
            pl.BlockSpec((1, seq, pair), lambda b, p, i: (b, 0, p)),
            pl.BlockSpec((2 * tk, 2 * tk), lambda b, p, i: (0, 0)),
        ],
        out_specs=pl.BlockSpec((1, tq, pair), lambda b, p, i: (b, i, p)),
        out_shape=jax.ShapeDtypeStruct((batch, seq, SB_WIDTH), BF16),
        compiler_params=pltpu.CompilerParams(
            dimension_semantics=("arbitrary", "arbitrary", "arbitrary"),
            vmem_limit_bytes=VMEM_LIMIT),
        name="sb_attention",
    )(q3, k3, v3, m2)


def _even_out_kernel(xb_ref, halo_ref, oa_ref, x_ref, pw_ref, ps_ref, wo_ref, o_ref, *, tm):
    i = pl.program_id(1)
    xb = xb_ref[0]
    halo = jnp.where(i > 0, halo_ref[0], 0.0)
    ext = jnp.concatenate([halo, xb], axis=0)
    pos = (i * tm + 1 + lax.broadcasted_iota(jnp.int32, (tm, 1), 0)).astype(F32)
    y = jnp.dot(oa_ref[0], wo_ref[0:SB_WIDTH, :], preferred_element_type=F32)
    pooled_out = []
    for g, w in enumerate(POOL_WINDOWS):
        lanes = slice(g * POOL_GROUP_DIM, (g + 1) * POOL_GROUP_DIM)
        s = ext[:, lanes]
        sh = 1
        while sh < w:
            s = s + pltpu.roll(s, sh, axis=0)
            sh *= 2
        window_sum = s[POOL_HALO:, :]
        pooled = window_sum / jnp.minimum(pos, float(w)) - xb[:, lanes]
        ob = jnp.dot(pooled.astype(BF16), pw_ref[g], preferred_element_type=F32)
        pooled_out.append((ob * ps_ref[:, lanes]).astype(BF16))
    y = y + jnp.dot(jnp.concatenate(pooled_out, axis=1), wo_ref[SB_WIDTH:, :],
                    preferred_element_type=F32)
    o_ref[0] = x_ref[0] + y


def _even_out(xb, o_a, x, pool_w, pool_scale, w_out):
    batch, seq, _ = x.shape
    tm = min(OUT_TM, seq)
    xb3 = xb.reshape(batch, seq, POOL_WIDTH)
    hb = tm // POOL_HALO
    const2 = lambda b, i: (0, 0)
    tile = lambda b, i: (b, i, 0)
    return pl.pallas_call(
        functools.partial(_even_out_kernel, tm=tm),
        grid=(batch, seq // tm),
        in_specs=[
            pl.BlockSpec((1, tm, POOL_WIDTH), tile),
            pl.BlockSpec((1, POOL_HALO, POOL_WIDTH),
                         lambda b, i: (b, jnp.maximum(i * hb - 1, 0), 0)),
            pl.BlockSpec((1, tm, SB_WIDTH), tile),
            pl.BlockSpec((1, tm, D_MODEL), tile),
            pl.BlockSpec((len(POOL_WINDOWS), POOL_GROUP_DIM, POOL_GROUP_DIM),
                         lambda b, i: (0, 0, 0)),
            pl.BlockSpec((1, POOL_WIDTH), const2),
            pl.BlockSpec((D_MODEL, D_MODEL), const2),
        ],
        out_specs=pl.BlockSpec((1, tm, D_MODEL), tile),
        out_shape=jax.ShapeDtypeStruct((batch, seq, D_MODEL), F32),
        compiler_params=pltpu.CompilerParams(
            dimension_semantics=("arbitrary", "arbitrary"), vmem_limit_bytes=VMEM_LIMIT),
        name="even_out",
    )(xb3, xb3, o_a, x, pool_w.astype(BF16), pool_scale.reshape(1, POOL_WIDTH),
      w_out.astype(BF16))


def _ffn_kernel(x_ref, g_ref, wag_ref, cw_ref, wd_ref, o_ref,
                h_ref, carry_ref, ua_ref, ub_ref, acta_ref, actb_ref, acc_ref, *, tm):
    i = pl.program_id(1)
    nch = FFN_NCHUNK

    @pl.when(i == 0)
    def _():
        carry_ref[...] = jnp.zeros_like(carry_ref)

    x = x_ref[0]
    h_ref[...] = _rms(x, g_ref[...]).astype(BF16)
    acc_ref[...] = x

    def up(c, u_ref):
        u = jnp.dot(h_ref[...], wag_ref[c], preferred_element_type=F32)
        u_ref[0:8, :] = carry_ref[c]
        u_ref[8:8 + tm, :] = u
        carry_ref[c] = u[tm - 8:, :]

    def activate(c, u_ref, act_ref):
        cw = cw_ref[c]
        conv = (cw[3:4, :] + cw[2:3, :] * u_ref[8:8 + tm, :]
                + cw[1:2, :] * u_ref[7:7 + tm, :]
                + cw[0:1, :] * u_ref[6:6 + tm, :])
        a = conv[:, :FFN_CHUNK]
        gate = conv[:, FFN_CHUNK:]
        act_ref[...] = (a * jax.nn.sigmoid(a) * gate).astype(BF16)

    def down(c, act_ref):
        acc_ref[...] += jnp.dot(act_ref[...], wd_ref[c], preferred_element_type=F32)

    def stage(c, u_in, u_out, act_in, act_out):
        if act_in is not None:
            down(c - 1, act_in)
        if u_out is not None:
            up(c + 1, u_out)
        activate(c, u_in, act_out)

    u_bufs = (ua_ref, ub_ref)
    act_bufs = (acta_ref, actb_ref)
    up(0, u_bufs[0])
    for c in range(nch):
        stage(c, u_bufs[c % 2],
              u_bufs[(c + 1) % 2] if c + 1 < nch else None,
              act_bufs[(c + 1) % 2] if c > 0 else None,
              act_bufs[c % 2])
    down(nch - 1, act_bufs[(nch - 1) % 2])
    o_ref[0] = acc_ref[...]


def _ffn(x, gain, w_up, conv_w, conv_b, w_down):
    batch, seq, _ = x.shape
    tm = min(FFN_TM, seq)
    ck, nch = FFN_CHUNK, FFN_NCHUNK

    def chunked(t):
        a = t[..., :FFN_HIDDEN].reshape(t.shape[:-1] + (nch, ck))
        g = t[..., FFN_HIDDEN:].reshape(t.shape[:-1] + (nch, ck))
        return jnp.moveaxis(jnp.concatenate([a, g], axis=-1), -2, 0)

    wag = chunked(w_up.astype(BF16))
    taps = jnp.concatenate([conv_w, conv_b[None, :],
                            jnp.zeros((4, 2 * FFN_HIDDEN), F32)], axis=0)
    cw = chunked(taps)
    wd = w_down.astype(BF16).reshape(nch, ck, D_MODEL)
    tile = lambda b, i: (b, i, 0)
    const3 = lambda b, i: (0, 0, 0)
    return pl.pallas_call(
        functools.partial(_ffn_kernel, tm=tm),
        grid=(batch, seq // tm),
        in_specs=[
            pl.BlockSpec((1, tm, D_MODEL), tile),
            pl.BlockSpec((1, D_MODEL), lambda b, i: (0, 0)),
            pl.BlockSpec((nch, D_MODEL, 2 * ck), const3),
            pl.BlockSpec((nch, 8, 2 * ck), const3),
            pl.BlockSpec((nch, ck, D_MODEL), const3),
        ],
        out_specs=pl.BlockSpec((1, tm, D_MODEL), tile),
        out_shape=jax.ShapeDtypeStruct((batch, seq, D_MODEL), F32),
        scratch_shapes=[
            pltpu.VMEM((tm, D_MODEL), BF16),
            pltpu.VMEM((nch, 8, 2 * ck), F32),
            pltpu.VMEM((tm + 8, 2 * ck), F32), pltpu.VMEM((tm + 8, 2 * ck), F32),
            pltpu.VMEM((tm, ck), BF16), pltpu.VMEM((tm, ck), BF16),
            pltpu.VMEM((tm, D_MODEL), F32),
        ],
        compiler_params=pltpu.CompilerParams(
            dimension_semantics=("arbitrary", "arbitrary"), vmem_limit_bytes=VMEM_LIMIT,
            ),
        name="conv_ffn",
    )(x, gain.reshape(1, D_MODEL), wag, cw, wd)


def _gla_kernel(x_ref, g_ref, win_ref, wa2_ref, ba_ref, og_ref, wout_ref, tri_ref,
                o_ref, state_ref, *, tm):
    i = pl.program_id(1)

    @pl.when(i == 0)
    def _():
        state_ref[...] = jnp.zeros_like(state_ref)

    x = x_ref[0]
    h = _rms(x, g_ref[...]).astype(BF16)
    proj = jnp.dot(h, win_ref[...], preferred_element_type=F32)
    kw, vw = GLA_KEY_WIDTH, GLA_VALUE_WIDTH
    q = proj[:, 0:kw] * (GLA_KEY_DIM ** -0.5)
    k = proj[:, kw:2 * kw]
    v = proj[:, 2 * kw:2 * kw + vw]
    r = proj[:, 2 * kw + vw:2 * kw + 2 * vw]
    a_low = proj[:, 2 * kw + 2 * vw:]
    gate = jnp.dot(a_low.astype(BF16), wa2_ref[...], preferred_element_type=F32) + ba_ref[...]
    log_alpha = (jnp.minimum(gate, 0.0) - _log1pexp_neg_abs(gate)) * (1.0 / GLA_TAU)
    hi, lo = _split_bf16(log_alpha)
    tri = tri_ref[...]
    cum = (jnp.dot(tri, hi, preferred_element_type=F32)
           + jnp.dot(tri, lo, preferred_element_type=F32))

    nchunk = tm // GLA_CHUNK
    klanes = [slice(hd * GLA_KEY_DIM, (hd + 1) * GLA_KEY_DIM) for hd in range(GLA_HEADS)]
    vlanes = [slice(hd * GLA_VALUE_DIM, (hd + 1) * GLA_VALUE_DIM) for hd in range(GLA_HEADS)]
    updates, decays = [], []
    for c in range(nchunk):
        rows = slice(c * GLA_CHUNK, (c + 1) * GLA_CHUNK)
        cum_c = cum[rows, :]
        total = cum_c[GLA_CHUNK - 1:GLA_CHUNK, :]
        k_dec = k[rows, :] * jnp.exp(total - cum_c)
        decay = jnp.broadcast_to(jnp.exp(total), (GLA_CHUNK, kw))
        kd = jnp.concatenate([k_dec, decay], axis=0)
        v_c = v[rows, :].astype(BF16)
        for hd in range(GLA_HEADS):
            kd_t = kd[:, klanes[hd]].T
            decays.append(kd_t[:, GLA_CHUNK:GLA_CHUNK + 1])
            updates.append(jnp.dot(kd_t[:, :GLA_CHUNK].astype(BF16), v_c[:, vlanes[hd]],
                                   preferred_element_type=F32))
    states = []
    for hd in range(GLA_HEADS):
        st = state_ref[hd]
        for c in range(nchunk):
            st = st * decays[c * GLA_HEADS + hd] + updates[c * GLA_HEADS + hd]
            states.append(st.astype(BF16))
        state_ref[hd] = st
    outs = []
    for c in range(nchunk):
        q_c = q[c * GLA_CHUNK:(c + 1) * GLA_CHUNK, :].astype(BF16)
        outs.append(jnp.concatenate(
            [jnp.dot(q_c[:, klanes[hd]], states[hd * nchunk + c], preferred_element_type=F32)
             for hd in range(GLA_HEADS)], axis=1))
    o = jnp.concatenate(outs, axis=0)
    normed = jnp.concatenate(
        [_rms(o[:, hd * GLA_VALUE_DIM:(hd + 1) * GLA_VALUE_DIM], og_ref[...])
         for hd in range(GLA_HEADS)], axis=1)
    gated = (normed * (r * jax.nn.sigmoid(r))).astype(BF16)
    o_ref[0] = x + jnp.dot(gated, wout_ref[...], preferred_element_type=F32)


def _gla_mixer(x, gain, w_in, w_a2, b_a, o_gain, w_out):
    batch, seq, _ = x.shape
    tm = min(GLA_TM, seq)
    in_width = 2 * GLA_KEY_WIDTH + 2 * GLA_VALUE_WIDTH + GLA_GATE_PAD
    pad = GLA_GATE_PAD - GLA_GATE_RANK
    w_in_p = jnp.pad(w_in, ((0, 0), (0, pad))).astype(BF16)
    w_a2_p = jnp.pad(w_a2, ((0, pad), (0, 0))).astype(BF16)
    t = jnp.arange(tm)
    tri = ((t[:, None] // GLA_CHUNK == t[None, :] // GLA_CHUNK)
           & (t[:, None] >= t[None, :])).astype(BF16)
    tile = lambda b, i: (b, i, 0)
    const2 = lambda b, i: (0, 0)
    return pl.pallas_call(
        functools.partial(_gla_kernel, tm=tm),
        grid=(batch, seq // tm),
        in_specs=[
            pl.BlockSpec((1, tm, D_MODEL), tile),
            pl.BlockSpec((1, D_MODEL), const2),
            pl.BlockSpec((D_MODEL, in_width), const2),
            pl.BlockSpec((GLA_GATE_PAD, GLA_KEY_WIDTH), const2),
            pl.BlockSpec((1, GLA_KEY_WIDTH), const2),
            pl.BlockSpec((1, GLA_VALUE_DIM), const2),
            pl.BlockSpec((GLA_VALUE_WIDTH, D_MODEL), const2),
            pl.BlockSpec((tm, tm), const2),
        ],
        out_specs=pl.BlockSpec((1, tm, D_MODEL), tile),
        out_shape=jax.ShapeDtypeStruct((batch, seq, D_MODEL), F32),
        scratch_shapes=[pltpu.VMEM((GLA_HEADS, GLA_KEY_DIM, GLA_VALUE_DIM), F32)],
        compiler_params=pltpu.CompilerParams(
            dimension_semantics=("arbitrary", "arbitrary"), vmem_limit_bytes=VMEM_LIMIT),
        name="gla_mixer",
    )(x, gain.reshape(1, D_MODEL), w_in_p, w_a2_p, b_a.reshape(1, GLA_KEY_WIDTH),
      o_gain.reshape(1, GLA_VALUE_DIM), w_out.astype(BF16), tri)


def kernel(x, mix_norm_even, w_in_even, sb_q_gain, sb_k_gain, pool_w, pool_scale, w_out_even,
           mix_norm_odd, w_in_odd, gla_w_a2, gla_b_a, gla_o_gain, w_out_odd,
           ffn_norm, ffn_w_up, ffn_conv_w, ffn_conv_b, ffn_w_down):
    batch, seq, _ = x.shape
    depth = ffn_norm.shape[0]
    for layer in range(depth):
        i = layer // 2
        if layer % 2 == 0:
            q, k, v, xb = _even_in(x.reshape(batch * seq, D_MODEL), mix_norm_even[i],
                                   w_in_even[i], sb_q_gain[i], sb_k_gain[i])
            o_a = _attention(q, k, v, batch, seq)
            x = _even_out(xb, o_a, x, pool_w[i], pool_scale[i], w_out_even[i])
        else:
            x = _gla_mixer(x, mix_norm_odd[i], w_in_odd[i], gla_w_a2[i], gla_b_a[i],
                           gla_o_gain[i], w_out_odd[i])
        x = _ffn(x, ffn_norm[layer], ffn_w_up[layer], ffn_conv_w[layer],
                 ffn_conv_b[layer], ffn_w_down[layer])
    return x
```

```python
import functools

import jax
import jax.numpy as jnp
from jax import lax
from jax.experimental import pallas as pl
from jax.experimental.pallas import tpu as pltpu

F32 = jnp.float32
BF16 = jnp.bfloat16

D_MODEL = 1024
NORM_EPS = 1e-6
LOG2E = 1.4426950408889634
UNDERFLOW_LOG2 = 160.0

SB_HEADS = 8
SB_HEAD_DIM = 64
SB_WIDTH = SB_HEADS * SB_HEAD_DIM
POOL_WINDOWS = (2, 4, 8, 16)
POOL_GROUP_DIM = 128
POOL_WIDTH = 512
POOL_HALO = 16

GLA_HEADS = 4
GLA_KEY_WIDTH = 512
GLA_VALUE_WIDTH = 1024
GLA_KEY_DIM = 128
GLA_VALUE_DIM = 256
GLA_GATE_RANK = 16
GLA_TAU = 16.0
GLA_CHUNK = 64
GLA_GATE_PAD = 128

FFN_HIDDEN = 2816
FFN_CHUNK = 256
FFN_NCHUNK = FFN_HIDDEN // FFN_CHUNK
assert FFN_NCHUNK % 2 == 1 and FFN_NCHUNK >= 3

IN_TM = 512
ATT_TQ = 512
ATT_TK = 128
ATT_NB = 2
OUT_TM = 512
FFN_TM = 512
GLA_TM = 256

VMEM_LIMIT = 56 * 1024 * 1024


def _rms(x, gain):
    ms = jnp.mean(x * x, axis=-1, keepdims=True)
    return x * lax.rsqrt(ms + NORM_EPS) * gain


def _log1pexp_neg_abs(z):
    return jnp.log(1.0 + jnp.exp(jnp.minimum(z, -z)))


def _split_bf16(x):
    hi = x.astype(BF16)
    lo = (x - hi.astype(F32)).astype(BF16)
    return hi, lo


def _even_in_kernel(x_ref, g_ref, w_ref, qg_ref, kg_ref, bd_ref,
                    q_ref, k_ref, v_ref, xb_ref):
    h = _rms(x_ref[...], g_ref[...]).astype(BF16)
    proj = jnp.dot(h, w_ref[...], preferred_element_type=F32)

    def head_norm(t, gain):
        ss = jnp.dot((t * t).astype(BF16), bd_ref[...], preferred_element_type=F32)
        return t * lax.rsqrt(ss * (1.0 / SB_HEAD_DIM) + NORM_EPS) * gain

    q = head_norm(proj[:, 0:SB_WIDTH], qg_ref[...])
    q_ref[...] = (q * (SB_HEAD_DIM ** -0.5 * LOG2E)).astype(BF16)
    k_ref[...] = head_norm(proj[:, SB_WIDTH:2 * SB_WIDTH], kg_ref[...]).astype(BF16)
    v_ref[...] = proj[:, 2 * SB_WIDTH:3 * SB_WIDTH].astype(BF16)
    xb_ref[...] = proj[:, 3 * SB_WIDTH:]


def _even_in(x2d, gain, w_in, q_gain, k_gain):
    n = x2d.shape[0]
    tm = min(IN_TM, n)
    head = jnp.arange(SB_WIDTH) // SB_HEAD_DIM
    blockdiag = (head[:, None] == head[None, :]).astype(BF16)
    width = w_in.shape[1]
    const = lambda i: (0, 0)
    tile = lambda i: (i, 0)
    return pl.pallas_call(
        _even_in_kernel,
        grid=(n // tm,),
        in_specs=[
            pl.BlockSpec((tm, D_MODEL), tile),
            pl.BlockSpec((1, D_MODEL), const),
            pl.BlockSpec((D_MODEL, width), const),
            pl.BlockSpec((1, SB_WIDTH), const),
            pl.BlockSpec((1, SB_WIDTH), const),
            pl.BlockSpec((SB_WIDTH, SB_WIDTH), const),
        ],
        out_specs=[
            pl.BlockSpec((tm, SB_WIDTH), tile),
            pl.BlockSpec((tm, SB_WIDTH), tile),
            pl.BlockSpec((tm, SB_WIDTH), tile),
            pl.BlockSpec((tm, POOL_WIDTH), tile),
        ],
        out_shape=[
            jax.ShapeDtypeStruct((n, SB_WIDTH), BF16),
            jax.ShapeDtypeStruct((n, SB_WIDTH), BF16),
            jax.ShapeDtypeStruct((n, SB_WIDTH), BF16),
            jax.ShapeDtypeStruct((n, POOL_WIDTH), F32),
        ],
        compiler_params=pltpu.CompilerParams(
            dimension_semantics=("arbitrary",), vmem_limit_bytes=VMEM_LIMIT),
        name="even_in",
    )(x2d, gain.reshape(1, D_MODEL), w_in.astype(BF16),
      jnp.tile(q_gain, SB_HEADS).reshape(1, SB_WIDTH),
      jnp.tile(k_gain, SB_HEADS).reshape(1, SB_WIDTH), blockdiag)


def _attn_kernel(q_ref, k_ref, v_ref, m2_ref, o_ref, za_ref, zb_ref, wa_ref, wb_ref,
                 *, tq, tk, nb):
    i = pl.program_id(2)
    q = q_ref[0]
    lane = lax.broadcasted_iota(jnp.int32, (tk, 2 * SB_HEAD_DIM), 1)
    first_head = lane < SB_HEAD_DIM
    zero = jnp.zeros((), BF16)

    def stack_heads(blk):
        return jnp.concatenate(
            [jnp.where(first_head, blk, zero), jnp.where(first_head, zero, blk)], axis=0)

    def stacked(ref, start, n):
        return jnp.concatenate(
            [stack_heads(ref[0, pl.ds(start + u * tk, tk), :]) for u in range(n)], axis=0)

    def scores(q_rows, start, n):
        return lax.dot_general(q_rows, stacked(k_ref, start, n), (((1,), (1,)), ((), ())),
                               preferred_element_type=F32)

    def attend(w, start, n):
        return jnp.dot(w, stacked(v_ref, start, n), preferred_element_type=F32)

    def block_weights(z2, c0, c1, visible):
        ws, cs = [], []
        for hh, c in ((0, c0), (1, c1)):
            z = z2[:, hh * tk:(hh + 1) * tk]
            softplus = jnp.maximum(z, 0.0) + jnp.log2(1.0 + jnp.exp2(jnp.minimum(z, -z)))
            if visible is not None:
                softplus = jnp.where(visible, softplus, 0.0)
            hi, lo = _split_bf16(softplus)
            cum = jnp.dot(jnp.concatenate([hi, lo], axis=1), m2_ref[...],
                          preferred_element_type=F32)
            w = jnp.exp2(z + cum[:, :tk] + c)
            if visible is not None:
                w = jnp.where(visible, w, 0.0)
            ws.append(w.astype(BF16))
            cs.append(c + cum[:, tk:])
        return jnp.concatenate(ws, axis=1), cs[0], cs[1]

    ratio = tq // tk
    c0 = jnp.zeros((tq, tk), F32)
    c1 = jnp.zeros((tq, tk), F32)
    acc = jnp.zeros((tq, 2 * SB_HEAD_DIM), F32)
    for u in reversed(range(ratio)):
        r0 = u * tk
        start = pl.multiple_of(i * tq + r0, tk)
        row = lax.broadcasted_iota(jnp.int32, (tq - r0, tk), 0)
        col = lax.broadcasted_iota(jnp.int32, (tq - r0, tk), 1)
        w, p0, p1 = block_weights(scores(q[r0:, :], start, 1), c0[r0:, :], c1[r0:, :], col < row)
        pacc = acc[r0:, :] + attend(w, start, 1)
        if r0:
            c0 = jnp.concatenate([c0[:r0, :], p0], axis=0)
            c1 = jnp.concatenate([c1[:r0, :], p1], axis=0)
            acc = jnp.concatenate([acc[:r0, :], pacc], axis=0)
        else:
            c0, c1, acc = p0, p1, pacc

    ngroups = i * (ratio // nb)

    def group_start(g):
        return pl.multiple_of(jnp.maximum(i * ratio - (g + 1) * nb, 0) * tk, tk)

    def stage(k_next, v_prev, z_in, z_out, w_in, w_out, carry):
        c0, c1, acc = carry
        acc = acc + jnp.dot(w_in[...], v_prev, preferred_element_type=F32)
        z_out[...] = lax.dot_general(q, k_next, (((1,), (1,)), ((), ())),
                                     preferred_element_type=F32)
        for u in reversed(range(nb)):
            cols = slice(u * 2 * tk, (u + 1) * 2 * tk)
            w, c0, c1 = block_weights(z_in[:, cols], c0, c1, None)
            w_out[:, cols] = w
        return c0, c1, acc

    wb_ref[...] = jnp.zeros_like(wb_ref)
    za_ref[...] = scores(q, group_start(0), nb)

    def any_weight_left(c0, c1):
        return jnp.maximum(jnp.max(c0), jnp.max(c1)) > -UNDERFLOW_LOG2

    def cond(state):
        t, alive = state[0], state[1]
        return jnp.logical_and(t < ngroups // 2, alive)

    def body(state):
        t, _, c0, c1, acc = state
        k1 = stacked(k_ref, group_start(2 * t + 1), nb)
        k2 = stacked(k_ref, group_start(2 * t + 2), nb)
        v0 = stacked(v_ref, group_start(2 * t - 1), nb)
        v1 = stacked(v_ref, group_start(2 * t), nb)
        carry = stage(k1, v0, za_ref, zb_ref, wb_ref, wa_ref, (c0, c1, acc))
        c0, c1, acc = stage(k2, v1, zb_ref, za_ref, wa_ref, wb_ref, carry)
        return t + 1, any_weight_left(c0, c1), c0, c1, acc

    steps, _, c0, c1, acc = lax.while_loop(
        cond, body, (jnp.int32(0), any_weight_left(c0, c1), c0, c1, acc))
    acc = acc + attend(wb_ref[...], group_start(2 * steps - 1), nb)
    o_ref[0] = acc.astype(BF16)


def _attention(q, k, v, batch, seq):
    tq = min(ATT_TQ, seq)
    tk = min(ATT_TK, tq)
    q3 = q.reshape(batch, seq, SB_WIDTH)
    k3 = k.reshape(batch, seq, SB_WIDTH)
    v3 = v.reshape(batch, seq, SB_WIDTH)
    nb = ATT_NB
    assert seq % tq == 0 and (tq // tk) % (2 * nb) == 0
    r = jnp.arange(2 * tk) % tk
    ccol = jnp.arange(2 * tk)
    m2 = -jnp.where(ccol[None, :] < tk, r[:, None] >= ccol[None, :], True).astype(BF16)
    pair = 2 * SB_HEAD_DIM
    return pl.pallas_call(
        functools.partial(_attn_kernel, tq=tq, tk=tk, nb=nb),
        grid=(batch, SB_WIDTH // pair, seq // tq),
        scratch_shapes=[
            pltpu.VMEM((tq, nb * 2 * tk), F32), pltpu.VMEM((tq, nb * 2 * tk), F32),
            pltpu.VMEM((tq, nb * 2 * tk), BF16), pltpu.VMEM((tq, nb * 2 * tk), BF16),
        ],
        in_specs=[
            pl.BlockSpec((1, tq, pair), lambda b, p, i: (b, i, p)),
            pl.BlockSpec((1, seq, pair), lambda b, p, i: (b, 0, p)),
            pl.BlockSpec((1, seq, pair), lambda b, p, i: (b, 0, p)),
            pl.BlockSpec((2 * tk, 2 * tk), lambda b, p, i: (0, 0)),
        ],
        out_specs=pl.BlockSpec((1, tq, pair), lambda b, p, i: (b, i, p)),
        out_shape=jax.ShapeDtypeStruct((batch, seq, SB_WIDTH), BF16),
        compiler_params=pltpu.CompilerParams(
            dimension_semantics=("arbitrary", "arbitrary", "arbitrary"),
            vmem_limit_bytes=VMEM_LIMIT),
        name="sb_attention",
    )(q3, k3, v3, m2)


def _even_out_kernel(xb_ref, halo_ref, oa_ref, x_ref, pw_ref, ps_ref, wo_ref, o_ref, *, tm):
    i = pl.program_id(1)
    xb = xb_ref[0]
    halo = jnp.where(i > 0, halo_ref[0], 0.0)
    ext = jnp.concatenate([halo, xb], axis=0)
    pos = (i * tm + 1 + lax.broadcasted_iota(jnp.int32, (tm, 1), 0)).astype(F32)
    y = jnp.dot(oa_ref[0], wo_ref[0:SB_WIDTH, :], preferred_element_type=F32)
    pooled_out = []
    for g, w in enumerate(POOL_WINDOWS):
        lanes = slice(g * POOL_GROUP_DIM, (g + 1) * POOL_GROUP_DIM)
        s = ext[:, lanes]
        sh = 1
        while sh < w:
            s = s + pltpu.roll(s, sh, axis=0)
            sh *= 2
        window_sum = s[POOL_HALO:, :]
        pooled = window_sum / jnp.minimum(pos, float(w)) - xb[:, lanes]
        ob = jnp.dot(pooled.astype(BF16), pw_ref[g], preferred_element_type=F32)
        pooled_out.append((ob * ps_ref[:, lanes]).astype(BF16))
    y = y + jnp.dot(jnp.concatenate(pooled_out, axis=1), wo_ref[SB_WIDTH:, :],
                    preferred_element_type=F32)
    o_ref[0] = x_ref[0] + y


def _even_out(xb, o_a, x, pool_w, pool_scale, w_out):
    batch, seq, _ = x.shape
    tm = min(OUT_TM, seq)
    xb3 = xb.reshape(batch, seq, POOL_WIDTH)
    hb = tm // POOL_HALO
    const2 = lambda b, i: (0, 0)
    tile = lambda b, i: (b, i, 0)
    return pl.pallas_call(
        functools.partial(_even_out_kernel, tm=tm),
        grid=(batch, seq // tm),
        in_specs=[
            pl.BlockSpec((1, tm, POOL_WIDTH), tile),
            pl.BlockSpec((1, POOL_HALO, POOL_WIDTH),
                         lambda b, i: (b, jnp.maximum(i * hb - 1, 0), 0)),
            pl.BlockSpec((1, tm, SB_WIDTH), tile),
            pl.BlockSpec((1, tm, D_MODEL), tile),
            pl.BlockSpec((len(POOL_WINDOWS), POOL_GROUP_DIM, POOL_GROUP_DIM),
                         lambda b, i: (0, 0, 0)),
            pl.BlockSpec((1, POOL_WIDTH), const2),
            pl.BlockSpec((D_MODEL, D_MODEL), const2),
        ],
        out_specs=pl.BlockSpec((1, tm, D_MODEL), tile),
        out_shape=jax.ShapeDtypeStruct((batch, seq, D_MODEL), F32),
        compiler_params=pltpu.CompilerParams(
            dimension_semantics=("arbitrary", "arbitrary"), vmem_limit_bytes=VMEM_LIMIT),
        name="even_out",
    )(xb3, xb3, o_a, x, pool_w.astype(BF16), pool_scale.reshape(1, POOL_WIDTH),
      w_out.astype(BF16))


def _ffn_kernel(x_ref, g_ref, wag_ref, cw_ref, wd_ref, o_ref,
                h_ref, carry_ref, ua_ref, ub_ref, acta_ref, actb_ref, acc_ref, *, tm):
    i = pl.program_id(1)
    nch = FFN_NCHUNK

    @pl.when(i == 0)
    def _():
        carry_ref[...] = jnp.zeros_like(carry_ref)

    x = x_ref[0]
    h_ref[...] = _rms(x, g_ref[...]).astype(BF16)
    acc_ref[...] = x

    def up(c, u_ref):
        u = jnp.dot(h_ref[...], wag_ref[c], preferred_element_type=F32)
        u_ref[0:8, :] = carry_ref[c]
        u_ref[8:8 + tm, :] = u
        carry_ref[c] = u[tm - 8:, :]

    def activate(c, u_ref, act_ref):
        cw = cw_ref[c]
        conv = (cw[3:4, :] + cw[2:3, :] * u_ref[8:8 + tm, :]
                + cw[1:2, :] * u_ref[7:7 + tm, :]
                + cw[0:1, :] * u_ref[6:6 + tm, :])
        a = conv[:, :FFN_CHUNK]
        gate = conv[:, FFN_CHUNK:]
        act_ref[...] = (a * jax.nn.sigmoid(a) * gate).astype(BF16)

    def down(c, act_ref):
        acc_ref[...] += jnp.dot(act_ref[...], wd_ref[c], preferred_element_type=F32)

    def stage(c, u_in, u_out, act_in, act_out):
        if act_in is not None:
            down(c - 1, act_in)
        if u_out is not None:
            up(c + 1, u_out)
        activate(c, u_in, act_out)

    u_bufs = (ua_ref, ub_ref)
    act_bufs = (acta_ref, actb_ref)
    up(0, u_bufs[0])
    for c in range(nch):
        stage(c, u_bufs[c % 2],
              u_bufs[(c + 1) % 2] if c + 1 < nch else None,
              act_bufs[(c + 1) % 2] if c > 0 else None,
              act_bufs[c % 2])
    down(nch - 1, act_bufs[(nch - 1) % 2])
    o_ref[0] = acc_ref[...]


def _ffn(x, gain, w_up, conv_w, conv_b, w_down):
    batch, seq, _ = x.shape
    tm = min(FFN_TM, seq)
    ck, nch = FFN_CHUNK, FFN_NCHUNK

    def chunked(t):
        a = t[..., :FFN_HIDDEN].reshape(t.shape[:-1] + (nch, ck))
        g = t[..., FFN_HIDDEN:].reshape(t.shape[:-1] + (nch, ck))
        return jnp.moveaxis(jnp.concatenate([a, g], axis=-1), -2, 0)

    wag = chunked(w_up.astype(BF16))
    taps = jnp.concatenate([conv_w, conv_b[None, :],
                            jnp.zeros((4, 2 * FFN_HIDDEN), F32)], axis=0)
    cw = chunked(taps)
    wd = w_down.astype(BF16).reshape(nch, ck, D_MODEL)
    tile = lambda b, i: (b, i, 0)
    const3 = lambda b, i: (0, 0, 0)
    return pl.pallas_call(
        functools.partial(_ffn_kernel, tm=tm),
        grid=(batch, seq // tm),
        in_specs=[
            pl.BlockSpec((1, tm, D_MODEL), tile),
            pl.BlockSpec((1, D_MODEL), lambda b, i: (0, 0)),
            pl.BlockSpec((nch, D_MODEL, 2 * ck), const3),
            pl.BlockSpec((nch, 8, 2 * ck), const3),
            pl.BlockSpec((nch, ck, D_MODEL), const3),
        ],
        out_specs=pl.BlockSpec((1, tm, D_MODEL), tile),
        out_shape=jax.ShapeDtypeStruct((batch, seq, D_MODEL), F32),
        scratch_shapes=[
            pltpu.VMEM((tm, D_MODEL), BF16),
            pltpu.VMEM((nch, 8, 2 * ck), F32),
            pltpu.VMEM((tm + 8, 2 * ck), F32), pltpu.VMEM((tm + 8, 2 * ck), F32),
            pltpu.VMEM((tm, ck), BF16), pltpu.VMEM((tm, ck), BF16),
            pltpu.VMEM((tm, D_MODEL), F32),
        ],
        compiler_params=pltpu.CompilerParams(
            dimension_semantics=("arbitrary", "arbitrary"), vmem_limit_bytes=VMEM_LIMIT,
            ),
        name="conv_ffn",
    )(x, gain.reshape(1, D_MODEL), wag, cw, wd)


def _gla_kernel(x_ref, g_ref, win_ref, wa2_ref, ba_ref, og_ref, wout_ref, tri_ref,
                o_ref, state_ref, *, tm):
    i = pl.program_id(1)

    @pl.when(i == 0)
    def _():
        state_ref[...] = jnp.zeros_like(state_ref)

    x = x_ref[0]
    h = _rms(x, g_ref[...]).astype(BF16)
    proj = jnp.dot(h, win_ref[...], preferred_element_type=F32)
    kw, vw = GLA_KEY_WIDTH, GLA_VALUE_WIDTH
    q = proj[:, 0:kw] * (GLA_KEY_DIM ** -0.5)
    k = proj[:, kw:2 * kw]
    v = proj[:, 2 * kw:2 * kw + vw]
    r = proj[:, 2 * kw + vw:2 * kw + 2 * vw]
    a_low = proj[:, 2 * kw + 2 * vw:]
    gate = jnp.dot(a_low.astype(BF16), wa2_ref[...], preferred_element_type=F32) + ba_ref[...]
    log_alpha = (jnp.minimum(gate, 0.0) - _log1pexp_neg_abs(gate)) * (1.0 / GLA_TAU)
    hi, lo = _split_bf16(log_alpha)
    tri = tri_ref[...]
    cum = (jnp.dot(tri, hi, preferred_element_type=F32)
           + jnp.dot(tri, lo, preferred_element_type=F32))

    nchunk = tm // GLA_CHUNK
    klanes = [slice(hd * GLA_KEY_DIM, (hd + 1) * GLA_KEY_DIM) for hd in range(GLA_HEADS)]
    vlanes = [slice(hd * GLA_VALUE_DIM, (hd + 1) * GLA_VALUE_DIM) for hd in range(GLA_HEADS)]
    updates, decays = [], []
    for c in range(nchunk):
        rows = slice(c * GLA_CHUNK, (c + 1) * GLA_CHUNK)
        cum_c = cum[rows, :]
        total = cum_c[GLA_CHUNK - 1:GLA_CHUNK, :]
        k_dec = k[rows, :] * jnp.exp(total - cum_c)
        decay = jnp.broadcast_to(jnp.exp(total), (GLA_CHUNK, kw))
        kd = jnp.concatenate([k_dec, decay], axis=0)
        v_c = v[rows, :].astype(BF16)
        for hd in range(GLA_HEADS):
            kd_t = kd[:, klanes[hd]].T
            decays.append(kd_t[:, GLA_CHUNK:GLA_CHUNK + 1])
            updates.append(jnp.dot(kd_t[:, :GLA_CHUNK].astype(BF16), v_c[:, vlanes[hd]],
                                   preferred_element_type=F32))
    states = []
    for hd in range(GLA_HEADS):
        st = state_ref[hd]
        for c in range(nchunk):
            st = st * decays[c * GLA_HEADS + hd] + updates[c * GLA_HEADS + hd]
            states.append(st.astype(BF16))
        state_ref[hd] = st
    outs = []
    for c in range(nchunk):
        q_c = q[c * GLA_CHUNK:(c + 1) * GLA_CHUNK, :].astype(BF16)
        outs.append(jnp.concatenate(
            [jnp.dot(q_c[:, klanes[hd]], states[hd * nchunk + c], preferred_element_type=F32)
             for hd in range(GLA_HEADS)], axis=1))
    o = jnp.concatenate(outs, axis=0)
    normed = jnp.concatenate(
        [_rms(o[:, hd * GLA_VALUE_DIM:(hd + 1) * GLA_VALUE_DIM], og_ref[...])
         for hd in range(GLA_HEADS)], axis=1)
    gated = (normed * (r * jax.nn.sigmoid(r))).astype(BF16)
    o_ref[0] = x + jnp.dot(gated, wout_ref[...], preferred_element_type=F32)


def _gla_mixer(x, gain, w_in, w_a2, b_a, o_gain, w_out):
    batch, seq, _ = x.shape
    tm = min(GLA_TM, seq)
    in_width = 2 * GLA_KEY_WIDTH + 2 * GLA_VALUE_WIDTH + GLA_GATE_PAD
    pad = GLA_GATE_PAD - GLA_GATE_RANK
    w_in_p = jnp.pad(w_in, ((0, 0), (0, pad))).astype(BF16)
    w_a2_p = jnp.pad(w_a2, ((0, pad), (0, 0))).astype(BF16)
    t = jnp.arange(tm)
    tri = ((t[:, None] // GLA_CHUNK == t[None, :] // GLA_CHUNK)
           & (t[:, None] >= t[None, :])).astype(BF16)
    tile = lambda b, i: (b, i, 0)
    const2 = lambda b, i: (0, 0)
    return pl.pallas_call(
        functools.partial(_gla_kernel, tm=tm),
        grid=(batch, seq // tm),
        in_specs=[
            pl.BlockSpec((1, tm, D_MODEL), tile),
            pl.BlockSpec((1, D_MODEL), const2),
            pl.BlockSpec((D_MODEL, in_width), const2),
            pl.BlockSpec((GLA_GATE_PAD, GLA_KEY_WIDTH), const2),
            pl.BlockSpec((1, GLA_KEY_WIDTH), const2),
            pl.BlockSpec((1, GLA_VALUE_DIM), const2),
            pl.BlockSpec((GLA_VALUE_WIDTH, D_MODEL), const2),
            pl.BlockSpec((tm, tm), const2),
        ],
        out_specs=pl.BlockSpec((1, tm, D_MODEL), tile),
        out_shape=jax.ShapeDtypeStruct((batch, seq, D_MODEL), F32),
        scratch_shapes=[pltpu.VMEM((GLA_HEADS, GLA_KEY_DIM, GLA_VALUE_DIM), F32)],
        compiler_params=pltpu.CompilerParams(
            dimension_semantics=("arbitrary", "arbitrary"), vmem_limit_bytes=VMEM_LIMIT),
        name="gla_mixer",
    )(x, gain.reshape(1, D_MODEL), w_in_p, w_a2_p, b_a.reshape(1, GLA_KEY_WIDTH),
      o_gain.reshape(1, GLA_VALUE_DIM), w_out.astype(BF16), tri)


def kernel(x, mix_norm_even, w_in_even, sb_q_gain, sb_k_gain, pool_w, pool_scale, w_out_even,
           mix_norm_odd, w_in_odd, gla_w_a2, gla_b_a, gla_o_gain, w_out_odd,
           ffn_norm, ffn_w_up, ffn_conv_w, ffn_conv_b, ffn_w_down):
    batch, seq, _ = x.shape
    depth = ffn_norm.shape[0]
    for layer in range(depth):
        i = layer // 2
        if layer % 2 == 0:
            q, k, v, xb = _even_in(x.reshape(batch * seq, D_MODEL), mix_norm_even[i],
                                   w_in_even[i], sb_q_gain[i], sb_k_gain[i])
            o_a = _attention(q, k, v, batch, seq)
            x = _even_out(xb, o_a, x, pool_w[i], pool_scale[i], w_out_even[i])
        else:
            x = _gla_mixer(x, mix_norm_odd[i], w_in_odd[i], gla_w_a2[i], gla_b_a[i],
                           gla_o_gain[i], w_out_odd[i])
        x = _ffn(x, ffn_norm[layer], ffn_w_up[layer], ffn_conv_w[layer],
                 ffn_conv_b[layer], ffn_w_down[layer])
    return x
```

```python
import functools

import jax
import jax.numpy as jnp
from jax import lax
from jax.experimental import pallas as pl
from jax.experimental.pallas import tpu as pltpu

F32 = jnp.float32
BF16 = jnp.bfloat16

D_MODEL = 1024
NORM_EPS = 1e-6
LOG2E = 1.4426950408889634
UNDERFLOW_LOG2 = 160.0

SB_HEADS = 8
SB_HEAD_DIM = 64
SB_WIDTH = SB_HEADS * SB_HEAD_DIM
POOL_WINDOWS = (2, 4, 8, 16)
POOL_GROUP_DIM = 128
POOL_WIDTH = 512
POOL_HALO = 16

GLA_HEADS = 4
GLA_KEY_WIDTH = 512
GLA_VALUE_WIDTH = 1024
GLA_KEY_DIM = 128
GLA_VALUE_DIM = 256
GLA_GATE_RANK = 16
GLA_TAU = 16.0
GLA_CHUNK = 64
GLA_GATE_PAD = 128

FFN_HIDDEN = 2816
FFN_CHUNK = 256
FFN_NCHUNK = FFN_HIDDEN // FFN_CHUNK
assert FFN_NCHUNK % 2 == 1 and FFN_NCHUNK >= 3

IN_TM = 512
ATT_TQ = 512
ATT_TK = 128
ATT_NB = 2
OUT_TM = 512
FFN_TM = 512
GLA_TM = 256

VMEM_LIMIT = 56 * 1024 * 1024


def _rms(x, gain):
    ms = jnp.mean(x * x, axis=-1, keepdims=True)
    return x * lax.rsqrt(ms + NORM_EPS) * gain


def _log1pexp_neg_abs(z):
    return jnp.log(1.0 + jnp.exp(jnp.minimum(z, -z)))


def _split_bf16(x):
    hi = x.astype(BF16)
    lo = (x - hi.astype(F32)).astype(BF16)
    return hi, lo


def _even_in_kernel(x_ref, g_ref, w_ref, qg_ref, kg_ref, bd_ref,
                    q_ref, k_ref, v_ref, xb_ref):
    h = _rms(x_ref[...], g_ref[...]).astype(BF16)
    proj = jnp.dot(h, w_ref[...], preferred_element_type=F32)

    def head_norm(t, gain):
        ss = jnp.dot((t * t).astype(BF16), bd_ref[...], preferred_element_type=F32)
        return t * lax.rsqrt(ss * (1.0 / SB_HEAD_DIM) + NORM_EPS) * gain

    q = head_norm(proj[:, 0:SB_WIDTH], qg_ref[...])
    q_ref[...] = (q * (SB_HEAD_DIM ** -0.5 * LOG2E)).astype(BF16)
    k_ref[...] = head_norm(proj[:, SB_WIDTH:2 * SB_WIDTH], kg_ref[...]).astype(BF16)
    v_ref[...] = proj[:, 2 * SB_WIDTH:3 * SB_WIDTH].astype(BF16)
    xb_ref[...] = proj[:, 3 * SB_WIDTH:]


def _even_in(x2d, gain, w_in, q_gain, k_gain):
    n = x2d.shape[0]
    tm = min(IN_TM, n)
    head = jnp.arange(SB_WIDTH) // SB_HEAD_DIM
    blockdiag = (head[:, None] == head[None, :]).astype(BF16)
    width = w_in.shape[1]
    const = lambda i: (0, 0)
    tile = lambda i: (i, 0)
    return pl.pallas_call(
        _even_in_kernel,
        grid=(n // tm,),
        in_specs=[
            pl.BlockSpec((tm, D_MODEL), tile),
            pl.BlockSpec((1, D_MODEL), const),
            pl.BlockSpec((D_MODEL, width), const),
            pl.BlockSpec((1, SB_WIDTH), const),
            pl.BlockSpec((1, SB_WIDTH), const),
            pl.BlockSpec((SB_WIDTH, SB_WIDTH), const),
        ],
        out_specs=[
            pl.BlockSpec((tm, SB_WIDTH), tile),
            pl.BlockSpec((tm, SB_WIDTH), tile),
            pl.BlockSpec((tm, SB_WIDTH), tile),
            pl.BlockSpec((tm, POOL_WIDTH), tile),
        ],
        out_shape=[
            jax.ShapeDtypeStruct((n, SB_WIDTH), BF16),
            jax.ShapeDtypeStruct((n, SB_WIDTH), BF16),
            jax.ShapeDtypeStruct((n, SB_WIDTH), BF16),
            jax.ShapeDtypeStruct((n, POOL_WIDTH), F32),
        ],
        compiler_params=pltpu.CompilerParams(
            dimension_semantics=("arbitrary",), vmem_limit_bytes=VMEM_LIMIT),
        name="even_in",
    )(x2d, gain.reshape(1, D_MODEL), w_in.astype(BF16),
      jnp.tile(q_gain, SB_HEADS).reshape(1, SB_WIDTH),
      jnp.tile(k_gain, SB_HEADS).reshape(1, SB_WIDTH), blockdiag)


def _attn_kernel(q_ref, k_ref, v_ref, m2_ref, o_ref, za_ref, zb_ref, wa_ref, wb_ref,
                 c0_ref, c1_ref, acc_ref, *, tq, tk, nb):
    i = pl.program_id(2)
    q = q_ref[0]
    lane = lax.broadcasted_iota(jnp.int32, (tk, 2 * SB_HEAD_DIM), 1)
    first_head = lane < SB_HEAD_DIM
    zero = jnp.zeros((), BF16)

    def stack_heads(blk):
        return jnp.concatenate(
            [jnp.where(first_head, blk, zero), jnp.where(first_head, zero, blk)], axis=0)

    def stacked(ref, start, n):
        return jnp.concatenate(
            [stack_heads(ref[0, pl.ds(start + u * tk, tk), :]) for u in range(n)], axis=0)

    def scores(q_rows, start, n):
        return lax.dot_general(q_rows, stacked(k_ref, start, n), (((1,), (1,)), ((), ())),
                               preferred_element_type=F32)

    def attend(w, start, n):
        return jnp.dot(w, stacked(v_ref, start, n), preferred_element_type=F32)

    def block_weights(z2, c0, c1, visible):
        ws, cs = [], []
        for hh, c in ((0, c0), (1, c1)):
            z = z2[:, hh * tk:(hh + 1) * tk]
            softplus = jnp.maximum(z, 0.0) + jnp.log2(1.0 + jnp.exp2(jnp.minimum(z, -z)))
            if visible is not None:
                softplus = jnp.where(visible, softplus, 0.0)
            hi, lo = _split_bf16(softplus)
            cum = jnp.dot(jnp.concatenate([hi, lo], axis=1), m2_ref[...],
                          preferred_element_type=F32)
            w = jnp.exp2(z + cum[:, :tk] + c)
            if visible is not None:
                w = jnp.where(visible, w, 0.0)
            ws.append(w.astype(BF16))
            cs.append(c + cum[:, tk:])
        return jnp.concatenate(ws, axis=1), cs[0], cs[1]

    ratio = tq // tk
    c0 = jnp.zeros((tq, tk), F32)
    c1 = jnp.zeros((tq, tk), F32)
    acc = jnp.zeros((tq, 2 * SB_HEAD_DIM), F32)
    for u in reversed(range(ratio)):
        r0 = u * tk
        start = pl.multiple_of(i * tq + r0, tk)
        row = lax.broadcasted_iota(jnp.int32, (tq - r0, tk), 0)
        col = lax.broadcasted_iota(jnp.int32, (tq - r0, tk), 1)
        w, p0, p1 = block_weights(scores(q[r0:, :], start, 1), c0[r0:, :], c1[r0:, :], col < row)
        pacc = acc[r0:, :] + attend(w, start, 1)
        if r0:
            c0 = jnp.concatenate([c0[:r0, :], p0], axis=0)
            c1 = jnp.concatenate([c1[:r0, :], p1], axis=0)
            acc = jnp.concatenate([acc[:r0, :], pacc], axis=0)
        else:
            c0, c1, acc = p0, p1, pacc

    ngroups = i * (ratio // nb)

    def group_start(g):
        return pl.multiple_of(jnp.maximum(i * ratio - (g + 1) * nb, 0) * tk, tk)

    def stage(g, z_in, z_out, w_in, w_out):
        acc_ref[...] += attend(w_in[...], group_start(g - 1), nb)
        z_out[...] = scores(q, group_start(g + 1), nb)
        for u in reversed(range(nb)):
            cols = slice(u * 2 * tk, (u + 1) * 2 * tk)
            w, c0, c1 = block_weights(z_in[:, cols], c0_ref[...], c1_ref[...], None)
            c0_ref[...] = c0
            c1_ref[...] = c1
            w_out[:, cols] = w

    def any_weight_left():
        return jnp.maximum(jnp.max(c0_ref[...]), jnp.max(c1_ref[...])) > -UNDERFLOW_LOG2

    c0_ref[...] = c0
    c1_ref[...] = c1
    acc_ref[...] = acc
    wb_ref[...] = jnp.zeros_like(wb_ref)
    za_ref[...] = scores(q, group_start(0), nb)

    def cond(state):
        g, alive = state
        return jnp.logical_and(g < ngroups, alive)

    def body(state):
        g, _ = state
        stage(g, za_ref, zb_ref, wb_ref, wa_ref)
        more = jnp.logical_and(g + 1 < ngroups, any_weight_left())

        @pl.when(more)
        def _():
            stage(g + 1, zb_ref, za_ref, wa_ref, wb_ref)

        return g + 1 + more.astype(jnp.int32), any_weight_left()

    done, _ = lax.while_loop(cond, body, (jnp.int32(0), any_weight_left()))

    @pl.when(done % 2 == 1)
    def _():
        acc_ref[...] += attend(wa_ref[...], group_start(done - 1), nb)

    @pl.when(done % 2 == 0)
    def _():
        acc_ref[...] += attend(wb_ref[...], group_start(done - 1), nb)

    o_ref[0] = acc_ref[...].astype(BF16)


def _attention(q, k, v, batch, seq):
    tq = min(ATT_TQ, seq)
    tk = min(ATT_TK, tq)
    q3 = q.reshape(batch, seq, SB_WIDTH)
    k3 = k.reshape(batch, seq, SB_WIDTH)
    v3 = v.reshape(batch, seq, SB_WIDTH)
    nb = ATT_NB
    assert seq % tq == 0 and (tq // tk) % (2 * nb) == 0
    r = jnp.arange(2 * tk) % tk
    ccol = jnp.arange(2 * tk)
    m2 = -jnp.where(ccol[None, :] < tk, r[:, None] >= ccol[None, :], True).astype(BF16)
    pair = 2 * SB_HEAD_DIM
    return pl.pallas_call(
        functools.partial(_attn_kernel, tq=tq, tk=tk, nb=nb),
        grid=(batch, SB_WIDTH // pair, seq // tq),
        scratch_shapes=[
            pltpu.VMEM((tq, nb * 2 * tk), F32), pltpu.VMEM((tq, nb * 2 * tk), F32),
            pltpu.VMEM((tq, nb * 2 * tk), BF16), pltpu.VMEM((tq, nb * 2 * tk), BF16),
            pltpu.VMEM((tq, tk), F32), pltpu.VMEM((tq, tk), F32),
            pltpu.VMEM((tq, 2 * SB_HEAD_DIM), F32),
        ],
        in_specs=[
            pl.BlockSpec((1, tq, pair), lambda b, p, i: (b, i, p)),
            pl.BlockSpec((1, seq, pair), lambda b, p, i: (b, 0, p)),
            pl.BlockSpec((1, seq, pair), lambda b, p, i: (b, 0, p)),
            pl.BlockSpec((2 * tk, 2 * tk), lambda b, p, i: (0, 0)),
        ],
        out_specs=pl.BlockSpec((1, tq, pair), lambda b, p, i: (b, i, p)),
        out_shape=jax.ShapeDtypeStruct((batch, seq, SB_WIDTH), BF16),
        compiler_params=pltpu.CompilerParams(
            dimension_semantics=("arbitrary", "arbitrary", "arbitrary"),
            vmem_limit_bytes=VMEM_LIMIT),
        name="sb_attention",
    )(q3, k3, v3, m2)


def _even_out_kernel(xb_ref, halo_ref, oa_ref, x_ref, pw_ref, ps_ref, wo_ref, o_ref, *, tm):
    i = pl.program_id(1)
    xb = xb_ref[0]
    halo = jnp.where(i > 0, halo_ref[0], 0.0)
    ext = jnp.concatenate([halo, xb], axis=0)
    pos = (i * tm + 1 + lax.broadcasted_iota(jnp.int32, (tm, 1), 0)).astype(F32)
    y = jnp.dot(oa_ref[0], wo_ref[0:SB_WIDTH, :], preferred_element_type=F32)
    pooled_out = []
    for g, w in enumerate(POOL_WINDOWS):
        lanes = slice(g * POOL_GROUP_DIM, (g + 1) * POOL_GROUP_DIM)
        s = ext[:, lanes]
        sh = 1
        while sh < w:
            s = s + pltpu.roll(s, sh, axis=0)
            sh *= 2
        window_sum = s[POOL_HALO:, :]
        pooled = window_sum / jnp.minimum(pos, float(w)) - xb[:, lanes]
        ob = jnp.dot(pooled.astype(BF16), pw_ref[g], preferred_element_type=F32)
        pooled_out.append((ob * ps_ref[:, lanes]).astype(BF16))
    y = y + jnp.dot(jnp.concatenate(pooled_out, axis=1), wo_ref[SB_WIDTH:, :],
                    preferred_element_type=F32)
    o_ref[0] = x_ref[0] + y


def _even_out(xb, o_a, x, pool_w, pool_scale, w_out):
    batch, seq, _ = x.shape
    tm = min(OUT_TM, seq)
    xb3 = xb.reshape(batch, seq, POOL_WIDTH)
    hb = tm // POOL_HALO
    const2 = lambda b, i: (0, 0)
    tile = lambda b, i: (b, i, 0)
    return pl.pallas_call(
        functools.partial(_even_out_kernel, tm=tm),
        grid=(batch, seq // tm),
        in_specs=[
            pl.BlockSpec((1, tm, POOL_WIDTH), tile),
            pl.BlockSpec((1, POOL_HALO, POOL_WIDTH),
                         lambda b, i: (b, jnp.maximum(i * hb - 1, 0), 0)),
            pl.BlockSpec((1, tm, SB_WIDTH), tile),
            pl.BlockSpec((1, tm, D_MODEL), tile),
            pl.BlockSpec((len(POOL_WINDOWS), POOL_GROUP_DIM, POOL_GROUP_DIM),
                         lambda b, i: (0, 0, 0)),
            pl.BlockSpec((1, POOL_WIDTH), const2),
            pl.BlockSpec((D_MODEL, D_MODEL), const2),
        ],
        out_specs=pl.BlockSpec((1, tm, D_MODEL), tile),
        out_shape=jax.ShapeDtypeStruct((batch, seq, D_MODEL), F32),
        compiler_params=pltpu.CompilerParams(
            dimension_semantics=("arbitrary", "arbitrary"), vmem_limit_bytes=VMEM_LIMIT),
        name="even_out",
    )(xb3, xb3, o_a, x, pool_w.astype(BF16), pool_scale.reshape(1, POOL_WIDTH),
      w_out.astype(BF16))


def _ffn_kernel(x_ref, g_ref, wup_ref, cw_ref, wd_ref, o_ref,
                h_ref, carry_ref, ua_ref, ub_ref, acta_ref, actb_ref, acc_ref, *, tm):
    i = pl.program_id(1)
    nch = FFN_NCHUNK

    @pl.when(i == 0)
    def _():
        carry_ref[...] = jnp.zeros_like(carry_ref)

    x = x_ref[0]
    h_ref[...] = _rms(x, g_ref[...]).astype(BF16)
    acc_ref[...] = x

    def columns(c):
        return (slice(c * FFN_CHUNK, (c + 1) * FFN_CHUNK),
                slice(FFN_HIDDEN + c * FFN_CHUNK, FFN_HIDDEN + (c + 1) * FFN_CHUNK))

    def up(c, u_ref):
        h = h_ref[...]
        u = jnp.concatenate([jnp.dot(h, wup_ref[:, cols], preferred_element_type=F32)
                             for cols in columns(c)], axis=1)
        u_ref[0:8, :] = carry_ref[c]
        u_ref[8:8 + tm, :] = u
        carry_ref[c] = u[tm - 8:, :]

    def activate(c, u_ref, act_ref):
        cw = jnp.concatenate([cw_ref[:, cols] for cols in columns(c)], axis=1)
        conv = (cw[3:4, :] + cw[2:3, :] * u_ref[8:8 + tm, :]
                + cw[1:2, :] * u_ref[7:7 + tm, :]
                + cw[0:1, :] * u_ref[6:6 + tm, :])
        a = conv[:, :FFN_CHUNK]
        gate = conv[:, FFN_CHUNK:]
        act_ref[...] = (a * jax.nn.sigmoid(a) * gate).astype(BF16)

    def down(c, act_ref):
        acc_ref[...] += jnp.dot(act_ref[...], wd_ref[c], preferred_element_type=F32)

    def stage(c, u_in, u_out, act_in, act_out):
        if act_in is not None:
            down(c - 1, act_in)
        if u_out is not None:
            up(c + 1, u_out)
        activate(c, u_in, act_out)

    u_bufs = (ua_ref, ub_ref)
    act_bufs = (acta_ref, actb_ref)
    up(0, u_bufs[0])
    for c in range(nch):
        stage(c, u_bufs[c % 2],
              u_bufs[(c + 1) % 2] if c + 1 < nch else None,
              act_bufs[(c + 1) % 2] if c > 0 else None,
              act_bufs[c % 2])
    down(nch - 1, act_bufs[(nch - 1) % 2])
    o_ref[0] = acc_ref[...]


def _ffn(x, gain, w_up, conv_w, conv_b, w_down):
    batch, seq, _ = x.shape
    tm = min(FFN_TM, seq)
    ck, nch = FFN_CHUNK, FFN_NCHUNK

    taps = jnp.concatenate([conv_w, conv_b[None, :],
                            jnp.zeros((4, 2 * FFN_HIDDEN), F32)], axis=0)
    wd = w_down.astype(BF16).reshape(nch, ck, D_MODEL)
    tile = lambda b, i: (b, i, 0)
    const2 = lambda b, i: (0, 0)
    const3 = lambda b, i: (0, 0, 0)
    return pl.pallas_call(
        functools.partial(_ffn_kernel, tm=tm),
        grid=(batch, seq // tm),
        in_specs=[
            pl.BlockSpec((1, tm, D_MODEL), tile),
            pl.BlockSpec((1, D_MODEL), const2),
            pl.BlockSpec((D_MODEL, 2 * FFN_HIDDEN), const2),
            pl.BlockSpec((8, 2 * FFN_HIDDEN), const2),
            pl.BlockSpec((nch, ck, D_MODEL), const3),
        ],
        out_specs=pl.BlockSpec((1, tm, D_MODEL), tile),
        out_shape=jax.ShapeDtypeStruct((batch, seq, D_MODEL), F32),
        scratch_shapes=[
            pltpu.VMEM((tm, D_MODEL), BF16),
            pltpu.VMEM((nch, 8, 2 * ck), F32),
            pltpu.VMEM((tm + 8, 2 * ck), F32), pltpu.VMEM((tm + 8, 2 * ck), F32),
            pltpu.VMEM((tm, ck), BF16), pltpu.VMEM((tm, ck), BF16),
            pltpu.VMEM((tm, D_MODEL), F32),
        ],
        compiler_params=pltpu.CompilerParams(
            dimension_semantics=("arbitrary", "arbitrary"), vmem_limit_bytes=VMEM_LIMIT,
            ),
        name="conv_ffn",
    )(x, gain.reshape(1, D_MODEL), w_up.astype(BF16), taps, wd)


def _gla_kernel(x_ref, g_ref, win_ref, wa2_ref, ba_ref, og_ref, wout_ref, tri_ref,
                o_ref, state_ref, *, tm):
    i = pl.program_id(1)

    @pl.when(i == 0)
    def _():
        state_ref[...] = jnp.zeros_like(state_ref)

    x = x_ref[0]
    h = _rms(x, g_ref[...]).astype(BF16)
    proj = jnp.dot(h, win_ref[...], preferred_element_type=F32)
    kw, vw = GLA_KEY_WIDTH, GLA_VALUE_WIDTH
    q = proj[:, 0:kw] * (GLA_KEY_DIM ** -0.5)
    k = proj[:, kw:2 * kw]
    v = proj[:, 2 * kw:2 * kw + vw]
    r = proj[:, 2 * kw + vw:2 * kw + 2 * vw]
    a_low = proj[:, 2 * kw + 2 * vw:]
    gate = jnp.dot(a_low.astype(BF16), wa2_ref[...], preferred_element_type=F32) + ba_ref[...]
    log_alpha = (jnp.minimum(gate, 0.0) - _log1pexp_neg_abs(gate)) * (1.0 / GLA_TAU)
    hi, lo = _split_bf16(log_alpha)
    tri = tri_ref[...]
    cum = (jnp.dot(tri, hi, preferred_element_type=F32)
           + jnp.dot(tri, lo, preferred_element_type=F32))

    nchunk = tm // GLA_CHUNK
    klanes = [slice(hd * GLA_KEY_DIM, (hd + 1) * GLA_KEY_DIM) for hd in range(GLA_HEADS)]
    vlanes = [slice(hd * GLA_VALUE_DIM, (hd + 1) * GLA_VALUE_DIM) for hd in range(GLA_HEADS)]
    updates, decays = [], []
    for c in range(nchunk):
        rows = slice(c * GLA_CHUNK, (c + 1) * GLA_CHUNK)
        cum_c = cum[rows, :]
        total = cum_c[GLA_CHUNK - 1:GLA_CHUNK, :]
        k_dec = k[rows, :] * jnp.exp(total - cum_c)
        decay = jnp.broadcast_to(jnp.exp(total), (GLA_CHUNK, kw))
        kd = jnp.concatenate([k_dec, decay], axis=0)
        v_c = v[rows, :].astype(BF16)
        for hd in range(GLA_HEADS):
            kd_t = kd[:, klanes[hd]].T
            decays.append(kd_t[:, GLA_CHUNK:GLA_CHUNK + 1])
            updates.append(jnp.dot(kd_t[:, :GLA_CHUNK].astype(BF16), v_c[:, vlanes[hd]],
                                   preferred_element_type=F32))
    states = []
    for hd in range(GLA_HEADS):
        st = state_ref[hd]
        for c in range(nchunk):
            st = st * decays[c * GLA_HEADS + hd] + updates[c * GLA_HEADS + hd]
            states.append(st.astype(BF16))
        state_ref[hd] = st
    outs = []
    for c in range(nchunk):
        q_c = q[c * GLA_CHUNK:(c + 1) * GLA_CHUNK, :].astype(BF16)
        outs.append(jnp.concatenate(
            [jnp.dot(q_c[:, klanes[hd]], states[hd * nchunk + c], preferred_element_type=F32)
             for hd in range(GLA_HEADS)], axis=1))
    o = jnp.concatenate(outs, axis=0)
    normed = jnp.concatenate(
        [_rms(o[:, hd * GLA_VALUE_DIM:(hd + 1) * GLA_VALUE_DIM], og_ref[...])
         for hd in range(GLA_HEADS)], axis=1)
    gated = (normed * (r * jax.nn.sigmoid(r))).astype(BF16)
    o_ref[0] = x + jnp.dot(gated, wout_ref[...], preferred_element_type=F32)


def _gla_mixer(x, gain, w_in, w_a2, b_a, o_gain, w_out):
    batch, seq, _ = x.shape
    tm = min(GLA_TM, seq)
    in_width = 2 * GLA_KEY_WIDTH + 2 * GLA_VALUE_WIDTH + GLA_GATE_PAD
    pad = GLA_GATE_PAD - GLA_GATE_RANK
    w_in_p = jnp.pad(w_in, ((0, 0), (0, pad))).astype(BF16)
    w_a2_p = jnp.pad(w_a2, ((0, pad), (0, 0))).astype(BF16)
    t = jnp.arange(tm)
    tri = ((t[:, None] // GLA_CHUNK == t[None, :] // GLA_CHUNK)
           & (t[:, None] >= t[None, :])).astype(BF16)
    tile = lambda b, i: (b, i, 0)
    const2 = lambda b, i: (0, 0)
    return pl.pallas_call(
        functools.partial(_gla_kernel, tm=tm),
        grid=(batch, seq // tm),
        in_specs=[
            pl.BlockSpec((1, tm, D_MODEL), tile),
            pl.BlockSpec((1, D_MODEL), const2),
            pl.BlockSpec((D_MODEL, in_width), const2),
            pl.BlockSpec((GLA_GATE_PAD, GLA_KEY_WIDTH), const2),
            pl.BlockSpec((1, GLA_KEY_WIDTH), const2),
            pl.BlockSpec((1, GLA_VALUE_DIM), const2),
            pl.BlockSpec((GLA_VALUE_WIDTH, D_MODEL), const2),
            pl.BlockSpec((tm, tm), const2),
        ],
        out_specs=pl.BlockSpec((1, tm, D_MODEL), tile),
        out_shape=jax.ShapeDtypeStruct((batch, seq, D_MODEL), F32),
        scratch_shapes=[pltpu.VMEM((GLA_HEADS, GLA_KEY_DIM, GLA_VALUE_DIM), F32)],
        compiler_params=pltpu.CompilerParams(
            dimension_semantics=("arbitrary", "arbitrary"), vmem_limit_bytes=VMEM_LIMIT),
        name="gla_mixer",
    )(x, gain.reshape(1, D_MODEL), w_in_p, w_a2_p, b_a.reshape(1, GLA_KEY_WIDTH),
      o_gain.reshape(1, GLA_VALUE_DIM), w_out.astype(BF16), tri)


def kernel(x, mix_norm_even, w_in_even, sb_q_gain, sb_k_gain, pool_w, pool_scale, w_out_even,
           mix_norm_odd, w_in_odd, gla_w_a2, gla_b_a, gla_o_gain, w_out_odd,
           ffn_norm, ffn_w_up, ffn_conv_w, ffn_conv_b, ffn_w_down):
    batch, seq, _ = x.shape
    depth = ffn_norm.shape[0]
    for layer in range(depth):
        i = layer // 2
        if layer % 2 == 0:
            q, k, v, xb = _even_in(x.reshape(batch * seq, D_MODEL), mix_norm_even[i],
                                   w_in_even[i], sb_q_gain[i], sb_k_gain[i])
            o_a = _attention(q, k, v, batch, seq)
            x = _even_out(xb, o_a, x, pool_w[i], pool_scale[i], w_out_even[i])
        else:
            x = _gla_mixer(x, mix_norm_odd[i], w_in_odd[i], gla_w_a2[i], gla_b_a[i],
                           gla_o_gain[i], w_out_odd[i])
        x = _ffn(x, ffn_norm[layer], ffn_w_up[layer], ffn_conv_w[layer],
                 ffn_conv_b[layer], ffn_w_down[layer])
    return x
```

```python
import functools

import jax
import jax.numpy as jnp
from jax import lax
from jax.experimental import pallas as pl
from jax.experimental.pallas import tpu as pltpu

F32 = jnp.float32
BF16 = jnp.bfloat16

D_MODEL = 1024
NORM_EPS = 1e-6
LOG2E = 1.4426950408889634
UNDERFLOW_LOG2 = 160.0

SB_HEADS = 8
SB_HEAD_DIM = 64
SB_WIDTH = SB_HEADS * SB_HEAD_DIM
POOL_WINDOWS = (2, 4, 8, 16)
POOL_GROUP_DIM = 128
POOL_WIDTH = 512
POOL_HALO = 16

GLA_HEADS = 4
GLA_KEY_WIDTH = 512
GLA_VALUE_WIDTH = 1024
GLA_KEY_DIM = 128
GLA_VALUE_DIM = 256
GLA_GATE_RANK = 16
GLA_TAU = 16.0
GLA_CHUNK = 64
GLA_GATE_PAD = 128

FFN_HIDDEN = 2816
FFN_CHUNK = 256
FFN_NCHUNK = FFN_HIDDEN // FFN_CHUNK
assert FFN_NCHUNK % 2 == 1 and FFN_NCHUNK >= 3

IN_TM = 512
ATT_TQ = 512
ATT_TK = 128
ATT_NB = 2
OUT_TM = 512
FFN_TM = 512
GLA_TM = 256

VMEM_LIMIT = 56 * 1024 * 1024


def _rms(x, gain):
    ms = jnp.mean(x * x, axis=-1, keepdims=True)
    return x * lax.rsqrt(ms + NORM_EPS) * gain


def _log1pexp_neg_abs(z):
    return jnp.log(1.0 + jnp.exp(jnp.minimum(z, -z)))


def _split_bf16(x):
    hi = x.astype(BF16)
    lo = (x - hi.astype(F32)).astype(BF16)
    return hi, lo


def _even_in_kernel(x_ref, g_ref, w_ref, qg_ref, kg_ref, bd_ref,
                    q_ref, k_ref, v_ref, xb_ref):
    h = _rms(x_ref[...], g_ref[...]).astype(BF16)
    proj = jnp.dot(h, w_ref[...], preferred_element_type=F32)

    def head_norm(t, gain):
        ss = jnp.dot((t * t).astype(BF16), bd_ref[...], preferred_element_type=F32)
        return t * lax.rsqrt(ss * (1.0 / SB_HEAD_DIM) + NORM_EPS) * gain

    q = head_norm(proj[:, 0:SB_WIDTH], qg_ref[...])
    q_ref[...] = (q * (SB_HEAD_DIM ** -0.5 * LOG2E)).astype(BF16)
    k_ref[...] = head_norm(proj[:, SB_WIDTH:2 * SB_WIDTH], kg_ref[...]).astype(BF16)
    v_ref[...] = proj[:, 2 * SB_WIDTH:3 * SB_WIDTH].astype(BF16)
    xb_ref[...] = proj[:, 3 * SB_WIDTH:]


def _even_in(x2d, gain, w_in, q_gain, k_gain):
    n = x2d.shape[0]
    tm = min(IN_TM, n)
    head = jnp.arange(SB_WIDTH) // SB_HEAD_DIM
    blockdiag = (head[:, None] == head[None, :]).astype(BF16)
    width = w_in.shape[1]
    const = lambda i: (0, 0)
    tile = lambda i: (i, 0)
    return pl.pallas_call(
        _even_in_kernel,
        grid=(n // tm,),
        in_specs=[
            pl.BlockSpec((tm, D_MODEL), tile),
            pl.BlockSpec((1, D_MODEL), const),
            pl.BlockSpec((D_MODEL, width), const),
            pl.BlockSpec((1, SB_WIDTH), const),
            pl.BlockSpec((1, SB_WIDTH), const),
            pl.BlockSpec((SB_WIDTH, SB_WIDTH), const),
        ],
        out_specs=[
            pl.BlockSpec((tm, SB_WIDTH), tile),
            pl.BlockSpec((tm, SB_WIDTH), tile),
            pl.BlockSpec((tm, SB_WIDTH), tile),
            pl.BlockSpec((tm, POOL_WIDTH), tile),
        ],
        out_shape=[
            jax.ShapeDtypeStruct((n, SB_WIDTH), BF16),
            jax.ShapeDtypeStruct((n, SB_WIDTH), BF16),
            jax.ShapeDtypeStruct((n, SB_WIDTH), BF16),
            jax.ShapeDtypeStruct((n, POOL_WIDTH), F32),
        ],
        compiler_params=pltpu.CompilerParams(
            dimension_semantics=("arbitrary",), vmem_limit_bytes=VMEM_LIMIT),
        name="even_in",
    )(x2d, gain.reshape(1, D_MODEL), w_in.astype(BF16),
      jnp.tile(q_gain, SB_HEADS).reshape(1, SB_WIDTH),
      jnp.tile(k_gain, SB_HEADS).reshape(1, SB_WIDTH), blockdiag)


def _attn_kernel(q_ref, k_ref, v_ref, m2_ref, o_ref, za_ref, zb_ref, wa_ref, wb_ref,
                 c0_ref, c1_ref, acc_ref, *, tq, tk, nb):
    i = pl.program_id(2)
    q = q_ref[0]
    lane = lax.broadcasted_iota(jnp.int32, (tk, 2 * SB_HEAD_DIM), 1)
    first_head = lane < SB_HEAD_DIM
    zero = jnp.zeros((), BF16)

    def stack_heads(blk):
        return jnp.concatenate(
            [jnp.where(first_head, blk, zero), jnp.where(first_head, zero, blk)], axis=0)

    def stacked(ref, start, n):
        return jnp.concatenate(
            [stack_heads(ref[0, pl.ds(start + u * tk, tk), :]) for u in range(n)], axis=0)

    def scores(q_rows, start, n):
        return lax.dot_general(q_rows, stacked(k_ref, start, n), (((1,), (1,)), ((), ())),
                               preferred_element_type=F32)

    def attend(w, start, n):
        return jnp.dot(w, stacked(v_ref, start, n), preferred_element_type=F32)

    def block_weights(z2, c0, c1, visible):
        ws, cs = [], []
        for hh, c in ((0, c0), (1, c1)):
            z = z2[:, hh * tk:(hh + 1) * tk]
            softplus = jnp.maximum(z, 0.0) + jnp.log2(1.0 + jnp.exp2(jnp.minimum(z, -z)))
            if visible is not None:
                softplus = jnp.where(visible, softplus, 0.0)
            hi, lo = _split_bf16(softplus)
            cum = jnp.dot(jnp.concatenate([hi, lo], axis=1), m2_ref[...],
                          preferred_element_type=F32)
            w = jnp.exp2(z + cum[:, :tk] + c)
            if visible is not None:
                w = jnp.where(visible, w, 0.0)
            ws.append(w.astype(BF16))
            cs.append(c + cum[:, tk:])
        return jnp.concatenate(ws, axis=1), cs[0], cs[1]

    ratio = tq // tk
    c0 = jnp.zeros((tq, tk), F32)
    c1 = jnp.zeros((tq, tk), F32)
    acc = jnp.zeros((tq, 2 * SB_HEAD_DIM), F32)
    for u in reversed(range(ratio)):
        r0 = u * tk
        start = pl.multiple_of(i * tq + r0, tk)
        row = lax.broadcasted_iota(jnp.int32, (tq - r0, tk), 0)
        col = lax.broadcasted_iota(jnp.int32, (tq - r0, tk), 1)
        w, p0, p1 = block_weights(scores(q[r0:, :], start, 1), c0[r0:, :], c1[r0:, :], col < row)
        pacc = acc[r0:, :] + attend(w, start, 1)
        if r0:
            c0 = jnp.concatenate([c0[:r0, :], p0], axis=0)
            c1 = jnp.concatenate([c1[:r0, :], p1], axis=0)
            acc = jnp.concatenate([acc[:r0, :], pacc], axis=0)
        else:
            c0, c1, acc = p0, p1, pacc

    ngroups = i * (ratio // nb)

    def group_start(g):
        return pl.multiple_of(jnp.maximum(i * ratio - (g + 1) * nb, 0) * tk, tk)

    z_near = scores(q, group_start(0), nb)
    w_near = [None] * nb
    for u in reversed(range(nb)):
        w_near[u], c0, c1 = block_weights(z_near[:, u * 2 * tk:(u + 1) * 2 * tk], c0, c1, None)
    w_near = jnp.where(i > 0, jnp.concatenate(w_near, axis=1), zero)
    acc = acc + attend(w_near, group_start(0), nb)


    def stage(g, z_in, z_out, w_in, w_out):
        acc_ref[...] += attend(w_in[...], group_start(g - 1), nb)
        z_out[...] = scores(q, group_start(g + 1), nb)
        for u in reversed(range(nb)):
            cols = slice(u * 2 * tk, (u + 1) * 2 * tk)
            w, c0, c1 = block_weights(z_in[:, cols], c0_ref[...], c1_ref[...], None)
            c0_ref[...] = c0
            c1_ref[...] = c1
            w_out[:, cols] = w

    def any_weight_left():
        return jnp.maximum(jnp.max(c0_ref[...]), jnp.max(c1_ref[...])) > -UNDERFLOW_LOG2

    c0_ref[...] = c0
    c1_ref[...] = c1
    acc_ref[...] = acc
    farther = jnp.logical_and(ngroups > 1, any_weight_left())

    @pl.when(farther)
    def _():
        wb_ref[...] = jnp.zeros_like(wb_ref)
        za_ref[...] = scores(q, group_start(1), nb)

    def cond(state):
        g, alive = state
        return jnp.logical_and(g < ngroups, alive)

    def body(state):
        g, _ = state
        stage(g, za_ref, zb_ref, wb_ref, wa_ref)
        more = jnp.logical_and(g + 1 < ngroups, any_weight_left())

        @pl.when(more)
        def _():
            stage(g + 1, zb_ref, za_ref, wa_ref, wb_ref)

        return g + 1 + more.astype(jnp.int32), any_weight_left()

    last, _ = lax.while_loop(cond, body, (jnp.int32(1), farther))

    stages = last - 1

    @pl.when(stages % 2 == 1)
    def _():
        acc_ref[...] += attend(wa_ref[...], group_start(last - 1), nb)

    @pl.when(jnp.logical_and(stages > 0, stages % 2 == 0))
    def _():
        acc_ref[...] += attend(wb_ref[...], group_start(last - 1), nb)

    o_ref[0] = acc_ref[...].astype(BF16)


def _attention(q, k, v, batch, seq):
    tq = min(ATT_TQ, seq)
    tk = min(ATT_TK, tq)
    q3 = q.reshape(batch, seq, SB_WIDTH)
    k3 = k.reshape(batch, seq, SB_WIDTH)
    v3 = v.reshape(batch, seq, SB_WIDTH)
    nb = ATT_NB
    assert seq % tq == 0 and (tq // tk) % (2 * nb) == 0
    r = jnp.arange(2 * tk) % tk
    ccol = jnp.arange(2 * tk)
    m2 = -jnp.where(ccol[None, :] < tk, r[:, None] >= ccol[None, :], True).astype(BF16)
    pair = 2 * SB_HEAD_DIM
    return pl.pallas_call(
        functools.partial(_attn_kernel, tq=tq, tk=tk, nb=nb),
        grid=(batch, SB_WIDTH // pair, seq // tq),
        scratch_shapes=[
            pltpu.VMEM((tq, nb * 2 * tk), F32), pltpu.VMEM((tq, nb * 2 * tk), F32),
            pltpu.VMEM((tq, nb * 2 * tk), BF16), pltpu.VMEM((tq, nb * 2 * tk), BF16),
            pltpu.VMEM((tq, tk), F32), pltpu.VMEM((tq, tk), F32),
            pltpu.VMEM((tq, 2 * SB_HEAD_DIM), F32),
        ],
        in_specs=[
            pl.BlockSpec((1, tq, pair), lambda b, p, i: (b, i, p)),
            pl.BlockSpec((1, seq, pair), lambda b, p, i: (b, 0, p)),
            pl.BlockSpec((1, seq, pair), lambda b, p, i: (b, 0, p)),
            pl.BlockSpec((2 * tk, 2 * tk), lambda b, p, i: (0, 0)),
        ],
        out_specs=pl.BlockSpec((1, tq, pair), lambda b, p, i: (b, i, p)),
        out_shape=jax.ShapeDtypeStruct((batch, seq, SB_WIDTH), BF16),
        compiler_params=pltpu.CompilerParams(
            dimension_semantics=("arbitrary", "arbitrary", "arbitrary"),
            vmem_limit_bytes=VMEM_LIMIT),
        name="sb_attention",
    )(q3, k3, v3, m2)


def _even_out_kernel(xb_ref, halo_ref, oa_ref, x_ref, pw_ref, ps_ref, wo_ref, o_ref, *, tm):
    i = pl.program_id(1)
    xb = xb_ref[0]
    halo = jnp.where(i > 0, halo_ref[0], 0.0)
    ext = jnp.concatenate([halo, xb], axis=0)
    pos = (i * tm + 1 + lax.broadcasted_iota(jnp.int32, (tm, 1), 0)).astype(F32)
    y = jnp.dot(oa_ref[0], wo_ref[0:SB_WIDTH, :], preferred_element_type=F32)
    pooled_out = []
    for g, w in enumerate(POOL_WINDOWS):
        lanes = slice(g * POOL_GROUP_DIM, (g + 1) * POOL_GROUP_DIM)
        s = ext[:, lanes]
        sh = 1
        while sh < w:
            s = s + pltpu.roll(s, sh, axis=0)
            sh *= 2
        window_sum = s[POOL_HALO:, :]
        pooled = window_sum / jnp.minimum(pos, float(w)) - xb[:, lanes]
        ob = jnp.dot(pooled.astype(BF16), pw_ref[g], preferred_element_type=F32)
        pooled_out.append((ob * ps_ref[:, lanes]).astype(BF16))
    y = y + jnp.dot(jnp.concatenate(pooled_out, axis=1), wo_ref[SB_WIDTH:, :],
                    preferred_element_type=F32)
    o_ref[0] = x_ref[0] + y


def _even_out(xb, o_a, x, pool_w, pool_scale, w_out):
    batch, seq, _ = x.shape
    tm = min(OUT_TM, seq)
    xb3 = xb.reshape(batch, seq, POOL_WIDTH)
    hb = tm // POOL_HALO
    const2 = lambda b, i: (0, 0)
    tile = lambda b, i: (b, i, 0)
    return pl.pallas_call(
        functools.partial(_even_out_kernel, tm=tm),
        grid=(batch, seq // tm),
        in_specs=[
            pl.BlockSpec((1, tm, POOL_WIDTH), tile),
            pl.BlockSpec((1, POOL_HALO, POOL_WIDTH),
                         lambda b, i: (b, jnp.maximum(i * hb - 1, 0), 0)),
            pl.BlockSpec((1, tm, SB_WIDTH), tile),
            pl.BlockSpec((1, tm, D_MODEL), tile),
            pl.BlockSpec((len(POOL_WINDOWS), POOL_GROUP_DIM, POOL_GROUP_DIM),
                         lambda b, i: (0, 0, 0)),
            pl.BlockSpec((1, POOL_WIDTH), const2),
            pl.BlockSpec((D_MODEL, D_MODEL), const2),
        ],
        out_specs=pl.BlockSpec((1, tm, D_MODEL), tile),
        out_shape=jax.ShapeDtypeStruct((batch, seq, D_MODEL), F32),
        compiler_params=pltpu.CompilerParams(
            dimension_semantics=("arbitrary", "arbitrary"), vmem_limit_bytes=VMEM_LIMIT),
        name="even_out",
    )(xb3, xb3, o_a, x, pool_w.astype(BF16), pool_scale.reshape(1, POOL_WIDTH),
      w_out.astype(BF16))


def _ffn_kernel(x_ref, g_ref, wup_ref, cw_ref, wd_ref, o_ref,
                h_ref, carry_ref, ua_ref, ub_ref, acta_ref, actb_ref, acc_ref, *, tm):
    i = pl.program_id(1)
    nch = FFN_NCHUNK

    @pl.when(i == 0)
    def _():
        carry_ref[...] = jnp.zeros_like(carry_ref)

    x = x_ref[0]
    h_ref[...] = _rms(x, g_ref[...]).astype(BF16)
    acc_ref[...] = x

    def columns(c):
        return (slice(c * FFN_CHUNK, (c + 1) * FFN_CHUNK),
                slice(FFN_HIDDEN + c * FFN_CHUNK, FFN_HIDDEN + (c + 1) * FFN_CHUNK))

    def up(c, u_ref):
        h = h_ref[...]
        u = jnp.concatenate([jnp.dot(h, wup_ref[:, cols], preferred_element_type=F32)
                             for cols in columns(c)], axis=1)
        u_ref[0:8, :] = carry_ref[c]
        u_ref[8:8 + tm, :] = u
        carry_ref[c] = u[tm - 8:, :]

    def activate(c, u_ref, act_ref):
        cw = jnp.concatenate([cw_ref[:, cols] for cols in columns(c)], axis=1)
        conv = (cw[3:4, :] + cw[2:3, :] * u_ref[8:8 + tm, :]
                + cw[1:2, :] * u_ref[7:7 + tm, :]
                + cw[0:1, :] * u_ref[6:6 + tm, :])
        a = conv[:, :FFN_CHUNK]
        gate = conv[:, FFN_CHUNK:]
        act_ref[...] = (a * jax.nn.sigmoid(a) * gate).astype(BF16)

    def down(c, act_ref):
        acc_ref[...] += jnp.dot(act_ref[...], wd_ref[c], preferred_element_type=F32)

    def stage(c, u_in, u_out, act_in, act_out):
        activate(c, u_in, act_out)
        if act_in is not None:
            down(c - 1, act_in)
        if u_out is not None:
            up(c + 1, u_out)

    u_bufs = (ua_ref, ub_ref)
    act_bufs = (acta_ref, actb_ref)
    up(0, u_bufs[0])
    for c in range(nch):
        stage(c, u_bufs[c % 2],
              u_bufs[(c + 1) % 2] if c + 1 < nch else None,
              act_bufs[(c + 1) % 2] if c > 0 else None,
              act_bufs[c % 2])
    down(nch - 1, act_bufs[(nch - 1) % 2])
    o_ref[0] = acc_ref[...]


def _ffn(x, gain, w_up, conv_w, conv_b, w_down):
    batch, seq, _ = x.shape
    tm = min(FFN_TM, seq)
    ck, nch = FFN_CHUNK, FFN_NCHUNK

    taps = jnp.concatenate([conv_w, conv_b[None, :],
                            jnp.zeros((4, 2 * FFN_HIDDEN), F32)], axis=0)
    wd = w_down.astype(BF16).reshape(nch, ck, D_MODEL)
    tile = lambda b, i: (b, i, 0)
    const2 = lambda b, i: (0, 0)
    const3 = lambda b, i: (0, 0, 0)
    return pl.pallas_call(
        functools.partial(_ffn_kernel, tm=tm),
        grid=(batch, seq // tm),
        in_specs=[
            pl.BlockSpec((1, tm, D_MODEL), tile),
            pl.BlockSpec((1, D_MODEL), const2),
            pl.BlockSpec((D_MODEL, 2 * FFN_HIDDEN), const2),
            pl.BlockSpec((8, 2 * FFN_HIDDEN), const2),
            pl.BlockSpec((nch, ck, D_MODEL), const3),
        ],
        out_specs=pl.BlockSpec((1, tm, D_MODEL), tile),
        out_shape=jax.ShapeDtypeStruct((batch, seq, D_MODEL), F32),
        scratch_shapes=[
            pltpu.VMEM((tm, D_MODEL), BF16),
            pltpu.VMEM((nch, 8, 2 * ck), F32),
            pltpu.VMEM((tm + 8, 2 * ck), F32), pltpu.VMEM((tm + 8, 2 * ck), F32),
            pltpu.VMEM((tm, ck), BF16), pltpu.VMEM((tm, ck), BF16),
            pltpu.VMEM((tm, D_MODEL), F32),
        ],
        compiler_params=pltpu.CompilerParams(
            dimension_semantics=("arbitrary", "arbitrary"), vmem_limit_bytes=VMEM_LIMIT),
        name="conv_ffn",
    )(x, gain.reshape(1, D_MODEL), w_up.astype(BF16), taps, wd)


def _gla_kernel(x_ref, g_ref, win_ref, wa2_ref, ba_ref, og_ref, wout_ref, tri_ref,
                o_ref, state_ref, *, tm):
    i = pl.program_id(1)

    @pl.when(i == 0)
    def _():
        state_ref[...] = jnp.zeros_like(state_ref)

    x = x_ref[0]
    h = _rms(x, g_ref[...]).astype(BF16)
    proj = jnp.dot(h, win_ref[...], preferred_element_type=F32)
    kw, vw = GLA_KEY_WIDTH, GLA_VALUE_WIDTH
    q = proj[:, 0:kw] * (GLA_KEY_DIM ** -0.5)
    k = proj[:, kw:2 * kw]
    v = proj[:, 2 * kw:2 * kw + vw]
    r = proj[:, 2 * kw + vw:2 * kw + 2 * vw]
    a_low = proj[:, 2 * kw + 2 * vw:]
    gate = jnp.dot(a_low.astype(BF16), wa2_ref[...], preferred_element_type=F32) + ba_ref[...]
    log_alpha = (jnp.minimum(gate, 0.0) - _log1pexp_neg_abs(gate)) * (1.0 / GLA_TAU)
    hi, lo = _split_bf16(log_alpha)
    tri = tri_ref[...]
    cum = (jnp.dot(tri, hi, preferred_element_type=F32)
           + jnp.dot(tri, lo, preferred_element_type=F32))

    nchunk = tm // GLA_CHUNK
    klanes = [slice(hd * GLA_KEY_DIM, (hd + 1) * GLA_KEY_DIM) for hd in range(GLA_HEADS)]
    vlanes = [slice(hd * GLA_VALUE_DIM, (hd + 1) * GLA_VALUE_DIM) for hd in range(GLA_HEADS)]
    updates, decays = [], []
    for c in range(nchunk):
        rows = slice(c * GLA_CHUNK, (c + 1) * GLA_CHUNK)
        cum_c = cum[rows, :]
        total = cum_c[GLA_CHUNK - 1:GLA_CHUNK, :]
        k_dec = k[rows, :] * jnp.exp(total - cum_c)
        decay = jnp.broadcast_to(jnp.exp(total), (GLA_CHUNK, kw))
        kd = jnp.concatenate([k_dec, decay], axis=0)
        v_c = v[rows, :].astype(BF16)
        for hd in range(GLA_HEADS):
            kd_t = kd[:, klanes[hd]].T
            decays.append(kd_t[:, GLA_CHUNK:GLA_CHUNK + 1])
            updates.append(jnp.dot(kd_t[:, :GLA_CHUNK].astype(BF16), v_c[:, vlanes[hd]],
                                   preferred_element_type=F32))
    states = []
    for hd in range(GLA_HEADS):
        st = state_ref[hd]
        for c in range(nchunk):
            st = st * decays[c * GLA_HEADS + hd] + updates[c * GLA_HEADS + hd]
            states.append(st.astype(BF16))
        state_ref[hd] = st
    outs = []
    for c in range(nchunk):
        q_c = q[c * GLA_CHUNK:(c + 1) * GLA_CHUNK, :].astype(BF16)
        outs.append(jnp.concatenate(
            [jnp.dot(q_c[:, klanes[hd]], states[hd * nchunk + c], preferred_element_type=F32)
             for hd in range(GLA_HEADS)], axis=1))
    o = jnp.concatenate(outs, axis=0)
    normed = jnp.concatenate(
        [_rms(o[:, hd * GLA_VALUE_DIM:(hd + 1) * GLA_VALUE_DIM], og_ref[...])
         for hd in range(GLA_HEADS)], axis=1)
    gated = (normed * (r * jax.nn.sigmoid(r))).astype(BF16)
    o_ref[0] = x + jnp.dot(gated, wout_ref[...], preferred_element_type=F32)


def _gla_mixer(x, gain, w_in, w_a2, b_a, o_gain, w_out):
    batch, seq, _ = x.shape
    tm = min(GLA_TM, seq)
    in_width = 2 * GLA_KEY_WIDTH + 2 * GLA_VALUE_WIDTH + GLA_GATE_PAD
    pad = GLA_GATE_PAD - GLA_GATE_RANK
    w_in_p = jnp.pad(w_in, ((0, 0), (0, pad))).astype(BF16)
    w_a2_p = jnp.pad(w_a2, ((0, pad), (0, 0))).astype(BF16)
    t = jnp.arange(tm)
    tri = ((t[:, None] // GLA_CHUNK == t[None, :] // GLA_CHUNK)
           & (t[:, None] >= t[None, :])).astype(BF16)
    tile = lambda b, i: (b, i, 0)
    const2 = lambda b, i: (0, 0)
    return pl.pallas_call(
        functools.partial(_gla_kernel, tm=tm),
        grid=(batch, seq // tm),
        in_specs=[
            pl.BlockSpec((1, tm, D_MODEL), tile),
            pl.BlockSpec((1, D_MODEL), const2),
            pl.BlockSpec((D_MODEL, in_width), const2),
            pl.BlockSpec((GLA_GATE_PAD, GLA_KEY_WIDTH), const2),
            pl.BlockSpec((1, GLA_KEY_WIDTH), const2),
            pl.BlockSpec((1, GLA_VALUE_DIM), const2),
            pl.BlockSpec((GLA_VALUE_WIDTH, D_MODEL), const2),
            pl.BlockSpec((tm, tm), const2),
        ],
        out_specs=pl.BlockSpec((1, tm, D_MODEL), tile),
        out_shape=jax.ShapeDtypeStruct((batch, seq, D_MODEL), F32),
        scratch_shapes=[pltpu.VMEM((GLA_HEADS, GLA_KEY_DIM, GLA_VALUE_DIM), F32)],
        compiler_params=pltpu.CompilerParams(
            dimension_semantics=("arbitrary", "arbitrary"), vmem_limit_bytes=VMEM_LIMIT),
        name="gla_mixer",
    )(x, gain.reshape(1, D_MODEL), w_in_p, w_a2_p, b_a.reshape(1, GLA_KEY_WIDTH),
      o_gain.reshape(1, GLA_VALUE_DIM), w_out.astype(BF16), tri)


def kernel(x, mix_norm_even, w_in_even, sb_q_gain, sb_k_gain, pool_w, pool_scale, w_out_even,
           mix_norm_odd, w_in_odd, gla_w_a2, gla_b_a, gla_o_gain, w_out_odd,
           ffn_norm, ffn_w_up, ffn_conv_w, ffn_conv_b, ffn_w_down):
    batch, seq, _ = x.shape
    depth = ffn_norm.shape[0]
    for layer in range(depth):
        i = layer // 2
        if layer % 2 == 0:
            q, k, v, xb = _even_in(x.reshape(batch * seq, D_MODEL), mix_norm_even[i],
                                   w_in_even[i], sb_q_gain[i], sb_k_gain[i])
            o_a = _attention(q, k, v, batch, seq)
            x = _even_out(xb, o_a, x, pool_w[i], pool_scale[i], w_out_even[i])
        else:
            x = _gla_mixer(x, mix_norm_odd[i], w_in_odd[i], gla_w_a2[i], gla_b_a[i],
                           gla_o_gain[i], w_out_odd[i])
        x = _ffn(x, ffn_norm[layer], ffn_w_up[layer], ffn_conv_w[layer],
                 ffn_conv_b[layer], ffn_w_down[layer])
    return x
```

```python
import functools

import jax
import jax.numpy as jnp
from jax import lax
from jax.experimental import pallas as pl
from jax.experimental.pallas import tpu as pltpu

F32 = jnp.float32
BF16 = jnp.bfloat16

D_MODEL = 1024
NORM_EPS = 1e-6
LOG2E = 1.4426950408889634
UNDERFLOW_LOG2 = 160.0

SB_HEADS = 8
SB_HEAD_DIM = 64
SB_WIDTH = SB_HEADS * SB_HEAD_DIM
POOL_WINDOWS = (2, 4, 8, 16)
POOL_GROUP_DIM = 128
POOL_WIDTH = 512
POOL_HALO = 16

GLA_HEADS = 4
GLA_KEY_WIDTH = 512
GLA_VALUE_WIDTH = 1024
GLA_KEY_DIM = 128
GLA_VALUE_DIM = 256
GLA_GATE_RANK = 16
GLA_TAU = 16.0
GLA_CHUNK = 64
GLA_GATE_PAD = 128

FFN_HIDDEN = 2816
FFN_CHUNK = 256
FFN_NCHUNK = FFN_HIDDEN // FFN_CHUNK

IN_TM = 512
ATT_TQ = 512
ATT_TK = 128
ATT_NB = 2
OUT_TM = 512
FFN_TM = 512
GLA_TM = 256

VMEM_LIMIT = 56 * 1024 * 1024


def _rms(x, gain):
    ms = jnp.mean(x * x, axis=-1, keepdims=True)
    return x * lax.rsqrt(ms + NORM_EPS) * gain


def _log1pexp_neg_abs(z):
    return jnp.log(1.0 + jnp.exp(jnp.minimum(z, -z)))


def _split_bf16(x):
    hi = x.astype(BF16)
    lo = (x - hi.astype(F32)).astype(BF16)
    return hi, lo


def _even_in_kernel(x_ref, g_ref, w_ref, qg_ref, kg_ref, bd_ref,
                    q_ref, k_ref, v_ref, xb_ref):
    h = _rms(x_ref[...], g_ref[...]).astype(BF16)
    proj = jnp.dot(h, w_ref[...], preferred_element_type=F32)

    def head_norm(t, gain):
        ss = jnp.dot((t * t).astype(BF16), bd_ref[...], preferred_element_type=F32)
        return t * lax.rsqrt(ss * (1.0 / SB_HEAD_DIM) + NORM_EPS) * gain

    q = head_norm(proj[:, 0:SB_WIDTH], qg_ref[...])
    q_ref[...] = (q * (SB_HEAD_DIM ** -0.5 * LOG2E)).astype(BF16)
    k_ref[...] = head_norm(proj[:, SB_WIDTH:2 * SB_WIDTH], kg_ref[...]).astype(BF16)
    v_ref[...] = proj[:, 2 * SB_WIDTH:3 * SB_WIDTH].astype(BF16)
    xb_ref[...] = proj[:, 3 * SB_WIDTH:]


def _even_in(x2d, gain, w_in, q_gain, k_gain):
    n = x2d.shape[0]
    tm = min(IN_TM, n)
    head = jnp.arange(SB_WIDTH) // SB_HEAD_DIM
    blockdiag = (head[:, None] == head[None, :]).astype(BF16)
    width = w_in.shape[1]
    const = lambda i: (0, 0)
    tile = lambda i: (i, 0)
    return pl.pallas_call(
        _even_in_kernel,
        grid=(n // tm,),
        in_specs=[
            pl.BlockSpec((tm, D_MODEL), tile),
            pl.BlockSpec((1, D_MODEL), const),
            pl.BlockSpec((D_MODEL, width), const),
            pl.BlockSpec((1, SB_WIDTH), const),
            pl.BlockSpec((1, SB_WIDTH), const),
            pl.BlockSpec((SB_WIDTH, SB_WIDTH), const),
        ],
        out_specs=[
            pl.BlockSpec((tm, SB_WIDTH), tile),
            pl.BlockSpec((tm, SB_WIDTH), tile),
            pl.BlockSpec((tm, SB_WIDTH), tile),
            pl.BlockSpec((tm, POOL_WIDTH), tile),
        ],
        out_shape=[
            jax.ShapeDtypeStruct((n, SB_WIDTH), BF16),
            jax.ShapeDtypeStruct((n, SB_WIDTH), BF16),
            jax.ShapeDtypeStruct((n, SB_WIDTH), BF16),
            jax.ShapeDtypeStruct((n, POOL_WIDTH), F32),
        ],
        compiler_params=pltpu.CompilerParams(
            dimension_semantics=("arbitrary",), vmem_limit_bytes=VMEM_LIMIT),
        name="even_in",
    )(x2d, gain.reshape(1, D_MODEL), w_in.astype(BF16),
      jnp.tile(q_gain, SB_HEADS).reshape(1, SB_WIDTH),
      jnp.tile(k_gain, SB_HEADS).reshape(1, SB_WIDTH), blockdiag)


def _attn_kernel(q_ref, k_ref, v_ref, m2_ref, o_ref, za_ref, zb_ref, wa_ref, wb_ref,
                 c0_ref, c1_ref, acc_ref, *, tq, tk, nb):
    i = pl.program_id(2)
    q = q_ref[0]
    lane = lax.broadcasted_iota(jnp.int32, (tk, 2 * SB_HEAD_DIM), 1)
    first_head = lane < SB_HEAD_DIM
    zero = jnp.zeros((), BF16)

    def stack_heads(blk):
        return jnp.concatenate(
            [jnp.where(first_head, blk, zero), jnp.where(first_head, zero, blk)], axis=0)

    def stacked(ref, start, n):
        return jnp.concatenate(
            [stack_heads(ref[0, pl.ds(start + u * tk, tk), :]) for u in range(n)], axis=0)

    def scores(q_rows, start, n):
        return lax.dot_general(q_rows, stacked(k_ref, start, n), (((1,), (1,)), ((), ())),
                               preferred_element_type=F32)

    def attend(w, start, n):
        return jnp.dot(w, stacked(v_ref, start, n), preferred_element_type=F32)

    def block_weights(z2, c0, c1, visible):
        ws, cs = [], []
        for hh, c in ((0, c0), (1, c1)):
            z = z2[:, hh * tk:(hh + 1) * tk]
            softplus = jnp.maximum(z, 0.0) + jnp.log2(1.0 + jnp.exp2(jnp.minimum(z, -z)))
            if visible is not None:
                softplus = jnp.where(visible, softplus, 0.0)
            hi, lo = _split_bf16(softplus)
            cum = jnp.dot(jnp.concatenate([hi, lo], axis=1), m2_ref[...],
                          preferred_element_type=F32)
            w = jnp.exp2(z + cum[:, :tk] + c)
            if visible is not None:
                w = jnp.where(visible, w, 0.0)
            ws.append(w.astype(BF16))
            cs.append(c + cum[:, tk:])
        return jnp.concatenate(ws, axis=1), cs[0], cs[1]

    ratio = tq // tk
    c0 = jnp.zeros((tq, tk), F32)
    c1 = jnp.zeros((tq, tk), F32)
    acc = jnp.zeros((tq, 2 * SB_HEAD_DIM), F32)
    for u in reversed(range(ratio)):
        r0 = u * tk
        start = pl.multiple_of(i * tq + r0, tk)
        row = lax.broadcasted_iota(jnp.int32, (tq - r0, tk), 0)
        col = lax.broadcasted_iota(jnp.int32, (tq - r0, tk), 1)
        w, p0, p1 = block_weights(scores(q[r0:, :], start, 1), c0[r0:, :], c1[r0:, :], col < row)
        pacc = acc[r0:, :] + attend(w, start, 1)
        if r0:
            c0 = jnp.concatenate([c0[:r0, :], p0], axis=0)
            c1 = jnp.concatenate([c1[:r0, :], p1], axis=0)
            acc = jnp.concatenate([acc[:r0, :], pacc], axis=0)
        else:
            c0, c1, acc = p0, p1, pacc

    ngroups = i * (ratio // nb)

    def group_start(g):
        return pl.multiple_of(jnp.maximum(i * ratio - (g + 1) * nb, 0) * tk, tk)

    z_near = scores(q, group_start(0), nb)
    w_near = [None] * nb
    for u in reversed(range(nb)):
        w_near[u], c0, c1 = block_weights(z_near[:, u * 2 * tk:(u + 1) * 2 * tk], c0, c1, None)
    w_near = jnp.where(i > 0, jnp.concatenate(w_near, axis=1), zero)
    acc = acc + attend(w_near, group_start(0), nb)


    def stage(g, z_in, z_out, w_in, w_out):
        acc_ref[...] += attend(w_in[...], group_start(g - 1), nb)
        z_out[...] = scores(q, group_start(g + 1), nb)
        for u in reversed(range(nb)):
            cols = slice(u * 2 * tk, (u + 1) * 2 * tk)
            w, c0, c1 = block_weights(z_in[:, cols], c0_ref[...], c1_ref[...], None)
            c0_ref[...] = c0
            c1_ref[...] = c1
            w_out[:, cols] = w

    def any_weight_left():
        return jnp.maximum(jnp.max(c0_ref[...]), jnp.max(c1_ref[...])) > -UNDERFLOW_LOG2

    c0_ref[...] = c0
    c1_ref[...] = c1
    acc_ref[...] = acc
    farther = jnp.logical_and(ngroups > 1, any_weight_left())

    @pl.when(farther)
    def _():
        wb_ref[...] = jnp.zeros_like(wb_ref)
        za_ref[...] = scores(q, group_start(1), nb)

    def cond(state):
        g, alive = state
        return jnp.logical_and(g < ngroups, alive)

    def body(state):
        g, _ = state
        stage(g, za_ref, zb_ref, wb_ref, wa_ref)
        more = jnp.logical_and(g + 1 < ngroups, any_weight_left())

        @pl.when(more)
        def _():
            stage(g + 1, zb_ref, za_ref, wa_ref, wb_ref)

        return g + 1 + more.astype(jnp.int32), any_weight_left()

    last, _ = lax.while_loop(cond, body, (jnp.int32(1), farther))

    stages = last - 1

    @pl.when(stages % 2 == 1)
    def _():
        acc_ref[...] += attend(wa_ref[...], group_start(last - 1), nb)

    @pl.when(jnp.logical_and(stages > 0, stages % 2 == 0))
    def _():
        acc_ref[...] += attend(wb_ref[...], group_start(last - 1), nb)

    o_ref[0] = acc_ref[...].astype(BF16)


def _attention(q, k, v, batch, seq):
    tq = min(ATT_TQ, seq)
    tk = min(ATT_TK, tq)
    q3 = q.reshape(batch, seq, SB_WIDTH)
    k3 = k.reshape(batch, seq, SB_WIDTH)
    v3 = v.reshape(batch, seq, SB_WIDTH)
    nb = ATT_NB
    assert seq % tq == 0 and (tq // tk) % (2 * nb) == 0
    r = jnp.arange(2 * tk) % tk
    ccol = jnp.arange(2 * tk)
    m2 = -jnp.where(ccol[None, :] < tk, r[:, None] >= ccol[None, :], True).astype(BF16)
    pair = 2 * SB_HEAD_DIM
    return pl.pallas_call(
        functools.partial(_attn_kernel, tq=tq, tk=tk, nb=nb),
        grid=(batch, SB_WIDTH // pair, seq // tq),
        scratch_shapes=[
            pltpu.VMEM((tq, nb * 2 * tk), F32), pltpu.VMEM((tq, nb * 2 * tk), F32),
            pltpu.VMEM((tq, nb * 2 * tk), BF16), pltpu.VMEM((tq, nb * 2 * tk), BF16),
            pltpu.VMEM((tq, tk), F32), pltpu.VMEM((tq, tk), F32),
            pltpu.VMEM((tq, 2 * SB_HEAD_DIM), F32),
        ],
        in_specs=[
            pl.BlockSpec((1, tq, pair), lambda b, p, i: (b, i, p)),
            pl.BlockSpec((1, seq, pair), lambda b, p, i: (b, 0, p)),
            pl.BlockSpec((1, seq, pair), lambda b, p, i: (b, 0, p)),
            pl.BlockSpec((2 * tk, 2 * tk), lambda b, p, i: (0, 0)),
        ],
        out_specs=pl.BlockSpec((1, tq, pair), lambda b, p, i: (b, i, p)),
        out_shape=jax.ShapeDtypeStruct((batch, seq, SB_WIDTH), BF16),
        compiler_params=pltpu.CompilerParams(
            dimension_semantics=("arbitrary", "arbitrary", "arbitrary"),
            vmem_limit_bytes=VMEM_LIMIT),
        name="sb_attention",
    )(q3, k3, v3, m2)


def _even_out_kernel(xb_ref, halo_ref, oa_ref, x_ref, pw_ref, ps_ref, wo_ref, o_ref, *, tm):
    i = pl.program_id(1)
    xb = xb_ref[0]
    halo = jnp.where(i > 0, halo_ref[0], 0.0)
    ext = jnp.concatenate([halo, xb], axis=0)
    pos = (i * tm + 1 + lax.broadcasted_iota(jnp.int32, (tm, 1), 0)).astype(F32)
    y = jnp.dot(oa_ref[0], wo_ref[0:SB_WIDTH, :], preferred_element_type=F32)
    pooled_out = []
    for g, w in enumerate(POOL_WINDOWS):
        lanes = slice(g * POOL_GROUP_DIM, (g + 1) * POOL_GROUP_DIM)
        s = ext[:, lanes]
        sh = 1
        while sh < w:
            s = s + pltpu.roll(s, sh, axis=0)
            sh *= 2
        window_sum = s[POOL_HALO:, :]
        pooled = window_sum / jnp.minimum(pos, float(w)) - xb[:, lanes]
        ob = jnp.dot(pooled.astype(BF16), pw_ref[g], preferred_element_type=F32)
        pooled_out.append((ob * ps_ref[:, lanes]).astype(BF16))
    y = y + jnp.dot(jnp.concatenate(pooled_out, axis=1), wo_ref[SB_WIDTH:, :],
                    preferred_element_type=F32)
    o_ref[0] = x_ref[0] + y


def _even_out(xb, o_a, x, pool_w, pool_scale, w_out):
    batch, seq, _ = x.shape
    tm = min(OUT_TM, seq)
    xb3 = xb.reshape(batch, seq, POOL_WIDTH)
    hb = tm // POOL_HALO
    const2 = lambda b, i: (0, 0)
    tile = lambda b, i: (b, i, 0)
    return pl.pallas_call(
        functools.partial(_even_out_kernel, tm=tm),
        grid=(batch, seq // tm),
        in_specs=[
            pl.BlockSpec((1, tm, POOL_WIDTH), tile),
            pl.BlockSpec((1, POOL_HALO, POOL_WIDTH),
                         lambda b, i: (b, jnp.maximum(i * hb - 1, 0), 0)),
            pl.BlockSpec((1, tm, SB_WIDTH), tile),
            pl.BlockSpec((1, tm, D_MODEL), tile),
            pl.BlockSpec((len(POOL_WINDOWS), POOL_GROUP_DIM, POOL_GROUP_DIM),
                         lambda b, i: (0, 0, 0)),
            pl.BlockSpec((1, POOL_WIDTH), const2),
            pl.BlockSpec((D_MODEL, D_MODEL), const2),
        ],
        out_specs=pl.BlockSpec((1, tm, D_MODEL), tile),
        out_shape=jax.ShapeDtypeStruct((batch, seq, D_MODEL), F32),
        compiler_params=pltpu.CompilerParams(
            dimension_semantics=("arbitrary", "arbitrary"), vmem_limit_bytes=VMEM_LIMIT),
        name="even_out",
    )(xb3, xb3, o_a, x, pool_w.astype(BF16), pool_scale.reshape(1, POOL_WIDTH),
      w_out.astype(BF16))


def _ffn_kernel(x_ref, g_ref, wup_ref, cw_ref, wd_ref, o_ref,
                h_ref, carry_ref, u_ref, act_ref, *, tm):
    i = pl.program_id(1)
    nch = FFN_NCHUNK

    @pl.when(i == 0)
    def _():
        carry_ref[...] = jnp.zeros_like(carry_ref)

    x = x_ref[0]
    h_ref[...] = _rms(x, g_ref[...]).astype(BF16)

    def columns(c):
        return (slice(c * FFN_CHUNK, (c + 1) * FFN_CHUNK),
                slice(FFN_HIDDEN + c * FFN_CHUNK, FFN_HIDDEN + (c + 1) * FFN_CHUNK))

    def up(c):
        h = h_ref[...]
        u = jnp.concatenate([jnp.dot(h, wup_ref[:, cols], preferred_element_type=F32)
                             for cols in columns(c)], axis=1)
        u_ref[0:8, :] = carry_ref[c]
        u_ref[8:8 + tm, :] = u
        carry_ref[c] = u[tm - 8:, :]

    def activate(c):
        cw = jnp.concatenate([cw_ref[:, cols] for cols in columns(c)], axis=1)
        conv = (cw[3:4, :] + cw[2:3, :] * u_ref[8:8 + tm, :]
                + cw[1:2, :] * u_ref[7:7 + tm, :]
                + cw[0:1, :] * u_ref[6:6 + tm, :])
        a = conv[:, :FFN_CHUNK]
        gate = conv[:, FFN_CHUNK:]
        act_ref[:, columns(c)[0]] = (a * jax.nn.sigmoid(a) * gate).astype(BF16)

    for c in range(nch):
        up(c)
        activate(c)
    o_ref[0] = x + jnp.dot(act_ref[...], wd_ref[...], preferred_element_type=F32)


def _ffn(x, gain, w_up, conv_w, conv_b, w_down):
    batch, seq, _ = x.shape
    tm = min(FFN_TM, seq)
    ck, nch = FFN_CHUNK, FFN_NCHUNK

    taps = jnp.concatenate([conv_w, conv_b[None, :],
                            jnp.zeros((4, 2 * FFN_HIDDEN), F32)], axis=0)
    tile = lambda b, i: (b, i, 0)
    const2 = lambda b, i: (0, 0)
    const3 = lambda b, i: (0, 0, 0)
    return pl.pallas_call(
        functools.partial(_ffn_kernel, tm=tm),
        grid=(batch, seq // tm),
        in_specs=[
            pl.BlockSpec((1, tm, D_MODEL), tile),
            pl.BlockSpec((1, D_MODEL), const2),
            pl.BlockSpec((D_MODEL, 2 * FFN_HIDDEN), const2),
            pl.BlockSpec((8, 2 * FFN_HIDDEN), const2),
            pl.BlockSpec((FFN_HIDDEN, D_MODEL), const2),
        ],
        out_specs=pl.BlockSpec((1, tm, D_MODEL), tile),
        out_shape=jax.ShapeDtypeStruct((batch, seq, D_MODEL), F32),
        scratch_shapes=[
            pltpu.VMEM((tm, D_MODEL), BF16),
            pltpu.VMEM((nch, 8, 2 * ck), F32),
            pltpu.VMEM((tm + 8, 2 * ck), F32),
            pltpu.VMEM((tm, FFN_HIDDEN), BF16),
        ],
        compiler_params=pltpu.CompilerParams(
            dimension_semantics=("arbitrary", "arbitrary"), vmem_limit_bytes=VMEM_LIMIT),
        name="conv_ffn",
    )(x, gain.reshape(1, D_MODEL), w_up.astype(BF16), taps, w_down.astype(BF16))


def _gla_kernel(x_ref, g_ref, win_ref, wa2_ref, ba_ref, og_ref, wout_ref, tri_ref,
                o_ref, state_ref, *, tm):
    i = pl.program_id(1)

    @pl.when(i == 0)
    def _():
        state_ref[...] = jnp.zeros_like(state_ref)

    x = x_ref[0]
    h = _rms(x, g_ref[...]).astype(BF16)
    proj = jnp.dot(h, win_ref[...], preferred_element_type=F32)
    kw, vw = GLA_KEY_WIDTH, GLA_VALUE_WIDTH
    q = proj[:, 0:kw] * (GLA_KEY_DIM ** -0.5)
    k = proj[:, kw:2 * kw]
    v = proj[:, 2 * kw:2 * kw + vw]
    r = proj[:, 2 * kw + vw:2 * kw + 2 * vw]
    a_low = proj[:, 2 * kw + 2 * vw:]
    gate = jnp.dot(a_low.astype(BF16), wa2_ref[...], preferred_element_type=F32) + ba_ref[...]
    log_alpha = (jnp.minimum(gate, 0.0) - _log1pexp_neg_abs(gate)) * (1.0 / GLA_TAU)
    hi, lo = _split_bf16(log_alpha)
    tri = tri_ref[...]
    cum = (jnp.dot(tri, hi, preferred_element_type=F32)
           + jnp.dot(tri, lo, preferred_element_type=F32))

    nchunk = tm // GLA_CHUNK
    klanes = [slice(hd * GLA_KEY_DIM, (hd + 1) * GLA_KEY_DIM) for hd in range(GLA_HEADS)]
    vlanes = [slice(hd * GLA_VALUE_DIM, (hd + 1) * GLA_VALUE_DIM) for hd in range(GLA_HEADS)]
    updates, decays = [], []
    for c in range(nchunk):
        rows = slice(c * GLA_CHUNK, (c + 1) * GLA_CHUNK)
        cum_c = cum[rows, :]
        total = cum_c[GLA_CHUNK - 1:GLA_CHUNK, :]
        k_dec = k[rows, :] * jnp.exp(total - cum_c)
        decay = jnp.broadcast_to(jnp.exp(total), (GLA_CHUNK, kw))
        kd = jnp.concatenate([k_dec, decay], axis=0)
        v_c = v[rows, :].astype(BF16)
        for hd in range(GLA_HEADS):
            kd_t = kd[:, klanes[hd]].T
            decays.append(kd_t[:, GLA_CHUNK:GLA_CHUNK + 1])
            updates.append(jnp.dot(kd_t[:, :GLA_CHUNK].astype(BF16), v_c[:, vlanes[hd]],
                                   preferred_element_type=F32))
    states = []
    for hd in range(GLA_HEADS):
        st = state_ref[hd]
        for c in range(nchunk):
            st = st * decays[c * GLA_HEADS + hd] + updates[c * GLA_HEADS + hd]
            states.append(st.astype(BF16))
        state_ref[hd] = st
    outs = []
    for c in range(nchunk):
        q_c = q[c * GLA_CHUNK:(c + 1) * GLA_CHUNK, :].astype(BF16)
        outs.append(jnp.concatenate(
            [jnp.dot(q_c[:, klanes[hd]], states[hd * nchunk + c], preferred_element_type=F32)
             for hd in range(GLA_HEADS)], axis=1))
    o = jnp.concatenate(outs, axis=0)
    normed = jnp.concatenate(
        [_rms(o[:, hd * GLA_VALUE_DIM:(hd + 1) * GLA_VALUE_DIM], og_ref[...])
         for hd in range(GLA_HEADS)], axis=1)
    gated = (normed * (r * jax.nn.sigmoid(r))).astype(BF16)
    o_ref[0] = x + jnp.dot(gated, wout_ref[...], preferred_element_type=F32)


def _gla_mixer(x, gain, w_in, w_a2, b_a, o_gain, w_out):
    batch, seq, _ = x.shape
    tm = min(GLA_TM, seq)
    in_width = 2 * GLA_KEY_WIDTH + 2 * GLA_VALUE_WIDTH + GLA_GATE_PAD
    pad = GLA_GATE_PAD - GLA_GATE_RANK
    w_in_p = jnp.pad(w_in, ((0, 0), (0, pad))).astype(BF16)
    w_a2_p = jnp.pad(w_a2, ((0, pad), (0, 0))).astype(BF16)
    t = jnp.arange(tm)
    tri = ((t[:, None] // GLA_CHUNK == t[None, :] // GLA_CHUNK)
           & (t[:, None] >= t[None, :])).astype(BF16)
    tile = lambda b, i: (b, i, 0)
    const2 = lambda b, i: (0, 0)
    return pl.pallas_call(
        functools.partial(_gla_kernel, tm=tm),
        grid=(batch, seq // tm),
        in_specs=[
            pl.BlockSpec((1, tm, D_MODEL), tile),
            pl.BlockSpec((1, D_MODEL), const2),
            pl.BlockSpec((D_MODEL, in_width), const2),
            pl.BlockSpec((GLA_GATE_PAD, GLA_KEY_WIDTH), const2),
            pl.BlockSpec((1, GLA_KEY_WIDTH), const2),
            pl.BlockSpec((1, GLA_VALUE_DIM), const2),
            pl.BlockSpec((GLA_VALUE_WIDTH, D_MODEL), const2),
            pl.BlockSpec((tm, tm), const2),
        ],
        out_specs=pl.BlockSpec((1, tm, D_MODEL), tile),
        out_shape=jax.ShapeDtypeStruct((batch, seq, D_MODEL), F32),
        scratch_shapes=[pltpu.VMEM((GLA_HEADS, GLA_KEY_DIM, GLA_VALUE_DIM), F32)],
        compiler_params=pltpu.CompilerParams(
            dimension_semantics=("arbitrary", "arbitrary"), vmem_limit_bytes=VMEM_LIMIT),
        name="gla_mixer",
    )(x, gain.reshape(1, D_MODEL), w_in_p, w_a2_p, b_a.reshape(1, GLA_KEY_WIDTH),
      o_gain.reshape(1, GLA_VALUE_DIM), w_out.astype(BF16), tri)


def kernel(x, mix_norm_even, w_in_even, sb_q_gain, sb_k_gain, pool_w, pool_scale, w_out_even,
           mix_norm_odd, w_in_odd, gla_w_a2, gla_b_a, gla_o_gain, w_out_odd,
           ffn_norm, ffn_w_up, ffn_conv_w, ffn_conv_b, ffn_w_down):
    batch, seq, _ = x.shape
    depth = ffn_norm.shape[0]
    for layer in range(depth):
        i = layer // 2
        if layer % 2 == 0:
            q, k, v, xb = _even_in(x.reshape(batch * seq, D_MODEL), mix_norm_even[i],
                                   w_in_even[i], sb_q_gain[i], sb_k_gain[i])
            o_a = _attention(q, k, v, batch, seq)
            x = _even_out(xb, o_a, x, pool_w[i], pool_scale[i], w_out_even[i])
        else:
            x = _gla_mixer(x, mix_norm_odd[i], w_in_odd[i], gla_w_a2[i], gla_b_a[i],
                           gla_o_gain[i], w_out_odd[i])
        x = _ffn(x, ffn_norm[layer], ffn_w_up[layer], ffn_conv_w[layer],
                 ffn_conv_b[layer], ffn_w_down[layer])
    return x
```

```python
import functools

import jax
import jax.numpy as jnp
from jax import lax
from jax.experimental import pallas as pl
from jax.experimental.pallas import tpu as pltpu

F32 = jnp.float32
BF16 = jnp.bfloat16

D_MODEL = 1024
NORM_EPS = 1e-6
LOG2E = 1.4426950408889634
UNDERFLOW_LOG2 = 160.0

SB_HEADS = 8
SB_HEAD_DIM = 64
SB_WIDTH = SB_HEADS * SB_HEAD_DIM
POOL_WINDOWS = (2, 4, 8, 16)
POOL_GROUP_DIM = 128
POOL_WIDTH = 512
POOL_HALO = 16

GLA_HEADS = 4
GLA_KEY_WIDTH = 512
GLA_VALUE_WIDTH = 1024
GLA_KEY_DIM = 128
GLA_VALUE_DIM = 256
GLA_GATE_RANK = 16
GLA_TAU = 16.0
GLA_CHUNK = 64
GLA_GATE_PAD = 128

FFN_HIDDEN = 2816
FFN_CHUNK = 256
FFN_NCHUNK = FFN_HIDDEN // FFN_CHUNK

IN_TM = 512
ATT_TQ = 512
ATT_TK = 128
ATT_PAIRS = 2
ATT_NB = 2
OUT_TM = 512
FFN_TM = 512
GLA_TM = 256

VMEM_LIMIT = 56 * 1024 * 1024


def _rms(x, gain):
    ms = jnp.mean(x * x, axis=-1, keepdims=True)
    return x * lax.rsqrt(ms + NORM_EPS) * gain


def _log1pexp_neg_abs(z):
    return jnp.log(1.0 + jnp.exp(jnp.minimum(z, -z)))


def _split_bf16(x):
    hi = x.astype(BF16)
    lo = (x - hi.astype(F32)).astype(BF16)
    return hi, lo


def _even_in_kernel(x_ref, g_ref, w_ref, qg_ref, kg_ref, bd_ref,
                    q_ref, k_ref, v_ref, xb_ref):
    h = _rms(x_ref[...], g_ref[...]).astype(BF16)
    proj = jnp.dot(h, w_ref[...], preferred_element_type=F32)

    def head_norm(t, gain):
        ss = jnp.dot((t * t).astype(BF16), bd_ref[...], preferred_element_type=F32)
        return t * lax.rsqrt(ss * (1.0 / SB_HEAD_DIM) + NORM_EPS) * gain

    q = head_norm(proj[:, 0:SB_WIDTH], qg_ref[...])
    q_ref[...] = (q * (SB_HEAD_DIM ** -0.5 * LOG2E)).astype(BF16)
    k_ref[...] = head_norm(proj[:, SB_WIDTH:2 * SB_WIDTH], kg_ref[...]).astype(BF16)
    v_ref[...] = proj[:, 2 * SB_WIDTH:3 * SB_WIDTH].astype(BF16)
    xb_ref[...] = proj[:, 3 * SB_WIDTH:]


def _even_in(x2d, gain, w_in, q_gain, k_gain):
    n = x2d.shape[0]
    tm = min(IN_TM, n)
    head = jnp.arange(SB_WIDTH) // SB_HEAD_DIM
    blockdiag = (head[:, None] == head[None, :]).astype(BF16)
    width = w_in.shape[1]
    const = lambda i: (0, 0)
    tile = lambda i: (i, 0)
    return pl.pallas_call(
        _even_in_kernel,
        grid=(n // tm,),
        in_specs=[
            pl.BlockSpec((tm, D_MODEL), tile),
            pl.BlockSpec((1, D_MODEL), const),
            pl.BlockSpec((D_MODEL, width), const),
            pl.BlockSpec((1, SB_WIDTH), const),
            pl.BlockSpec((1, SB_WIDTH), const),
            pl.BlockSpec((SB_WIDTH, SB_WIDTH), const),
        ],
        out_specs=[
            pl.BlockSpec((tm, SB_WIDTH), tile),
            pl.BlockSpec((tm, SB_WIDTH), tile),
            pl.BlockSpec((tm, SB_WIDTH), tile),
            pl.BlockSpec((tm, POOL_WIDTH), tile),
        ],
        out_shape=[
            jax.ShapeDtypeStruct((n, SB_WIDTH), BF16),
            jax.ShapeDtypeStruct((n, SB_WIDTH), BF16),
            jax.ShapeDtypeStruct((n, SB_WIDTH), BF16),
            jax.ShapeDtypeStruct((n, POOL_WIDTH), F32),
        ],
        compiler_params=pltpu.CompilerParams(
            dimension_semantics=("arbitrary",), vmem_limit_bytes=VMEM_LIMIT),
        name="even_in",
    )(x2d, gain.reshape(1, D_MODEL), w_in.astype(BF16),
      jnp.tile(q_gain, SB_HEADS).reshape(1, SB_WIDTH),
      jnp.tile(k_gain, SB_HEADS).reshape(1, SB_WIDTH), blockdiag)


def _attn_kernel(q_ref, k_ref, v_ref, m2_ref, o_ref, za_ref, zb_ref, wa_ref, wb_ref,
                 c0_ref, c1_ref, acc_ref, *, tq, tk, nb):
    i = pl.program_id(2)
    npairs = q_ref.shape[2] // (2 * SB_HEAD_DIM)
    ratio = tq // tk
    ngroups = i * (ratio // nb)
    lane = lax.broadcasted_iota(jnp.int32, (tk, 2 * SB_HEAD_DIM), 1)
    first_head = lane < SB_HEAD_DIM
    zero = jnp.zeros((), BF16)

    def stack_heads(blk):
        return jnp.concatenate(
            [jnp.where(first_head, blk, zero), jnp.where(first_head, zero, blk)], axis=0)

    def stacked(ref, p, start, n):
        lanes = slice(p * 2 * SB_HEAD_DIM, (p + 1) * 2 * SB_HEAD_DIM)
        return jnp.concatenate(
            [stack_heads(ref[0, pl.ds(start + u * tk, tk), lanes]) for u in range(n)], axis=0)

    def scores(p, q_rows, start, n):
        return lax.dot_general(q_rows, stacked(k_ref, p, start, n), (((1,), (1,)), ((), ())),
                               preferred_element_type=F32)

    def attend(p, w, start, n):
        return jnp.dot(w, stacked(v_ref, p, start, n), preferred_element_type=F32)

    def group_start(g):
        return pl.multiple_of(jnp.maximum(i * ratio - (g + 1) * nb, 0) * tk, tk)

    def block_weights(z2, c0, c1, visible):
        ws, cs = [], []
        for hh, c in ((0, c0), (1, c1)):
            z = z2[:, hh * tk:(hh + 1) * tk]
            softplus = jnp.maximum(z, 0.0) + jnp.log2(1.0 + jnp.exp2(jnp.minimum(z, -z)))
            if visible is not None:
                softplus = jnp.where(visible, softplus, 0.0)
            hi, lo = _split_bf16(softplus)
            cum = jnp.dot(jnp.concatenate([hi, lo], axis=1), m2_ref[...],
                          preferred_element_type=F32)
            w = jnp.exp2(z + cum[:, :tk] + c)
            if visible is not None:
                w = jnp.where(visible, w, 0.0)
            ws.append(w.astype(BF16))
            cs.append(c + cum[:, tk:])
        return jnp.concatenate(ws, axis=1), cs[0], cs[1]

    def pair_query(p):
        return q_ref[0, :, p * 2 * SB_HEAD_DIM:(p + 1) * 2 * SB_HEAD_DIM]

    def diagonal(p, q):
        c0 = jnp.zeros((tq, tk), F32)
        c1 = jnp.zeros((tq, tk), F32)
        acc = jnp.zeros((tq, 2 * SB_HEAD_DIM), F32)
        for u in reversed(range(ratio)):
            r0 = u * tk
            start = pl.multiple_of(i * tq + r0, tk)
            row = lax.broadcasted_iota(jnp.int32, (tq - r0, tk), 0)
            col = lax.broadcasted_iota(jnp.int32, (tq - r0, tk), 1)
            w, p0, p1 = block_weights(scores(p, q[r0:, :], start, 1),
                                      c0[r0:, :], c1[r0:, :], col < row)
            pacc = acc[r0:, :] + attend(p, w, start, 1)
            if r0:
                c0 = jnp.concatenate([c0[:r0, :], p0], axis=0)
                c1 = jnp.concatenate([c1[:r0, :], p1], axis=0)
                acc = jnp.concatenate([acc[:r0, :], pacc], axis=0)
            else:
                c0, c1, acc = p0, p1, pacc
        return c0, c1, acc

    def spent(c0, c1):
        return jnp.maximum(jnp.max(c0), jnp.max(c1)) <= -UNDERFLOW_LOG2

    def near_short(p):
        q = pair_query(p)
        c0, c1, acc = diagonal(p, q)
        enough = spent(c0[2 * tk:, :], c1[2 * tk:, :])
        for b in (1, 2):
            rows = (3 - b) * tk
            start = pl.multiple_of(jnp.maximum(i * ratio - b, 0) * tk, tk)
            w, p0, p1 = block_weights(scores(p, q[:rows, :], start, 1),
                                      c0[:rows, :], c1[:rows, :], None)
            w = jnp.where(i > 0, w, zero)
            acc = jnp.concatenate([acc[:rows, :] + attend(p, w, start, 1), acc[rows:, :]], axis=0)
            c0, c1 = p0, p1
            enough = jnp.logical_and(enough, spent(c0[rows - tk:, :], c1[rows - tk:, :]))
        acc_ref[p] = acc
        return jnp.logical_or(i == 0, enough)

    def near(p):
        q = pair_query(p)
        c0, c1, acc = diagonal(p, q)
        z_near = scores(p, q, group_start(0), nb)
        w_near = [None] * nb
        for u in reversed(range(nb)):
            w_near[u], c0, c1 = block_weights(z_near[:, u * 2 * tk:(u + 1) * 2 * tk],
                                              c0, c1, None)
        w_near = jnp.where(i > 0, jnp.concatenate(w_near, axis=1), zero)
        acc = acc + attend(p, w_near, group_start(0), nb)
        c0_ref[p] = c0
        c1_ref[p] = c1
        acc_ref[p] = acc

    def far(p):
        q = pair_query(p)

        def stage(g, z_in, z_out, w_in, w_out):
            acc_ref[p] += attend(p, w_in[...], group_start(g - 1), nb)
            z_out[...] = scores(p, q, group_start(g + 1), nb)
            for u in reversed(range(nb)):
                cols = slice(u * 2 * tk, (u + 1) * 2 * tk)
                w, c0, c1 = block_weights(z_in[:, cols], c0_ref[p], c1_ref[p], None)
                c0_ref[p] = c0
                c1_ref[p] = c1
                w_out[:, cols] = w

        def any_weight_left():
            return jnp.maximum(jnp.max(c0_ref[p]), jnp.max(c1_ref[p])) > -UNDERFLOW_LOG2

        farther = jnp.logical_and(ngroups > 1, any_weight_left())

        @pl.when(farther)
        def _():
            wb_ref[...] = jnp.zeros_like(wb_ref)
            za_ref[...] = scores(p, q, group_start(1), nb)

        def cond(state):
            g, alive = state
            return jnp.logical_and(g < ngroups, alive)

        def body(state):
            g, _ = state
            stage(g, za_ref, zb_ref, wb_ref, wa_ref)
            more = jnp.logical_and(g + 1 < ngroups, any_weight_left())

            @pl.when(more)
            def _():
                stage(g + 1, zb_ref, za_ref, wa_ref, wb_ref)

            return g + 1 + more.astype(jnp.int32), any_weight_left()

        last, _ = lax.while_loop(cond, body, (jnp.int32(1), farther))

        stages = last - 1

        @pl.when(stages % 2 == 1)
        def _():
            acc_ref[p] += attend(p, wa_ref[...], group_start(last - 1), nb)

        @pl.when(jnp.logical_and(stages > 0, stages % 2 == 0))
        def _():
            acc_ref[p] += attend(p, wb_ref[...], group_start(last - 1), nb)

        o_ref[0, :, p * 2 * SB_HEAD_DIM:(p + 1) * 2 * SB_HEAD_DIM] = acc_ref[p].astype(BF16)

    enough = [near_short(p) for p in range(npairs)]
    for p in range(npairs):
        @pl.when(enough[p])
        def _(p=p):
            o_ref[0, :, p * 2 * SB_HEAD_DIM:(p + 1) * 2 * SB_HEAD_DIM] = acc_ref[p].astype(BF16)

        @pl.when(jnp.logical_not(enough[p]))
        def _(p=p):
            near(p)
            far(p)


def _attention(q, k, v, batch, seq):
    tq = min(ATT_TQ, seq)
    tk = min(ATT_TK, tq)
    q3 = q.reshape(batch, seq, SB_WIDTH)
    k3 = k.reshape(batch, seq, SB_WIDTH)
    v3 = v.reshape(batch, seq, SB_WIDTH)
    nb = ATT_NB
    assert seq % tq == 0 and (tq // tk) % nb == 0 and tq // tk >= 3
    r = jnp.arange(2 * tk) % tk
    ccol = jnp.arange(2 * tk)
    m2 = -jnp.where(ccol[None, :] < tk, r[:, None] >= ccol[None, :], True).astype(BF16)
    width = ATT_PAIRS * 2 * SB_HEAD_DIM
    return pl.pallas_call(
        functools.partial(_attn_kernel, tq=tq, tk=tk, nb=nb),
        grid=(batch, SB_WIDTH // width, seq // tq),
        scratch_shapes=[
            pltpu.VMEM((tq, nb * 2 * tk), F32), pltpu.VMEM((tq, nb * 2 * tk), F32),
            pltpu.VMEM((tq, nb * 2 * tk), BF16), pltpu.VMEM((tq, nb * 2 * tk), BF16),
            pltpu.VMEM((ATT_PAIRS, tq, tk), F32), pltpu.VMEM((ATT_PAIRS, tq, tk), F32),
            pltpu.VMEM((ATT_PAIRS, tq, 2 * SB_HEAD_DIM), F32),
        ],
        in_specs=[
            pl.BlockSpec((1, tq, width), lambda b, p, i: (b, i, p)),
            pl.BlockSpec((1, seq, width), lambda b, p, i: (b, 0, p)),
            pl.BlockSpec((1, seq, width), lambda b, p, i: (b, 0, p)),
            pl.BlockSpec((2 * tk, 2 * tk), lambda b, p, i: (0, 0)),
        ],
        out_specs=pl.BlockSpec((1, tq, width), lambda b, p, i: (b, i, p)),
        out_shape=jax.ShapeDtypeStruct((batch, seq, SB_WIDTH), BF16),
        compiler_params=pltpu.CompilerParams(
            dimension_semantics=("arbitrary", "arbitrary", "arbitrary"),
            vmem_limit_bytes=VMEM_LIMIT),
        name="sb_attention",
    )(q3, k3, v3, m2)


def _even_out_kernel(xb_ref, halo_ref, oa_ref, x_ref, pw_ref, ps_ref, wo_ref, o_ref, *, tm):
    i = pl.program_id(1)
    xb = xb_ref[0]
    halo = jnp.where(i > 0, halo_ref[0], 0.0)
    ext = jnp.concatenate([halo, xb], axis=0)
    pos = (i * tm + 1 + lax.broadcasted_iota(jnp.int32, (tm, 1), 0)).astype(F32)
    y = jnp.dot(oa_ref[0], wo_ref[0:SB_WIDTH, :], preferred_element_type=F32)
    pooled_out = []
    for g, w in enumerate(POOL_WINDOWS):
        lanes = slice(g * POOL_GROUP_DIM, (g + 1) * POOL_GROUP_DIM)
        s = ext[:, lanes]
        sh = 1
        while sh < w:
            s = s + pltpu.roll(s, sh, axis=0)
            sh *= 2
        window_sum = s[POOL_HALO:, :]
        pooled = window_sum / jnp.minimum(pos, float(w)) - xb[:, lanes]
        ob = jnp.dot(pooled.astype(BF16), pw_ref[g], preferred_element_type=F32)
        pooled_out.append((ob * ps_ref[:, lanes]).astype(BF16))
    y = y + jnp.dot(jnp.concatenate(pooled_out, axis=1), wo_ref[SB_WIDTH:, :],
                    preferred_element_type=F32)
    o_ref[0] = x_ref[0] + y


def _even_out(xb, o_a, x, pool_w, pool_scale, w_out):
    batch, seq, _ = x.shape
    tm = min(OUT_TM, seq)
    xb3 = xb.reshape(batch, seq, POOL_WIDTH)
    hb = tm // POOL_HALO
    const2 = lambda b, i: (0, 0)
    tile = lambda b, i: (b, i, 0)
    return pl.pallas_call(
        functools.partial(_even_out_kernel, tm=tm),
        grid=(batch, seq // tm),
        in_specs=[
            pl.BlockSpec((1, tm, POOL_WIDTH), tile),
            pl.BlockSpec((1, POOL_HALO, POOL_WIDTH),
                         lambda b, i: (b, jnp.maximum(i * hb - 1, 0), 0)),
            pl.BlockSpec((1, tm, SB_WIDTH), tile),
            pl.BlockSpec((1, tm, D_MODEL), tile),
            pl.BlockSpec((len(POOL_WINDOWS), POOL_GROUP_DIM, POOL_GROUP_DIM),
                         lambda b, i: (0, 0, 0)),
            pl.BlockSpec((1, POOL_WIDTH), const2),
            pl.BlockSpec((D_MODEL, D_MODEL), const2),
        ],
        out_specs=pl.BlockSpec((1, tm, D_MODEL), tile),
        out_shape=jax.ShapeDtypeStruct((batch, seq, D_MODEL), F32),
        compiler_params=pltpu.CompilerParams(
            dimension_semantics=("arbitrary", "arbitrary"), vmem_limit_bytes=VMEM_LIMIT),
        name="even_out",
    )(xb3, xb3, o_a, x, pool_w.astype(BF16), pool_scale.reshape(1, POOL_WIDTH),
      w_out.astype(BF16))


def _ffn_kernel(x_ref, g_ref, wup_ref, cw_ref, wd_ref, o_ref,
                h_ref, carry_ref, u_ref, act_ref, *, tm):
    i = pl.program_id(1)
    nch = FFN_NCHUNK

    @pl.when(i == 0)
    def _():
        carry_ref[...] = jnp.zeros_like(carry_ref)

    x = x_ref[0]
    h_ref[...] = _rms(x, g_ref[...]).astype(BF16)

    def columns(c):
        return (slice(c * FFN_CHUNK, (c + 1) * FFN_CHUNK),
                slice(FFN_HIDDEN + c * FFN_CHUNK, FFN_HIDDEN + (c + 1) * FFN_CHUNK))

    def up(c):
        h = h_ref[...]
        u = jnp.concatenate([jnp.dot(h, wup_ref[:, cols], preferred_element_type=F32)
                             for cols in columns(c)], axis=1)
        u_ref[0:8, :] = carry_ref[c]
        u_ref[8:8 + tm, :] = u
        carry_ref[c] = u[tm - 8:, :]

    def activate(c):
        cw = jnp.concatenate([cw_ref[:, cols] for cols in columns(c)], axis=1)
        conv = (cw[3:4, :] + cw[2:3, :] * u_ref[8:8 + tm, :]
                + cw[1:2, :] * u_ref[7:7 + tm, :]
                + cw[0:1, :] * u_ref[6:6 + tm, :])
        a = conv[:, :FFN_CHUNK]
        gate = conv[:, FFN_CHUNK:]
        act_ref[:, columns(c)[0]] = (a * jax.nn.sigmoid(a) * gate).astype(BF16)

    for c in range(nch):
        up(c)
        activate(c)
    o_ref[0] = x + jnp.dot(act_ref[...], wd_ref[...], preferred_element_type=F32)


def _ffn(x, gain, w_up, conv_w, conv_b, w_down):
    batch, seq, _ = x.shape
    tm = min(FFN_TM, seq)
    ck, nch = FFN_CHUNK, FFN_NCHUNK

    taps = jnp.concatenate([conv_w, conv_b[None, :],
                            jnp.zeros((4, 2 * FFN_HIDDEN), F32)], axis=0)
    tile = lambda b, i: (b, i, 0)
    const2 = lambda b, i: (0, 0)
    const3 = lambda b, i: (0, 0, 0)
    return pl.pallas_call(
        functools.partial(_ffn_kernel, tm=tm),
        grid=(batch, seq // tm),
        in_specs=[
            pl.BlockSpec((1, tm, D_MODEL), tile),
            pl.BlockSpec((1, D_MODEL), const2),
            pl.BlockSpec((D_MODEL, 2 * FFN_HIDDEN), const2),
            pl.BlockSpec((8, 2 * FFN_HIDDEN), const2),
            pl.BlockSpec((FFN_HIDDEN, D_MODEL), const2),
        ],
        out_specs=pl.BlockSpec((1, tm, D_MODEL), tile),
        out_shape=jax.ShapeDtypeStruct((batch, seq, D_MODEL), F32),
        scratch_shapes=[
            pltpu.VMEM((tm, D_MODEL), BF16),
            pltpu.VMEM((nch, 8, 2 * ck), F32),
            pltpu.VMEM((tm + 8, 2 * ck), F32),
            pltpu.VMEM((tm, FFN_HIDDEN), BF16),
        ],
        compiler_params=pltpu.CompilerParams(
            dimension_semantics=("arbitrary", "arbitrary"), vmem_limit_bytes=VMEM_LIMIT),
        name="conv_ffn",
    )(x, gain.reshape(1, D_MODEL), w_up.astype(BF16), taps, w_down.astype(BF16))


def _gla_kernel(x_ref, g_ref, win_ref, wa2_ref, ba_ref, og_ref, wout_ref, tri_ref,
                o_ref, state_ref, *, tm):
    i = pl.program_id(1)

    @pl.when(i == 0)
    def _():
        state_ref[...] = jnp.zeros_like(state_ref)

    x = x_ref[0]
    h = _rms(x, g_ref[...]).astype(BF16)
    proj = jnp.dot(h, win_ref[...], preferred_element_type=F32)
    kw, vw = GLA_KEY_WIDTH, GLA_VALUE_WIDTH
    q = proj[:, 0:kw] * (GLA_KEY_DIM ** -0.5)
    k = proj[:, kw:2 * kw]
    v = proj[:, 2 * kw:2 * kw + vw]
    r = proj[:, 2 * kw + vw:2 * kw + 2 * vw]
    a_low = proj[:, 2 * kw + 2 * vw:]
    gate = jnp.dot(a_low.astype(BF16), wa2_ref[...], preferred_element_type=F32) + ba_ref[...]
    log_alpha = (jnp.minimum(gate, 0.0) - _log1pexp_neg_abs(gate)) * (1.0 / GLA_TAU)
    hi, lo = _split_bf16(log_alpha)
    tri = tri_ref[...]
    cum = (jnp.dot(tri, hi, preferred_element_type=F32)
           + jnp.dot(tri, lo, preferred_element_type=F32))

    nchunk = tm // GLA_CHUNK
    klanes = [slice(hd * GLA_KEY_DIM, (hd + 1) * GLA_KEY_DIM) for hd in range(GLA_HEADS)]
    vlanes = [slice(hd * GLA_VALUE_DIM, (hd + 1) * GLA_VALUE_DIM) for hd in range(GLA_HEADS)]
    updates, decays = [], []
    for c in range(nchunk):
        rows = slice(c * GLA_CHUNK, (c + 1) * GLA_CHUNK)
        cum_c = cum[rows, :]
        total = cum_c[GLA_CHUNK - 1:GLA_CHUNK, :]
        k_dec = k[rows, :] * jnp.exp(total - cum_c)
        decay = jnp.broadcast_to(jnp.exp(total), (GLA_CHUNK, kw))
        kd = jnp.concatenate([k_dec, decay], axis=0)
        v_c = v[rows, :].astype(BF16)
        for hd in range(GLA_HEADS):
            kd_t = kd[:, klanes[hd]].T
            decays.append(kd_t[:, GLA_CHUNK:GLA_CHUNK + 1])
            updates.append(jnp.dot(kd_t[:, :GLA_CHUNK].astype(BF16), v_c[:, vlanes[hd]],
                                   preferred_element_type=F32))
    states = []
    for hd in range(GLA_HEADS):
        st = state_ref[hd]
        for c in range(nchunk):
            st = st * decays[c * GLA_HEADS + hd] + updates[c * GLA_HEADS + hd]
            states.append(st.astype(BF16))
        state_ref[hd] = st
    outs = []
    for c in range(nchunk):
        q_c = q[c * GLA_CHUNK:(c + 1) * GLA_CHUNK, :].astype(BF16)
        outs.append(jnp.concatenate(
            [jnp.dot(q_c[:, klanes[hd]], states[hd * nchunk + c], preferred_element_type=F32)
             for hd in range(GLA_HEADS)], axis=1))
    o = jnp.concatenate(outs, axis=0)
    normed = jnp.concatenate(
        [_rms(o[:, hd * GLA_VALUE_DIM:(hd + 1) * GLA_VALUE_DIM], og_ref[...])
         for hd in range(GLA_HEADS)], axis=1)
    gated = (normed * (r * jax.nn.sigmoid(r))).astype(BF16)
    o_ref[0] = x + jnp.dot(gated, wout_ref[...], preferred_element_type=F32)


def _gla_mixer(x, gain, w_in, w_a2, b_a, o_gain, w_out):
    batch, seq, _ = x.shape
    tm = min(GLA_TM, seq)
    in_width = 2 * GLA_KEY_WIDTH + 2 * GLA_VALUE_WIDTH + GLA_GATE_PAD
    pad = GLA_GATE_PAD - GLA_GATE_RANK
    w_in_p = jnp.pad(w_in, ((0, 0), (0, pad))).astype(BF16)
    w_a2_p = jnp.pad(w_a2, ((0, pad), (0, 0))).astype(BF16)
    t = jnp.arange(tm)
    tri = ((t[:, None] // GLA_CHUNK == t[None, :] // GLA_CHUNK)
           & (t[:, None] >= t[None, :])).astype(BF16)
    tile = lambda b, i: (b, i, 0)
    const2 = lambda b, i: (0, 0)
    return pl.pallas_call(
        functools.partial(_gla_kernel, tm=tm),
        grid=(batch, seq // tm),
        in_specs=[
            pl.BlockSpec((1, tm, D_MODEL), tile),
            pl.BlockSpec((1, D_MODEL), const2),
            pl.BlockSpec((D_MODEL, in_width), const2),
            pl.BlockSpec((GLA_GATE_PAD, GLA_KEY_WIDTH), const2),
            pl.BlockSpec((1, GLA_KEY_WIDTH), const2),
            pl.BlockSpec((1, GLA_VALUE_DIM), const2),
            pl.BlockSpec((GLA_VALUE_WIDTH, D_MODEL), const2),
            pl.BlockSpec((tm, tm), const2),
        ],
        out_specs=pl.BlockSpec((1, tm, D_MODEL), tile),
        out_shape=jax.ShapeDtypeStruct((batch, seq, D_MODEL), F32),
        scratch_shapes=[pltpu.VMEM((GLA_HEADS, GLA_KEY_DIM, GLA_VALUE_DIM), F32)],
        compiler_params=pltpu.CompilerParams(
            dimension_semantics=("arbitrary", "arbitrary"), vmem_limit_bytes=VMEM_LIMIT),
        name="gla_mixer",
    )(x, gain.reshape(1, D_MODEL), w_in_p, w_a2_p, b_a.reshape(1, GLA_KEY_WIDTH),
      o_gain.reshape(1, GLA_VALUE_DIM), w_out.astype(BF16), tri)


def kernel(x, mix_norm_even, w_in_even, sb_q_gain, sb_k_gain, pool_w, pool_scale, w_out_even,
           mix_norm_odd, w_in_odd, gla_w_a2, gla_b_a, gla_o_gain, w_out_odd,
           ffn_norm, ffn_w_up, ffn_conv_w, ffn_conv_b, ffn_w_down):
    batch, seq, _ = x.shape
    depth = ffn_norm.shape[0]
    for layer in range(depth):
        i = layer // 2
        if layer % 2 == 0:
            q, k, v, xb = _even_in(x.reshape(batch * seq, D_MODEL), mix_norm_even[i],
                                   w_in_even[i], sb_q_gain[i], sb_k_gain[i])
            o_a = _attention(q, k, v, batch, seq)
            x = _even_out(xb, o_a, x, pool_w[i], pool_scale[i], w_out_even[i])
        else:
            x = _gla_mixer(x, mix_norm_odd[i], w_in_odd[i], gla_w_a2[i], gla_b_a[i],
                           gla_o_gain[i], w_out_odd[i])
        x = _ffn(x, ffn_norm[layer], ffn_w_up[layer], ffn_conv_w[layer],
                 ffn_conv_b[layer], ffn_w_down[layer])
    return x
```

```python
import functools

import jax
import jax.numpy as jnp
from jax import lax
from jax.experimental import pallas as pl
from jax.experimental.pallas import tpu as pltpu

F32 = jnp.float32
BF16 = jnp.bfloat16

D_MODEL = 1024
NORM_EPS = 1e-6
LOG2E = 1.4426950408889634
UNDERFLOW_LOG2 = 160.0

SB_HEADS = 8
SB_HEAD_DIM = 64
SB_WIDTH = SB_HEADS * SB_HEAD_DIM
POOL_WINDOWS = (2, 4, 8, 16)
POOL_GROUP_DIM = 128
POOL_WIDTH = 512
POOL_HALO = 16

GLA_HEADS = 4
GLA_KEY_WIDTH = 512
GLA_VALUE_WIDTH = 1024
GLA_KEY_DIM = 128
GLA_VALUE_DIM = 256
GLA_GATE_RANK = 16
GLA_TAU = 16.0
GLA_CHUNK = 64
GLA_GATE_PAD = 128

FFN_HIDDEN = 2816
FFN_CHUNK = 256
FFN_NCHUNK = FFN_HIDDEN // FFN_CHUNK

IN_TM = 512
ATT_TQ = 512
ATT_TK = 128
ATT_PAIRS = 2
ATT_NB = 2
OUT_TM = 512
FFN_TM = 512
GLA_TM = 256

VMEM_LIMIT = 56 * 1024 * 1024


def _rms(x, gain):
    ms = jnp.mean(x * x, axis=-1, keepdims=True)
    return x * lax.rsqrt(ms + NORM_EPS) * gain


def _log1pexp_neg_abs(z):
    return jnp.log(1.0 + jnp.exp(jnp.minimum(z, -z)))


def _split_bf16(x):
    hi = x.astype(BF16)
    lo = (x - hi.astype(F32)).astype(BF16)
    return hi, lo


def _even_in_kernel(x_ref, g_ref, w_ref, qg_ref, kg_ref, bd_ref,
                    q_ref, k_ref, v_ref, xb_ref):
    h = _rms(x_ref[...], g_ref[...]).astype(BF16)
    proj = jnp.dot(h, w_ref[...], preferred_element_type=F32)

    def head_norm(t, gain):
        ss = jnp.dot((t * t).astype(BF16), bd_ref[...], preferred_element_type=F32)
        return t * lax.rsqrt(ss * (1.0 / SB_HEAD_DIM) + NORM_EPS) * gain

    q = head_norm(proj[:, 0:SB_WIDTH], qg_ref[...])
    q_ref[...] = (q * (SB_HEAD_DIM ** -0.5 * LOG2E)).astype(BF16)
    k_ref[...] = head_norm(proj[:, SB_WIDTH:2 * SB_WIDTH], kg_ref[...]).astype(BF16)
    v_ref[...] = proj[:, 2 * SB_WIDTH:3 * SB_WIDTH].astype(BF16)
    xb_ref[...] = proj[:, 3 * SB_WIDTH:]


def _even_in(x2d, gain, w_in, q_gain, k_gain):
    n = x2d.shape[0]
    tm = min(IN_TM, n)
    head = jnp.arange(SB_WIDTH) // SB_HEAD_DIM
    blockdiag = (head[:, None] == head[None, :]).astype(BF16)
    width = w_in.shape[1]
    const = lambda i: (0, 0)
    tile = lambda i: (i, 0)
    return pl.pallas_call(
        _even_in_kernel,
        grid=(n // tm,),
        in_specs=[
            pl.BlockSpec((tm, D_MODEL), tile),
            pl.BlockSpec((1, D_MODEL), const),
            pl.BlockSpec((D_MODEL, width), const),
            pl.BlockSpec((1, SB_WIDTH), const),
            pl.BlockSpec((1, SB_WIDTH), const),
            pl.BlockSpec((SB_WIDTH, SB_WIDTH), const),
        ],
        out_specs=[
            pl.BlockSpec((tm, SB_WIDTH), tile),
            pl.BlockSpec((tm, SB_WIDTH), tile),
            pl.BlockSpec((tm, SB_WIDTH), tile),
            pl.BlockSpec((tm, POOL_WIDTH), tile),
        ],
        out_shape=[
            jax.ShapeDtypeStruct((n, SB_WIDTH), BF16),
            jax.ShapeDtypeStruct((n, SB_WIDTH), BF16),
            jax.ShapeDtypeStruct((n, SB_WIDTH), BF16),
            jax.ShapeDtypeStruct((n, POOL_WIDTH), F32),
        ],
        compiler_params=pltpu.CompilerParams(
            dimension_semantics=("arbitrary",), vmem_limit_bytes=VMEM_LIMIT),
        name="even_in",
    )(x2d, gain.reshape(1, D_MODEL), w_in.astype(BF16),
      jnp.tile(q_gain, SB_HEADS).reshape(1, SB_WIDTH),
      jnp.tile(k_gain, SB_HEADS).reshape(1, SB_WIDTH), blockdiag)


def _attn_kernel(q_ref, k_ref, v_ref, m2_ref, o_ref, za_ref, zb_ref, wa_ref, wb_ref,
                 c0_ref, c1_ref, acc_ref, *, tq, tk, nb):
    i = pl.program_id(2)
    npairs = q_ref.shape[2] // (2 * SB_HEAD_DIM)
    ratio = tq // tk
    ngroups = i * (ratio // nb)
    lane = lax.broadcasted_iota(jnp.int32, (tk, 2 * SB_HEAD_DIM), 1)
    first_head = lane < SB_HEAD_DIM
    zero = jnp.zeros((), BF16)

    def stack_heads(blk):
        return jnp.concatenate(
            [jnp.where(first_head, blk, zero), jnp.where(first_head, zero, blk)], axis=0)

    def stacked(ref, p, start, n):
        lanes = slice(p * 2 * SB_HEAD_DIM, (p + 1) * 2 * SB_HEAD_DIM)
        return jnp.concatenate(
            [stack_heads(ref[0, pl.ds(start + u * tk, tk), lanes]) for u in range(n)], axis=0)

    def scores(p, q_rows, start, n):
        return lax.dot_general(q_rows, stacked(k_ref, p, start, n), (((1,), (1,)), ((), ())),
                               preferred_element_type=F32)

    def attend(p, w, start, n):
        return jnp.dot(w, stacked(v_ref, p, start, n), preferred_element_type=F32)

    def group_start(g):
        return pl.multiple_of(jnp.maximum(i * ratio - (g + 1) * nb, 0) * tk, tk)

    def block_weights(z2, c0, c1, visible):
        ws, cs = [], []
        for hh, c in ((0, c0), (1, c1)):
            z = z2[:, hh * tk:(hh + 1) * tk]
            softplus = jnp.maximum(z, 0.0) + jnp.log2(1.0 + jnp.exp2(jnp.minimum(z, -z)))
            if visible is not None:
                softplus = jnp.where(visible, softplus, 0.0)
            hi, lo = _split_bf16(softplus)
            cum = jnp.dot(jnp.concatenate([hi, lo], axis=1), m2_ref[...],
                          preferred_element_type=F32)
            w = jnp.exp2(z + cum[:, :tk] + c)
            if visible is not None:
                w = jnp.where(visible, w, 0.0)
            ws.append(w.astype(BF16))
            cs.append(c + cum[:, tk:])
        return jnp.concatenate(ws, axis=1), cs[0], cs[1]

    def pair_query(p):
        return q_ref[0, :, p * 2 * SB_HEAD_DIM:(p + 1) * 2 * SB_HEAD_DIM]

    def diagonal(p, q):
        c0 = jnp.zeros((tq, tk), F32)
        c1 = jnp.zeros((tq, tk), F32)
        acc = jnp.zeros((tq, 2 * SB_HEAD_DIM), F32)
        for u in reversed(range(ratio)):
            r0 = u * tk
            start = pl.multiple_of(i * tq + r0, tk)
            row = lax.broadcasted_iota(jnp.int32, (tq - r0, tk), 0)
            col = lax.broadcasted_iota(jnp.int32, (tq - r0, tk), 1)
            w, p0, p1 = block_weights(scores(p, q[r0:, :], start, 1),
                                      c0[r0:, :], c1[r0:, :], col < row)
            pacc = acc[r0:, :] + attend(p, w, start, 1)
            if r0:
                c0 = jnp.concatenate([c0[:r0, :], p0], axis=0)
                c1 = jnp.concatenate([c1[:r0, :], p1], axis=0)
                acc = jnp.concatenate([acc[:r0, :], pacc], axis=0)
            else:
                c0, c1, acc = p0, p1, pacc
        return c0, c1, acc

    def spent(c0, c1):
        return jnp.maximum(jnp.max(c0), jnp.max(c1)) <= -UNDERFLOW_LOG2

    def stacked_all(ref, start):
        blk = ref[0, pl.ds(start, tk), :]
        head = lax.broadcasted_iota(jnp.int32, blk.shape, 1) // SB_HEAD_DIM
        return jnp.concatenate(
            [jnp.where(head == h, blk, zero) for h in range(2 * npairs)], axis=0)

    def near_short():
        q = q_ref[0]
        nheads = 2 * npairs
        c = [jnp.zeros((tq, tk), F32)] * nheads
        acc = jnp.zeros((tq, q.shape[1]), F32)

        def visit(r0, r1, start, visible, off_diagonal):
            z = lax.dot_general(q[r0:r1, :], stacked_all(k_ref, start),
                                (((1,), (1,)), ((), ())), preferred_element_type=F32)
            ws, cs = [], []
            for p in range(npairs):
                w, p0, p1 = block_weights(z[:, p * 2 * tk:(p + 1) * 2 * tk],
                                          c[2 * p][r0:r1, :], c[2 * p + 1][r0:r1, :], visible)
                ws.append(w)
                cs += [p0, p1]
            w = jnp.concatenate(ws, axis=1)
            if off_diagonal:
                w = jnp.where(i > 0, w, zero)
            return cs, acc[r0:r1, :] + jnp.dot(w, stacked_all(v_ref, start),
                                               preferred_element_type=F32)

        for u in reversed(range(ratio)):
            r0 = u * tk
            row = lax.broadcasted_iota(jnp.int32, (tq - r0, tk), 0)
            col = lax.broadcasted_iota(jnp.int32, (tq - r0, tk), 1)
            cs, pacc = visit(r0, tq, pl.multiple_of(i * tq + r0, tk), col < row, False)
            if r0:
                c = [jnp.concatenate([old[:r0, :], new], axis=0) for old, new in zip(c, cs)]
                acc = jnp.concatenate([acc[:r0, :], pacc], axis=0)
            else:
                c, acc = cs, pacc
        enough = [spent(c[2 * p][2 * tk:, :], c[2 * p + 1][2 * tk:, :]) for p in range(npairs)]
        for b in (1, 2):
            rows = (3 - b) * tk
            c, pacc = visit(0, rows, pl.multiple_of(jnp.maximum(i * ratio - b, 0) * tk, tk),
                            None, True)
            acc = jnp.concatenate([pacc, acc[rows:, :]], axis=0)
            enough = [jnp.logical_and(e, spent(c[2 * p][rows - tk:, :], c[2 * p + 1][rows - tk:, :]))
                      for p, e in enumerate(enough)]
        for p in range(npairs):
            acc_ref[p] = acc[:, p * 2 * SB_HEAD_DIM:(p + 1) * 2 * SB_HEAD_DIM]
        return [jnp.logical_or(i == 0, e) for e in enough]

    def near(p):
        q = pair_query(p)
        c0, c1, acc = diagonal(p, q)
        z_near = scores(p, q, group_start(0), nb)
        w_near = [None] * nb
        for u in reversed(range(nb)):
            w_near[u], c0, c1 = block_weights(z_near[:, u * 2 * tk:(u + 1) * 2 * tk],
                                              c0, c1, None)
        w_near = jnp.where(i > 0, jnp.concatenate(w_near, axis=1), zero)
        acc = acc + attend(p, w_near, group_start(0), nb)
        c0_ref[p] = c0
        c1_ref[p] = c1
        acc_ref[p] = acc

    def far(p):
        q = pair_query(p)

        def stage(g, z_in, z_out, w_in, w_out):
            acc_ref[p] += attend(p, w_in[...], group_start(g - 1), nb)
            z_out[...] = scores(p, q, group_start(g + 1), nb)
            for u in reversed(range(nb)):
                cols = slice(u * 2 * tk, (u + 1) * 2 * tk)
                w, c0, c1 = block_weights(z_in[:, cols], c0_ref[p], c1_ref[p], None)
                c0_ref[p] = c0
                c1_ref[p] = c1
                w_out[:, cols] = w

        def any_weight_left():
            return jnp.maximum(jnp.max(c0_ref[p]), jnp.max(c1_ref[p])) > -UNDERFLOW_LOG2

        farther = jnp.logical_and(ngroups > 1, any_weight_left())

        @pl.when(farther)
        def _():
            wb_ref[...] = jnp.zeros_like(wb_ref)
            za_ref[...] = scores(p, q, group_start(1), nb)

        def cond(state):
            g, alive = state
            return jnp.logical_and(g < ngroups, alive)

        def body(state):
            g, _ = state
            stage(g, za_ref, zb_ref, wb_ref, wa_ref)
            more = jnp.logical_and(g + 1 < ngroups, any_weight_left())

            @pl.when(more)
            def _():
                stage(g + 1, zb_ref, za_ref, wa_ref, wb_ref)

            return g + 1 + more.astype(jnp.int32), any_weight_left()

        last, _ = lax.while_loop(cond, body, (jnp.int32(1), farther))

        stages = last - 1

        @pl.when(stages % 2 == 1)
        def _():
            acc_ref[p] += attend(p, wa_ref[...], group_start(last - 1), nb)

        @pl.when(jnp.logical_and(stages > 0, stages % 2 == 0))
        def _():
            acc_ref[p] += attend(p, wb_ref[...], group_start(last - 1), nb)

        o_ref[0, :, p * 2 * SB_HEAD_DIM:(p + 1) * 2 * SB_HEAD_DIM] = acc_ref[p].astype(BF16)

    enough = near_short()
    for p in range(npairs):
        @pl.when(enough[p])
        def _(p=p):
            o_ref[0, :, p * 2 * SB_HEAD_DIM:(p + 1) * 2 * SB_HEAD_DIM] = acc_ref[p].astype(BF16)

        @pl.when(jnp.logical_not(enough[p]))
        def _(p=p):
            near(p)
            far(p)


def _attention(q, k, v, batch, seq):
    tq = min(ATT_TQ, seq)
    tk = min(ATT_TK, tq)
    q3 = q.reshape(batch, seq, SB_WIDTH)
    k3 = k.reshape(batch, seq, SB_WIDTH)
    v3 = v.reshape(batch, seq, SB_WIDTH)
    nb = ATT_NB
    assert seq % tq == 0 and (tq // tk) % nb == 0 and tq // tk >= 3
    r = jnp.arange(2 * tk) % tk
    ccol = jnp.arange(2 * tk)
    m2 = -jnp.where(ccol[None, :] < tk, r[:, None] >= ccol[None, :], True).astype(BF16)
    width = ATT_PAIRS * 2 * SB_HEAD_DIM
    return pl.pallas_call(
        functools.partial(_attn_kernel, tq=tq, tk=tk, nb=nb),
        grid=(batch, SB_WIDTH // width, seq // tq),
        scratch_shapes=[
            pltpu.VMEM((tq, nb * 2 * tk), F32), pltpu.VMEM((tq, nb * 2 * tk), F32),
            pltpu.VMEM((tq, nb * 2 * tk), BF16), pltpu.VMEM((tq, nb * 2 * tk), BF16),
            pltpu.VMEM((ATT_PAIRS, tq, tk), F32), pltpu.VMEM((ATT_PAIRS, tq, tk), F32),
            pltpu.VMEM((ATT_PAIRS, tq, 2 * SB_HEAD_DIM), F32),
        ],
        in_specs=[
            pl.BlockSpec((1, tq, width), lambda b, p, i: (b, i, p)),
            pl.BlockSpec((1, seq, width), lambda b, p, i: (b, 0, p)),
            pl.BlockSpec((1, seq, width), lambda b, p, i: (b, 0, p)),
            pl.BlockSpec((2 * tk, 2 * tk), lambda b, p, i: (0, 0)),
        ],
        out_specs=pl.BlockSpec((1, tq, width), lambda b, p, i: (b, i, p)),
        out_shape=jax.ShapeDtypeStruct((batch, seq, SB_WIDTH), BF16),
        compiler_params=pltpu.CompilerParams(
            dimension_semantics=("arbitrary", "arbitrary", "arbitrary"),
            vmem_limit_bytes=VMEM_LIMIT),
        name="sb_attention",
    )(q3, k3, v3, m2)


def _even_out_kernel(xb_ref, halo_ref, oa_ref, x_ref, pw_ref, ps_ref, wo_ref, o_ref, *, tm):
    i = pl.program_id(1)
    xb = xb_ref[0]
    halo = jnp.where(i > 0, halo_ref[0], 0.0)
    ext = jnp.concatenate([halo, xb], axis=0)
    pos = (i * tm + 1 + lax.broadcasted_iota(jnp.int32, (tm, 1), 0)).astype(F32)
    y = jnp.dot(oa_ref[0], wo_ref[0:SB_WIDTH, :], preferred_element_type=F32)
    pooled_out = []
    for g, w in enumerate(POOL_WINDOWS):
        lanes = slice(g * POOL_GROUP_DIM, (g + 1) * POOL_GROUP_DIM)
        s = ext[:, lanes]
        sh = 1
        while sh < w:
            s = s + pltpu.roll(s, sh, axis=0)
            sh *= 2
        window_sum = s[POOL_HALO:, :]
        pooled = window_sum / jnp.minimum(pos, float(w)) - xb[:, lanes]
        ob = jnp.dot(pooled.astype(BF16), pw_ref[g], preferred_element_type=F32)
        pooled_out.append((ob * ps_ref[:, lanes]).astype(BF16))
    y = y + jnp.dot(jnp.concatenate(pooled_out, axis=1), wo_ref[SB_WIDTH:, :],
                    preferred_element_type=F32)
    o_ref[0] = x_ref[0] + y


def _even_out(xb, o_a, x, pool_w, pool_scale, w_out):
    batch, seq, _ = x.shape
    tm = min(OUT_TM, seq)
    xb3 = xb.reshape(batch, seq, POOL_WIDTH)
    hb = tm // POOL_HALO
    const2 = lambda b, i: (0, 0)
    tile = lambda b, i: (b, i, 0)
    return pl.pallas_call(
        functools.partial(_even_out_kernel, tm=tm),
        grid=(batch, seq // tm),
        in_specs=[
            pl.BlockSpec((1, tm, POOL_WIDTH), tile),
            pl.BlockSpec((1, POOL_HALO, POOL_WIDTH),
                         lambda b, i: (b, jnp.maximum(i * hb - 1, 0), 0)),
            pl.BlockSpec((1, tm, SB_WIDTH), tile),
            pl.BlockSpec((1, tm, D_MODEL), tile),
            pl.BlockSpec((len(POOL_WINDOWS), POOL_GROUP_DIM, POOL_GROUP_DIM),
                         lambda b, i: (0, 0, 0)),
            pl.BlockSpec((1, POOL_WIDTH), const2),
            pl.BlockSpec((D_MODEL, D_MODEL), const2),
        ],
        out_specs=pl.BlockSpec((1, tm, D_MODEL), tile),
        out_shape=jax.ShapeDtypeStruct((batch, seq, D_MODEL), F32),
        compiler_params=pltpu.CompilerParams(
            dimension_semantics=("arbitrary", "arbitrary"), vmem_limit_bytes=VMEM_LIMIT),
        name="even_out",
    )(xb3, xb3, o_a, x, pool_w.astype(BF16), pool_scale.reshape(1, POOL_WIDTH),
      w_out.astype(BF16))


def _ffn_kernel(x_ref, g_ref, wup_ref, cw_ref, wd_ref, o_ref,
                h_ref, carry_ref, u_ref, act_ref, *, tm):
    i = pl.program_id(1)
    nch = FFN_NCHUNK

    @pl.when(i == 0)
    def _():
        carry_ref[...] = jnp.zeros_like(carry_ref)

    x = x_ref[0]
    h_ref[...] = _rms(x, g_ref[...]).astype(BF16)

    def columns(c):
        return (slice(c * FFN_CHUNK, (c + 1) * FFN_CHUNK),
                slice(FFN_HIDDEN + c * FFN_CHUNK, FFN_HIDDEN + (c + 1) * FFN_CHUNK))

    def up(c):
        h = h_ref[...]
        u = jnp.concatenate([jnp.dot(h, wup_ref[:, cols], preferred_element_type=F32)
                             for cols in columns(c)], axis=1)
        u_ref[0:8, :] = carry_ref[c]
        u_ref[8:8 + tm, :] = u
        carry_ref[c] = u[tm - 8:, :]

    def activate(c):
        cw = jnp.concatenate([cw_ref[:, cols] for cols in columns(c)], axis=1)
        conv = (cw[3:4, :] + cw[2:3, :] * u_ref[8:8 + tm, :]
                + cw[1:2, :] * u_ref[7:7 + tm, :]
                + cw[0:1, :] * u_ref[6:6 + tm, :])
        a = conv[:, :FFN_CHUNK]
        gate = conv[:, FFN_CHUNK:]
        act_ref[:, columns(c)[0]] = (a * jax.nn.sigmoid(a) * gate).astype(BF16)

    for c in range(nch):
        up(c)
        activate(c)
    o_ref[0] = x + jnp.dot(act_ref[...], wd_ref[...], preferred_element_type=F32)


def _ffn(x, gain, w_up, conv_w, conv_b, w_down):
    batch, seq, _ = x.shape
    tm = min(FFN_TM, seq)
    ck, nch = FFN_CHUNK, FFN_NCHUNK

    taps = jnp.concatenate([conv_w, conv_b[None, :],
                            jnp.zeros((4, 2 * FFN_HIDDEN), F32)], axis=0)
    tile = lambda b, i: (b, i, 0)
    const2 = lambda b, i: (0, 0)
    const3 = lambda b, i: (0, 0, 0)
    return pl.pallas_call(
        functools.partial(_ffn_kernel, tm=tm),
        grid=(batch, seq // tm),
        in_specs=[
            pl.BlockSpec((1, tm, D_MODEL), tile),
            pl.BlockSpec((1, D_MODEL), const2),
            pl.BlockSpec((D_MODEL, 2 * FFN_HIDDEN), const2),
            pl.BlockSpec((8, 2 * FFN_HIDDEN), const2),
            pl.BlockSpec((FFN_HIDDEN, D_MODEL), const2),
        ],
        out_specs=pl.BlockSpec((1, tm, D_MODEL), tile),
        out_shape=jax.ShapeDtypeStruct((batch, seq, D_MODEL), F32),
        scratch_shapes=[
            pltpu.VMEM((tm, D_MODEL), BF16),
            pltpu.VMEM((nch, 8, 2 * ck), F32),
            pltpu.VMEM((tm + 8, 2 * ck), F32),
            pltpu.VMEM((tm, FFN_HIDDEN), BF16),
        ],
        compiler_params=pltpu.CompilerParams(
            dimension_semantics=("arbitrary", "arbitrary"), vmem_limit_bytes=VMEM_LIMIT),
        name="conv_ffn",
    )(x, gain.reshape(1, D_MODEL), w_up.astype(BF16), taps, w_down.astype(BF16))


def _gla_kernel(x_ref, g_ref, win_ref, wa2_ref, ba_ref, og_ref, wout_ref, tri_ref,
                o_ref, state_ref, *, tm):
    i = pl.program_id(1)

    @pl.when(i == 0)
    def _():
        state_ref[...] = jnp.zeros_like(state_ref)

    x = x_ref[0]
    h = _rms(x, g_ref[...]).astype(BF16)
    proj = jnp.dot(h, win_ref[...], preferred_element_type=F32)
    kw, vw = GLA_KEY_WIDTH, GLA_VALUE_WIDTH
    q = proj[:, 0:kw] * (GLA_KEY_DIM ** -0.5)
    k = proj[:, kw:2 * kw]
    v = proj[:, 2 * kw:2 * kw + vw]
    r = proj[:, 2 * kw + vw:2 * kw + 2 * vw]
    a_low = proj[:, 2 * kw + 2 * vw:]
    gate = jnp.dot(a_low.astype(BF16), wa2_ref[...], preferred_element_type=F32) + ba_ref[...]
    log_alpha = (jnp.minimum(gate, 0.0) - _log1pexp_neg_abs(gate)) * (1.0 / GLA_TAU)
    hi, lo = _split_bf16(log_alpha)
    tri = tri_ref[...]
    cum = (jnp.dot(tri, hi, preferred_element_type=F32)
           + jnp.dot(tri, lo, preferred_element_type=F32))

    nchunk = tm // GLA_CHUNK
    klanes = [slice(hd * GLA_KEY_DIM, (hd + 1) * GLA_KEY_DIM) for hd in range(GLA_HEADS)]
    vlanes = [slice(hd * GLA_VALUE_DIM, (hd + 1) * GLA_VALUE_DIM) for hd in range(GLA_HEADS)]
    updates, decays = [], []
    for c in range(nchunk):
        rows = slice(c * GLA_CHUNK, (c + 1) * GLA_CHUNK)
        cum_c = cum[rows, :]
        total = cum_c[GLA_CHUNK - 1:GLA_CHUNK, :]
        k_dec = k[rows, :] * jnp.exp(total - cum_c)
        decay = jnp.broadcast_to(jnp.exp(total), (GLA_CHUNK, kw))
        kd = jnp.concatenate([k_dec, decay], axis=0)
        v_c = v[rows, :].astype(BF16)
        for hd in range(GLA_HEADS):
            kd_t = kd[:, klanes[hd]].T
            decays.append(kd_t[:, GLA_CHUNK:GLA_CHUNK + 1])
            updates.append(jnp.dot(kd_t[:, :GLA_CHUNK].astype(BF16), v_c[:, vlanes[hd]],
                                   preferred_element_type=F32))
    states = []
    for hd in range(GLA_HEADS):
        st = state_ref[hd]
        for c in range(nchunk):
            st = st * decays[c * GLA_HEADS + hd] + updates[c * GLA_HEADS + hd]
            states.append(st.astype(BF16))
        state_ref[hd] = st
    outs = []
    for c in range(nchunk):
        q_c = q[c * GLA_CHUNK:(c + 1) * GLA_CHUNK, :].astype(BF16)
        outs.append(jnp.concatenate(
            [jnp.dot(q_c[:, klanes[hd]], states[hd * nchunk + c], preferred_element_type=F32)
             for hd in range(GLA_HEADS)], axis=1))
    o = jnp.concatenate(outs, axis=0)
    normed = jnp.concatenate(
        [_rms(o[:, hd * GLA_VALUE_DIM:(hd + 1) * GLA_VALUE_DIM], og_ref[...])
         for hd in range(GLA_HEADS)], axis=1)
    gated = (normed * (r * jax.nn.sigmoid(r))).astype(BF16)
    o_ref[0] = x + jnp.dot(gated, wout_ref[...], preferred_element_type=F32)


def _gla_mixer(x, gain, w_in, w_a2, b_a, o_gain, w_out):
    batch, seq, _ = x.shape
    tm = min(GLA_TM, seq)
    in_width = 2 * GLA_KEY_WIDTH + 2 * GLA_VALUE_WIDTH + GLA_GATE_PAD
    pad = GLA_GATE_PAD - GLA_GATE_RANK
    w_in_p = jnp.pad(w_in, ((0, 0), (0, pad))).astype(BF16)
    w_a2_p = jnp.pad(w_a2, ((0, pad), (0, 0))).astype(BF16)
    t = jnp.arange(tm)
    tri = ((t[:, None] // GLA_CHUNK == t[None, :] // GLA_CHUNK)
           & (t[:, None] >= t[None, :])).astype(BF16)
    tile = lambda b, i: (b, i, 0)
    const2 = lambda b, i: (0, 0)
    return pl.pallas_call(
        functools.partial(_gla_kernel, tm=tm),
        grid=(batch, seq // tm),
        in_specs=[
            pl.BlockSpec((1, tm, D_MODEL), tile),
            pl.BlockSpec((1, D_MODEL), const2),
            pl.BlockSpec((D_MODEL, in_width), const2),
            pl.BlockSpec((GLA_GATE_PAD, GLA_KEY_WIDTH), const2),
            pl.BlockSpec((1, GLA_KEY_WIDTH), const2),
            pl.BlockSpec((1, GLA_VALUE_DIM), const2),
            pl.BlockSpec((GLA_VALUE_WIDTH, D_MODEL), const2),
            pl.BlockSpec((tm, tm), const2),
        ],
        out_specs=pl.BlockSpec((1, tm, D_MODEL), tile),
        out_shape=jax.ShapeDtypeStruct((batch, seq, D_MODEL), F32),
        scratch_shapes=[pltpu.VMEM((GLA_HEADS, GLA_KEY_DIM, GLA_VALUE_DIM), F32)],
        compiler_params=pltpu.CompilerParams(
            dimension_semantics=("arbitrary", "arbitrary"), vmem_limit_bytes=VMEM_LIMIT),
        name="gla_mixer",
    )(x, gain.reshape(1, D_MODEL), w_in_p, w_a2_p, b_a.reshape(1, GLA_KEY_WIDTH),
      o_gain.reshape(1, GLA_VALUE_DIM), w_out.astype(BF16), tri)


def kernel(x, mix_norm_even, w_in_even, sb_q_gain, sb_k_gain, pool_w, pool_scale, w_out_even,
           mix_norm_odd, w_in_odd, gla_w_a2, gla_b_a, gla_o_gain, w_out_odd,
           ffn_norm, ffn_w_up, ffn_conv_w, ffn_conv_b, ffn_w_down):
    batch, seq, _ = x.shape
    depth = ffn_norm.shape[0]
    for layer in range(depth):
        i = layer // 2
        if layer % 2 == 0:
            q, k, v, xb = _even_in(x.reshape(batch * seq, D_MODEL), mix_norm_even[i],
                                   w_in_even[i], sb_q_gain[i], sb_k_gain[i])
            o_a = _attention(q, k, v, batch, seq)
            x = _even_out(xb, o_a, x, pool_w[i], pool_scale[i], w_out_even[i])
        else:
            x = _gla_mixer(x, mix_norm_odd[i], w_in_odd[i], gla_w_a2[i], gla_b_a[i],
                           gla_o_gain[i], w_out_odd[i])
        x = _ffn(x, ffn_norm[layer], ffn_w_up[layer], ffn_conv_w[layer],
                 ffn_conv_b[layer], ffn_w_down[layer])
    return x
```

```python
import functools

import jax
import jax.numpy as jnp
from jax import lax
from jax.experimental import pallas as pl
from jax.experimental.pallas import tpu as pltpu

F32 = jnp.float32
BF16 = jnp.bfloat16

D_MODEL = 1024
NORM_EPS = 1e-6
LOG2E = 1.4426950408889634
UNDERFLOW_LOG2 = 160.0

SB_HEADS = 8
SB_HEAD_DIM = 64
SB_WIDTH = SB_HEADS * SB_HEAD_DIM
POOL_WINDOWS = (2, 4, 8, 16)
POOL_GROUP_DIM = 128
POOL_WIDTH = 512
POOL_HALO = 16

GLA_HEADS = 4
GLA_KEY_WIDTH = 512
GLA_VALUE_WIDTH = 1024
GLA_KEY_DIM = 128
GLA_VALUE_DIM = 256
GLA_GATE_RANK = 16
GLA_TAU = 16.0
GLA_CHUNK = 64
GLA_GATE_PAD = 128

FFN_HIDDEN = 2816
FFN_CHUNK = 256
FFN_NCHUNK = FFN_HIDDEN // FFN_CHUNK

IN_TM = 1024
ATT_TQ = 512
ATT_TK = 128
ATT_PAIRS = 2
ATT_NB = 2
OUT_TM = 1024
FFN_TM = 512
GLA_TM = 512
GLA_TRI_SPAN = 256

VMEM_LIMIT = 56 * 1024 * 1024


def _rms(x, gain):
    ms = jnp.mean(x * x, axis=-1, keepdims=True)
    return x * lax.rsqrt(ms + NORM_EPS) * gain


def _log1pexp_neg_abs(z):
    return jnp.log(1.0 + jnp.exp(jnp.minimum(z, -z)))


def _split_bf16(x):
    hi = x.astype(BF16)
    lo = (x - hi.astype(F32)).astype(BF16)
    return hi, lo


def _even_in_kernel(x_ref, g_ref, w_ref, qg_ref, kg_ref, bd_ref,
                    q_ref, k_ref, v_ref, xb_ref):
    h = _rms(x_ref[...], g_ref[...]).astype(BF16)
    proj = jnp.dot(h, w_ref[...], preferred_element_type=F32)

    def head_norm(t, gain):
        ss = jnp.dot((t * t).astype(BF16), bd_ref[...], preferred_element_type=F32)
        return t * lax.rsqrt(ss * (1.0 / SB_HEAD_DIM) + NORM_EPS) * gain

    q = head_norm(proj[:, 0:SB_WIDTH], qg_ref[...])
    q_ref[...] = (q * (SB_HEAD_DIM ** -0.5 * LOG2E)).astype(BF16)
    k_ref[...] = head_norm(proj[:, SB_WIDTH:2 * SB_WIDTH], kg_ref[...]).astype(BF16)
    v_ref[...] = proj[:, 2 * SB_WIDTH:3 * SB_WIDTH].astype(BF16)
    xb_ref[...] = proj[:, 3 * SB_WIDTH:]


def _even_in(x2d, gain, w_in, q_gain, k_gain):
    n = x2d.shape[0]
    tm = min(IN_TM, n)
    head = jnp.arange(SB_WIDTH) // SB_HEAD_DIM
    blockdiag = (head[:, None] == head[None, :]).astype(BF16)
    width = w_in.shape[1]
    const = lambda i: (0, 0)
    tile = lambda i: (i, 0)
    return pl.pallas_call(
        _even_in_kernel,
        grid=(n // tm,),
        in_specs=[
            pl.BlockSpec((tm, D_MODEL), tile),
            pl.BlockSpec((1, D_MODEL), const),
            pl.BlockSpec((D_MODEL, width), const),
            pl.BlockSpec((1, SB_WIDTH), const),
            pl.BlockSpec((1, SB_WIDTH), const),
            pl.BlockSpec((SB_WIDTH, SB_WIDTH), const),
        ],
        out_specs=[
            pl.BlockSpec((tm, SB_WIDTH), tile),
            pl.BlockSpec((tm, SB_WIDTH), tile),
            pl.BlockSpec((tm, SB_WIDTH), tile),
            pl.BlockSpec((tm, POOL_WIDTH), tile),
        ],
        out_shape=[
            jax.ShapeDtypeStruct((n, SB_WIDTH), BF16),
            jax.ShapeDtypeStruct((n, SB_WIDTH), BF16),
            jax.ShapeDtypeStruct((n, SB_WIDTH), BF16),
            jax.ShapeDtypeStruct((n, POOL_WIDTH), F32),
        ],
        compiler_params=pltpu.CompilerParams(
            dimension_semantics=("arbitrary",), vmem_limit_bytes=VMEM_LIMIT),
        name="even_in",
    )(x2d, gain.reshape(1, D_MODEL), w_in.astype(BF16),
      jnp.tile(q_gain, SB_HEADS).reshape(1, SB_WIDTH),
      jnp.tile(k_gain, SB_HEADS).reshape(1, SB_WIDTH), blockdiag)


def _attn_kernel(q_ref, k_ref, v_ref, m2_ref, o_ref, za_ref, zb_ref, wa_ref, wb_ref,
                 c0_ref, c1_ref, acc_ref, *, tq, tk, nb):
    i = pl.program_id(2)
    npairs = q_ref.shape[2] // (2 * SB_HEAD_DIM)
    ratio = tq // tk
    ngroups = i * (ratio // nb)
    lane = lax.broadcasted_iota(jnp.int32, (tk, 2 * SB_HEAD_DIM), 1)
    first_head = lane < SB_HEAD_DIM
    zero = jnp.zeros((), BF16)

    def stack_heads(blk):
        return jnp.concatenate(
            [jnp.where(first_head, blk, zero), jnp.where(first_head, zero, blk)], axis=0)

    def stacked(ref, p, start, n):
        lanes = slice(p * 2 * SB_HEAD_DIM, (p + 1) * 2 * SB_HEAD_DIM)
        return jnp.concatenate(
            [stack_heads(ref[0, pl.ds(start + u * tk, tk), lanes]) for u in range(n)], axis=0)

    def scores(p, q_rows, start, n):
        return lax.dot_general(q_rows, stacked(k_ref, p, start, n), (((1,), (1,)), ((), ())),
                               preferred_element_type=F32)

    def attend(p, w, start, n):
        return jnp.dot(w, stacked(v_ref, p, start, n), preferred_element_type=F32)

    def group_start(g):
        return pl.multiple_of(jnp.maximum(i * ratio - (g + 1) * nb, 0) * tk, tk)

    def block_weights(z2, c0, c1, visible):
        ws, cs = [], []
        for hh, c in ((0, c0), (1, c1)):
            z = z2[:, hh * tk:(hh + 1) * tk]
            softplus = jnp.maximum(z, 0.0) + jnp.log2(1.0 + jnp.exp2(jnp.minimum(z, -z)))
            if visible is not None:
                softplus = jnp.where(visible, softplus, 0.0)
            hi, lo = _split_bf16(softplus)
            cum = jnp.dot(jnp.concatenate([hi, lo], axis=1), m2_ref[...],
                          preferred_element_type=F32)
            w = jnp.exp2(z + cum[:, :tk] + c)
            if visible is not None:
                w = jnp.where(visible, w, 0.0)
            ws.append(w.astype(BF16))
            cs.append(c + cum[:, tk:])
        return jnp.concatenate(ws, axis=1), cs[0], cs[1]

    def pair_query(p):
        return q_ref[0, :, p * 2 * SB_HEAD_DIM:(p + 1) * 2 * SB_HEAD_DIM]

    def diagonal(p, q):
        c0 = jnp.zeros((tq, tk), F32)
        c1 = jnp.zeros((tq, tk), F32)
        acc = jnp.zeros((tq, 2 * SB_HEAD_DIM), F32)
        for u in reversed(range(ratio)):
            r0 = u * tk
            start = pl.multiple_of(i * tq + r0, tk)
            row = lax.broadcasted_iota(jnp.int32, (tq - r0, tk), 0)
            col = lax.broadcasted_iota(jnp.int32, (tq - r0, tk), 1)
            w, p0, p1 = block_weights(scores(p, q[r0:, :], start, 1),
                                      c0[r0:, :], c1[r0:, :], col < row)
            pacc = acc[r0:, :] + attend(p, w, start, 1)
            if r0:
                c0 = jnp.concatenate([c0[:r0, :], p0], axis=0)
                c1 = jnp.concatenate([c1[:r0, :], p1], axis=0)
                acc = jnp.concatenate([acc[:r0, :], pacc], axis=0)
            else:
                c0, c1, acc = p0, p1, pacc
        return c0, c1, acc

    def spent(c0, c1):
        return jnp.maximum(jnp.max(c0), jnp.max(c1)) <= -UNDERFLOW_LOG2

    def stacked_all(ref, start):
        blk = ref[0, pl.ds(start, tk), :]
        head = lax.broadcasted_iota(jnp.int32, blk.shape, 1) // SB_HEAD_DIM
        return jnp.concatenate(
            [jnp.where(head == h, blk, zero) for h in range(2 * npairs)], axis=0)

    def near_short():
        q = q_ref[0]
        nheads = 2 * npairs
        c = [jnp.zeros((tq, tk), F32)] * nheads
        acc = jnp.zeros((tq, q.shape[1]), F32)

        def visit(r0, r1, start, visible, off_diagonal):
            z = lax.dot_general(q[r0:r1, :], stacked_all(k_ref, start),
                                (((1,), (1,)), ((), ())), preferred_element_type=F32)
            ws, cs = [], []
            for p in range(npairs):
                w, p0, p1 = block_weights(z[:, p * 2 * tk:(p + 1) * 2 * tk],
                                          c[2 * p][r0:r1, :], c[2 * p + 1][r0:r1, :], visible)
                ws.append(w)
                cs += [p0, p1]
            w = jnp.concatenate(ws, axis=1)
            if off_diagonal:
                w = jnp.where(i > 0, w, zero)
            return cs, acc[r0:r1, :] + jnp.dot(w, stacked_all(v_ref, start),
                                               preferred_element_type=F32)

        for u in reversed(range(ratio)):
            r0 = u * tk
            row = lax.broadcasted_iota(jnp.int32, (tq - r0, tk), 0)
            col = lax.broadcasted_iota(jnp.int32, (tq - r0, tk), 1)
            cs, pacc = visit(r0, tq, pl.multiple_of(i * tq + r0, tk), col < row, False)
            if r0:
                c = [jnp.concatenate([old[:r0, :], new], axis=0) for old, new in zip(c, cs)]
                acc = jnp.concatenate([acc[:r0, :], pacc], axis=0)
            else:
                c, acc = cs, pacc
        enough = [spent(c[2 * p][2 * tk:, :], c[2 * p + 1][2 * tk:, :]) for p in range(npairs)]
        for b in (1, 2):
            rows = (3 - b) * tk
            c, pacc = visit(0, rows, pl.multiple_of(jnp.maximum(i * ratio - b, 0) * tk, tk),
                            None, True)
            acc = jnp.concatenate([pacc, acc[rows:, :]], axis=0)
            enough = [jnp.logical_and(e, spent(c[2 * p][rows - tk:, :], c[2 * p + 1][rows - tk:, :]))
                      for p, e in enumerate(enough)]
        for p in range(npairs):
            acc_ref[p] = acc[:, p * 2 * SB_HEAD_DIM:(p + 1) * 2 * SB_HEAD_DIM]
        return [jnp.logical_or(i == 0, e) for e in enough]

    def near(p):
        q = pair_query(p)
        c0, c1, acc = diagonal(p, q)
        z_near = scores(p, q, group_start(0), nb)
        w_near = [None] * nb
        for u in reversed(range(nb)):
            w_near[u], c0, c1 = block_weights(z_near[:, u * 2 * tk:(u + 1) * 2 * tk],
                                              c0, c1, None)
        w_near = jnp.where(i > 0, jnp.concatenate(w_near, axis=1), zero)
        acc = acc + attend(p, w_near, group_start(0), nb)
        c0_ref[p] = c0
        c1_ref[p] = c1
        acc_ref[p] = acc

    def far(p):
        q = pair_query(p)

        def stage(g, z_in, z_out, w_in, w_out):
            acc_ref[p] += attend(p, w_in[...], group_start(g - 1), nb)
            z_out[...] = scores(p, q, group_start(g + 1), nb)
            for u in reversed(range(nb)):
                cols = slice(u * 2 * tk, (u + 1) * 2 * tk)
                w, c0, c1 = block_weights(z_in[:, cols], c0_ref[p], c1_ref[p], None)
                c0_ref[p] = c0
                c1_ref[p] = c1
                w_out[:, cols] = w

        def any_weight_left():
            return jnp.maximum(jnp.max(c0_ref[p]), jnp.max(c1_ref[p])) > -UNDERFLOW_LOG2

        farther = jnp.logical_and(ngroups > 1, any_weight_left())

        @pl.when(farther)
        def _():
            wb_ref[...] = jnp.zeros_like(wb_ref)
            za_ref[...] = scores(p, q, group_start(1), nb)

        def cond(state):
            g, alive = state
            return jnp.logical_and(g < ngroups, alive)

        def body(state):
            g, _ = state
            stage(g, za_ref, zb_ref, wb_ref, wa_ref)
            more = jnp.logical_and(g + 1 < ngroups, any_weight_left())

            @pl.when(more)
            def _():
                stage(g + 1, zb_ref, za_ref, wa_ref, wb_ref)

            return g + 1 + more.astype(jnp.int32), any_weight_left()

        last, _ = lax.while_loop(cond, body, (jnp.int32(1), farther))

        stages = last - 1

        @pl.when(stages % 2 == 1)
        def _():
            acc_ref[p] += attend(p, wa_ref[...], group_start(last - 1), nb)

        @pl.when(jnp.logical_and(stages > 0, stages % 2 == 0))
        def _():
            acc_ref[p] += attend(p, wb_ref[...], group_start(last - 1), nb)

        o_ref[0, :, p * 2 * SB_HEAD_DIM:(p + 1) * 2 * SB_HEAD_DIM] = acc_ref[p].astype(BF16)

    enough = near_short()
    for p in range(npairs):
        @pl.when(enough[p])
        def _(p=p):
            o_ref[0, :, p * 2 * SB_HEAD_DIM:(p + 1) * 2 * SB_HEAD_DIM] = acc_ref[p].astype(BF16)

        @pl.when(jnp.logical_not(enough[p]))
        def _(p=p):
            near(p)
            far(p)


def _attention(q, k, v, batch, seq):
    tq = min(ATT_TQ, seq)
    tk = min(ATT_TK, tq)
    q3 = q.reshape(batch, seq, SB_WIDTH)
    k3 = k.reshape(batch, seq, SB_WIDTH)
    v3 = v.reshape(batch, seq, SB_WIDTH)
    nb = ATT_NB
    assert seq % tq == 0 and (tq // tk) % nb == 0 and tq // tk >= 3
    r = jnp.arange(2 * tk) % tk
    ccol = jnp.arange(2 * tk)
    m2 = -jnp.where(ccol[None, :] < tk, r[:, None] >= ccol[None, :], True).astype(BF16)
    width = ATT_PAIRS * 2 * SB_HEAD_DIM
    return pl.pallas_call(
        functools.partial(_attn_kernel, tq=tq, tk=tk, nb=nb),
        grid=(batch, SB_WIDTH // width, seq // tq),
        scratch_shapes=[
            pltpu.VMEM((tq, nb * 2 * tk), F32), pltpu.VMEM((tq, nb * 2 * tk), F32),
            pltpu.VMEM((tq, nb * 2 * tk), BF16), pltpu.VMEM((tq, nb * 2 * tk), BF16),
            pltpu.VMEM((ATT_PAIRS, tq, tk), F32), pltpu.VMEM((ATT_PAIRS, tq, tk), F32),
            pltpu.VMEM((ATT_PAIRS, tq, 2 * SB_HEAD_DIM), F32),
        ],
        in_specs=[
            pl.BlockSpec((1, tq, width), lambda b, p, i: (b, i, p)),
            pl.BlockSpec((1, seq, width), lambda b, p, i: (b, 0, p)),
            pl.BlockSpec((1, seq, width), lambda b, p, i: (b, 0, p)),
            pl.BlockSpec((2 * tk, 2 * tk), lambda b, p, i: (0, 0)),
        ],
        out_specs=pl.BlockSpec((1, tq, width), lambda b, p, i: (b, i, p)),
        out_shape=jax.ShapeDtypeStruct((batch, seq, SB_WIDTH), BF16),
        compiler_params=pltpu.CompilerParams(
            dimension_semantics=("arbitrary", "arbitrary", "arbitrary"),
            vmem_limit_bytes=VMEM_LIMIT),
        name="sb_attention",
    )(q3, k3, v3, m2)


def _even_out_kernel(xb_ref, halo_ref, oa_ref, x_ref, pw_ref, ps_ref, wo_ref, o_ref, *, tm):
    i = pl.program_id(1)
    xb = xb_ref[0]
    halo = jnp.where(i > 0, halo_ref[0], 0.0)
    ext = jnp.concatenate([halo, xb], axis=0)
    pos = (i * tm + 1 + lax.broadcasted_iota(jnp.int32, (tm, 1), 0)).astype(F32)
    y = jnp.dot(oa_ref[0], wo_ref[0:SB_WIDTH, :], preferred_element_type=F32)
    pooled_out = []
    for g, w in enumerate(POOL_WINDOWS):
        lanes = slice(g * POOL_GROUP_DIM, (g + 1) * POOL_GROUP_DIM)
        s = ext[:, lanes]
        sh = 1
        while sh < w:
            s = s + pltpu.roll(s, sh, axis=0)
            sh *= 2
        window_sum = s[POOL_HALO:, :]
        pooled = window_sum / jnp.minimum(pos, float(w)) - xb[:, lanes]
        ob = jnp.dot(pooled.astype(BF16), pw_ref[g], preferred_element_type=F32)
        pooled_out.append((ob * ps_ref[:, lanes]).astype(BF16))
    y = y + jnp.dot(jnp.concatenate(pooled_out, axis=1), wo_ref[SB_WIDTH:, :],
                    preferred_element_type=F32)
    o_ref[0] = x_ref[0] + y


def _even_out(xb, o_a, x, pool_w, pool_scale, w_out):
    batch, seq, _ = x.shape
    tm = min(OUT_TM, seq)
    xb3 = xb.reshape(batch, seq, POOL_WIDTH)
    hb = tm // POOL_HALO
    const2 = lambda b, i: (0, 0)
    tile = lambda b, i: (b, i, 0)
    return pl.pallas_call(
        functools.partial(_even_out_kernel, tm=tm),
        grid=(batch, seq // tm),
        in_specs=[
            pl.BlockSpec((1, tm, POOL_WIDTH), tile),
            pl.BlockSpec((1, POOL_HALO, POOL_WIDTH),
                         lambda b, i: (b, jnp.maximum(i * hb - 1, 0), 0)),
            pl.BlockSpec((1, tm, SB_WIDTH), tile),
            pl.BlockSpec((1, tm, D_MODEL), tile),
            pl.BlockSpec((len(POOL_WINDOWS), POOL_GROUP_DIM, POOL_GROUP_DIM),
                         lambda b, i: (0, 0, 0)),
            pl.BlockSpec((1, POOL_WIDTH), const2),
            pl.BlockSpec((D_MODEL, D_MODEL), const2),
        ],
        out_specs=pl.BlockSpec((1, tm, D_MODEL), tile),
        out_shape=jax.ShapeDtypeStruct((batch, seq, D_MODEL), F32),
        compiler_params=pltpu.CompilerParams(
            dimension_semantics=("arbitrary", "arbitrary"), vmem_limit_bytes=VMEM_LIMIT),
        name="even_out",
    )(xb3, xb3, o_a, x, pool_w.astype(BF16), pool_scale.reshape(1, POOL_WIDTH),
      w_out.astype(BF16))


def _ffn_kernel(x_ref, g_ref, wup_ref, cw_ref, wd_ref, o_ref,
                h_ref, carry_ref, u_ref, act_ref, *, tm):
    i = pl.program_id(1)
    nch = FFN_NCHUNK

    @pl.when(i == 0)
    def _():
        carry_ref[...] = jnp.zeros_like(carry_ref)

    x = x_ref[0]
    h_ref[...] = _rms(x, g_ref[...]).astype(BF16)

    def columns(c):
        return (slice(c * FFN_CHUNK, (c + 1) * FFN_CHUNK),
                slice(FFN_HIDDEN + c * FFN_CHUNK, FFN_HIDDEN + (c + 1) * FFN_CHUNK))

    def up(c):
        h = h_ref[...]
        u = jnp.concatenate([jnp.dot(h, wup_ref[:, cols], preferred_element_type=F32)
                             for cols in columns(c)], axis=1)
        u_ref[0:8, :] = carry_ref[c]
        u_ref[8:8 + tm, :] = u
        carry_ref[c] = u[tm - 8:, :]

    def activate(c):
        cw = jnp.concatenate([cw_ref[:, cols] for cols in columns(c)], axis=1)
        conv = (cw[3:4, :] + cw[2:3, :] * u_ref[8:8 + tm, :]
                + cw[1:2, :] * u_ref[7:7 + tm, :]
                + cw[0:1, :] * u_ref[6:6 + tm, :])
        a = conv[:, :FFN_CHUNK]
        gate = conv[:, FFN_CHUNK:]
        act_ref[:, columns(c)[0]] = (a * jax.nn.sigmoid(a) * gate).astype(BF16)

    for c in range(nch):
        up(c)
        activate(c)
    o_ref[0] = x + jnp.dot(act_ref[...], wd_ref[...], preferred_element_type=F32)


def _ffn(x, gain, w_up, conv_w, conv_b, w_down):
    batch, seq, _ = x.shape
    tm = min(FFN_TM, seq)
    ck, nch = FFN_CHUNK, FFN_NCHUNK

    taps = jnp.concatenate([conv_w, conv_b[None, :],
                            jnp.zeros((4, 2 * FFN_HIDDEN), F32)], axis=0)
    tile = lambda b, i: (b, i, 0)
    const2 = lambda b, i: (0, 0)
    const3 = lambda b, i: (0, 0, 0)
    return pl.pallas_call(
        functools.partial(_ffn_kernel, tm=tm),
        grid=(batch, seq // tm),
        in_specs=[
            pl.BlockSpec((1, tm, D_MODEL), tile),
            pl.BlockSpec((1, D_MODEL), const2),
            pl.BlockSpec((D_MODEL, 2 * FFN_HIDDEN), const2),
            pl.BlockSpec((8, 2 * FFN_HIDDEN), const2),
            pl.BlockSpec((FFN_HIDDEN, D_MODEL), const2),
        ],
        out_specs=pl.BlockSpec((1, tm, D_MODEL), tile),
        out_shape=jax.ShapeDtypeStruct((batch, seq, D_MODEL), F32),
        scratch_shapes=[
            pltpu.VMEM((tm, D_MODEL), BF16),
            pltpu.VMEM((nch, 8, 2 * ck), F32),
            pltpu.VMEM((tm + 8, 2 * ck), F32),
            pltpu.VMEM((tm, FFN_HIDDEN), BF16),
        ],
        compiler_params=pltpu.CompilerParams(
            dimension_semantics=("arbitrary", "arbitrary"), vmem_limit_bytes=VMEM_LIMIT),
        name="conv_ffn",
    )(x, gain.reshape(1, D_MODEL), w_up.astype(BF16), taps, w_down.astype(BF16))


def _gla_kernel(x_ref, g_ref, win_ref, wa2_ref, ba_ref, og_ref, wout_ref, tri_ref,
                o_ref, state_ref, *, tm):
    i = pl.program_id(1)

    @pl.when(i == 0)
    def _():
        state_ref[...] = jnp.zeros_like(state_ref)

    x = x_ref[0]
    h = _rms(x, g_ref[...]).astype(BF16)
    proj = jnp.dot(h, win_ref[...], preferred_element_type=F32)
    kw, vw = GLA_KEY_WIDTH, GLA_VALUE_WIDTH
    q = proj[:, 0:kw] * (GLA_KEY_DIM ** -0.5)
    k = proj[:, kw:2 * kw]
    v = proj[:, 2 * kw:2 * kw + vw]
    r = proj[:, 2 * kw + vw:2 * kw + 2 * vw]
    a_low = proj[:, 2 * kw + 2 * vw:]
    gate = jnp.dot(a_low.astype(BF16), wa2_ref[...], preferred_element_type=F32) + ba_ref[...]
    log_alpha = (jnp.minimum(gate, 0.0) - _log1pexp_neg_abs(gate)) * (1.0 / GLA_TAU)
    hi, lo = _split_bf16(log_alpha)
    tri = tri_ref[...]
    span = tri.shape[0]
    cum = jnp.concatenate(
        [jnp.dot(tri, hi[r0:r0 + span, :], preferred_element_type=F32)
         + jnp.dot(tri, lo[r0:r0 + span, :], preferred_element_type=F32)
         for r0 in range(0, tm, span)], axis=0)

    nchunk = tm // GLA_CHUNK
    klanes = [slice(hd * GLA_KEY_DIM, (hd + 1) * GLA_KEY_DIM) for hd in range(GLA_HEADS)]
    vlanes = [slice(hd * GLA_VALUE_DIM, (hd + 1) * GLA_VALUE_DIM) for hd in range(GLA_HEADS)]
    updates, decays = [], []
    for c in range(nchunk):
        rows = slice(c * GLA_CHUNK, (c + 1) * GLA_CHUNK)
        cum_c = cum[rows, :]
        total = cum_c[GLA_CHUNK - 1:GLA_CHUNK, :]
        k_dec = k[rows, :] * jnp.exp(total - cum_c)
        decay = jnp.broadcast_to(jnp.exp(total), (GLA_CHUNK, kw))
        kd = jnp.concatenate([k_dec, decay], axis=0)
        v_c = v[rows, :].astype(BF16)
        for hd in range(GLA_HEADS):
            kd_t = kd[:, klanes[hd]].T
            decays.append(kd_t[:, GLA_CHUNK:GLA_CHUNK + 1])
            updates.append(jnp.dot(kd_t[:, :GLA_CHUNK].astype(BF16), v_c[:, vlanes[hd]],
                                   preferred_element_type=F32))
    states = []
    for hd in range(GLA_HEADS):
        st = state_ref[hd]
        for c in range(nchunk):
            st = st * decays[c * GLA_HEADS + hd] + updates[c * GLA_HEADS + hd]
            states.append(st.astype(BF16))
        state_ref[hd] = st
    outs = []
    for c in range(nchunk):
        q_c = q[c * GLA_CHUNK:(c + 1) * GLA_CHUNK, :].astype(BF16)
        outs.append(jnp.concatenate(
            [jnp.dot(q_c[:, klanes[hd]], states[hd * nchunk + c], preferred_element_type=F32)
             for hd in range(GLA_HEADS)], axis=1))
    o = jnp.concatenate(outs, axis=0)
    normed = jnp.concatenate(
        [_rms(o[:, hd * GLA_VALUE_DIM:(hd + 1) * GLA_VALUE_DIM], og_ref[...])
         for hd in range(GLA_HEADS)], axis=1)
    gated = (normed * (r * jax.nn.sigmoid(r))).astype(BF16)
    o_ref[0] = x + jnp.dot(gated, wout_ref[...], preferred_element_type=F32)


def _gla_mixer(x, gain, w_in, w_a2, b_a, o_gain, w_out):
    batch, seq, _ = x.shape
    tm = min(GLA_TM, seq)
    in_width = 2 * GLA_KEY_WIDTH + 2 * GLA_VALUE_WIDTH + GLA_GATE_PAD
    pad = GLA_GATE_PAD - GLA_GATE_RANK
    w_in_p = jnp.pad(w_in, ((0, 0), (0, pad))).astype(BF16)
    w_a2_p = jnp.pad(w_a2, ((0, pad), (0, 0))).astype(BF16)
    span = min(GLA_TRI_SPAN, tm)
    t = jnp.arange(span)
    tri = ((t[:, None] // GLA_CHUNK == t[None, :] // GLA_CHUNK)
           & (t[:, None] >= t[None, :])).astype(BF16)
    tile = lambda b, i: (b, i, 0)
    const2 = lambda b, i: (0, 0)
    return pl.pallas_call(
        functools.partial(_gla_kernel, tm=tm),
        grid=(batch, seq // tm),
        in_specs=[
            pl.BlockSpec((1, tm, D_MODEL), tile),
            pl.BlockSpec((1, D_MODEL), const2),
            pl.BlockSpec((D_MODEL, in_width), const2),
            pl.BlockSpec((GLA_GATE_PAD, GLA_KEY_WIDTH), const2),
            pl.BlockSpec((1, GLA_KEY_WIDTH), const2),
            pl.BlockSpec((1, GLA_VALUE_DIM), const2),
            pl.BlockSpec((GLA_VALUE_WIDTH, D_MODEL), const2),
            pl.BlockSpec((span, span), const2),
        ],
        out_specs=pl.BlockSpec((1, tm, D_MODEL), tile),
        out_shape=jax.ShapeDtypeStruct((batch, seq, D_MODEL), F32),
        scratch_shapes=[pltpu.VMEM((GLA_HEADS, GLA_KEY_DIM, GLA_VALUE_DIM), F32)],
        compiler_params=pltpu.CompilerParams(
            dimension_semantics=("arbitrary", "arbitrary"), vmem_limit_bytes=VMEM_LIMIT),
        name="gla_mixer",
    )(x, gain.reshape(1, D_MODEL), w_in_p, w_a2_p, b_a.reshape(1, GLA_KEY_WIDTH),
      o_gain.reshape(1, GLA_VALUE_DIM), w_out.astype(BF16), tri)


def kernel(x, mix_norm_even, w_in_even, sb_q_gain, sb_k_gain, pool_w, pool_scale, w_out_even,
           mix_norm_odd, w_in_odd, gla_w_a2, gla_b_a, gla_o_gain, w_out_odd,
           ffn_norm, ffn_w_up, ffn_conv_w, ffn_conv_b, ffn_w_down):
    batch, seq, _ = x.shape
    depth = ffn_norm.shape[0]
    for layer in range(depth):
        i = layer // 2
        if layer % 2 == 0:
            q, k, v, xb = _even_in(x.reshape(batch * seq, D_MODEL), mix_norm_even[i],
                                   w_in_even[i], sb_q_gain[i], sb_k_gain[i])
            o_a = _attention(q, k, v, batch, seq)
            x = _even_out(xb, o_a, x, pool_w[i], pool_scale[i], w_out_even[i])
        else:
            x = _gla_mixer(x, mix_norm_odd[i], w_in_odd[i], gla_w_a2[i], gla_b_a[i],
                           gla_o_gain[i], w_out_odd[i])
        x = _ffn(x, ffn_norm[layer], ffn_w_up[layer], ffn_conv_w[layer],
                 ffn_conv_b[layer], ffn_w_down[layer])
    return x
```

```python
import functools

import jax
import jax.numpy as jnp
from jax import lax
from jax.experimental import pallas as pl
from jax.experimental.pallas import tpu as pltpu

F32 = jnp.float32
BF16 = jnp.bfloat16

D_MODEL = 1024
NORM_EPS = 1e-6
LOG2E = 1.4426950408889634
UNDERFLOW_LOG2 = 160.0

SB_HEADS = 8
SB_HEAD_DIM = 64
SB_WIDTH = SB_HEADS * SB_HEAD_DIM
POOL_WINDOWS = (2, 4, 8, 16)
POOL_GROUP_DIM = 128
POOL_WIDTH = 512
POOL_HALO = 16

GLA_HEADS = 4
GLA_KEY_WIDTH = 512
GLA_VALUE_WIDTH = 1024
GLA_KEY_DIM = 128
GLA_VALUE_DIM = 256
GLA_GATE_RANK = 16
GLA_TAU = 16.0
GLA_CHUNK = 64
GLA_GATE_PAD = 128

FFN_HIDDEN = 2816
FFN_CHUNK = 256
FFN_NCHUNK = FFN_HIDDEN // FFN_CHUNK

IN_TM = 1024
ATT_TQ = 512
ATT_TK = 128
ATT_PAIRS = 2
ATT_NB = 2
OUT_TM = 1024
FFN_TM = 512
GLA_TM = 1024
GLA_TRI_SPAN = 256

VMEM_LIMIT = 56 * 1024 * 1024


def _rms(x, gain):
    ms = jnp.mean(x * x, axis=-1, keepdims=True)
    return x * lax.rsqrt(ms + NORM_EPS) * gain


def _log1pexp_neg_abs(z):
    return jnp.log(1.0 + jnp.exp(jnp.minimum(z, -z)))


def _split_bf16(x):
    hi = x.astype(BF16)
    lo = (x - hi.astype(F32)).astype(BF16)
    return hi, lo


def _even_in_kernel(x_ref, g_ref, w_ref, qg_ref, kg_ref, bd_ref,
                    q_ref, k_ref, v_ref, xb_ref):
    h = _rms(x_ref[...], g_ref[...]).astype(BF16)
    proj = jnp.dot(h, w_ref[...], preferred_element_type=F32)

    def head_norm(t, gain):
        ss = jnp.dot((t * t).astype(BF16), bd_ref[...], preferred_element_type=F32)
        return t * lax.rsqrt(ss * (1.0 / SB_HEAD_DIM) + NORM_EPS) * gain

    q = head_norm(proj[:, 0:SB_WIDTH], qg_ref[...])
    q_ref[...] = (q * (SB_HEAD_DIM ** -0.5 * LOG2E)).astype(BF16)
    k_ref[...] = head_norm(proj[:, SB_WIDTH:2 * SB_WIDTH], kg_ref[...]).astype(BF16)
    v_ref[...] = proj[:, 2 * SB_WIDTH:3 * SB_WIDTH].astype(BF16)
    xb_ref[...] = proj[:, 3 * SB_WIDTH:]


def _even_in(x2d, gain, w_in, q_gain, k_gain):
    n = x2d.shape[0]
    tm = min(IN_TM, n)
    head = jnp.arange(SB_WIDTH) // SB_HEAD_DIM
    blockdiag = (head[:, None] == head[None, :]).astype(BF16)
    width = w_in.shape[1]
    const = lambda i: (0, 0)
    tile = lambda i: (i, 0)
    return pl.pallas_call(
        _even_in_kernel,
        grid=(n // tm,),
        in_specs=[
            pl.BlockSpec((tm, D_MODEL), tile),
            pl.BlockSpec((1, D_MODEL), const),
            pl.BlockSpec((D_MODEL, width), const),
            pl.BlockSpec((1, SB_WIDTH), const),
            pl.BlockSpec((1, SB_WIDTH), const),
            pl.BlockSpec((SB_WIDTH, SB_WIDTH), const),
        ],
        out_specs=[
            pl.BlockSpec((tm, SB_WIDTH), tile),
            pl.BlockSpec((tm, SB_WIDTH), tile),
            pl.BlockSpec((tm, SB_WIDTH), tile),
            pl.BlockSpec((tm, POOL_WIDTH), tile),
        ],
        out_shape=[
            jax.ShapeDtypeStruct((n, SB_WIDTH), BF16),
            jax.ShapeDtypeStruct((n, SB_WIDTH), BF16),
            jax.ShapeDtypeStruct((n, SB_WIDTH), BF16),
            jax.ShapeDtypeStruct((n, POOL_WIDTH), F32),
        ],
        compiler_params=pltpu.CompilerParams(
            dimension_semantics=("arbitrary",), vmem_limit_bytes=VMEM_LIMIT),
        name="even_in",
    )(x2d, gain.reshape(1, D_MODEL), w_in.astype(BF16),
      jnp.tile(q_gain, SB_HEADS).reshape(1, SB_WIDTH),
      jnp.tile(k_gain, SB_HEADS).reshape(1, SB_WIDTH), blockdiag)


def _attn_kernel(q_ref, k_ref, v_ref, m2_ref, o_ref, za_ref, zb_ref, wa_ref, wb_ref,
                 c0_ref, c1_ref, acc_ref, *, tq, tk, nb):
    i = pl.program_id(2)
    npairs = q_ref.shape[2] // (2 * SB_HEAD_DIM)
    ratio = tq // tk
    ngroups = i * (ratio // nb)
    lane = lax.broadcasted_iota(jnp.int32, (tk, 2 * SB_HEAD_DIM), 1)
    first_head = lane < SB_HEAD_DIM
    zero = jnp.zeros((), BF16)

    def stack_heads(blk):
        return jnp.concatenate(
            [jnp.where(first_head, blk, zero), jnp.where(first_head, zero, blk)], axis=0)

    def stacked(ref, p, start, n):
        lanes = slice(p * 2 * SB_HEAD_DIM, (p + 1) * 2 * SB_HEAD_DIM)
        return jnp.concatenate(
            [stack_heads(ref[0, pl.ds(start + u * tk, tk), lanes]) for u in range(n)], axis=0)

    def scores(p, q_rows, start, n):
        return lax.dot_general(q_rows, stacked(k_ref, p, start, n), (((1,), (1,)), ((), ())),
                               preferred_element_type=F32)

    def attend(p, w, start, n):
        return jnp.dot(w, stacked(v_ref, p, start, n), preferred_element_type=F32)

    def group_start(g):
        return pl.multiple_of(jnp.maximum(i * ratio - (g + 1) * nb, 0) * tk, tk)

    def block_weights(z2, c0, c1, visible):
        ws, cs = [], []
        for hh, c in ((0, c0), (1, c1)):
            z = z2[:, hh * tk:(hh + 1) * tk]
            softplus = jnp.maximum(z, 0.0) + jnp.log2(1.0 + jnp.exp2(jnp.minimum(z, -z)))
            if visible is not None:
                softplus = jnp.where(visible, softplus, 0.0)
            hi, lo = _split_bf16(softplus)
            cum = jnp.dot(jnp.concatenate([hi, lo], axis=1), m2_ref[...],
                          preferred_element_type=F32)
            w = jnp.exp2(z + cum[:, :tk] + c)
            if visible is not None:
                w = jnp.where(visible, w, 0.0)
            ws.append(w.astype(BF16))
            cs.append(c + cum[:, tk:])
        return jnp.concatenate(ws, axis=1), cs[0], cs[1]

    def pair_query(p):
        return q_ref[0, :, p * 2 * SB_HEAD_DIM:(p + 1) * 2 * SB_HEAD_DIM]

    def diagonal(p, q):
        c0 = jnp.zeros((tq, tk), F32)
        c1 = jnp.zeros((tq, tk), F32)
        acc = jnp.zeros((tq, 2 * SB_HEAD_DIM), F32)
        for u in reversed(range(ratio)):
            r0 = u * tk
            start = pl.multiple_of(i * tq + r0, tk)
            row = lax.broadcasted_iota(jnp.int32, (tq - r0, tk), 0)
            col = lax.broadcasted_iota(jnp.int32, (tq - r0, tk), 1)
            w, p0, p1 = block_weights(scores(p, q[r0:, :], start, 1),
                                      c0[r0:, :], c1[r0:, :], col < row)
            pacc = acc[r0:, :] + attend(p, w, start, 1)
            if r0:
                c0 = jnp.concatenate([c0[:r0, :], p0], axis=0)
                c1 = jnp.concatenate([c1[:r0, :], p1], axis=0)
                acc = jnp.concatenate([acc[:r0, :], pacc], axis=0)
            else:
                c0, c1, acc = p0, p1, pacc
        return c0, c1, acc

    def spent(c0, c1):
        return jnp.maximum(jnp.max(c0), jnp.max(c1)) <= -UNDERFLOW_LOG2

    def stacked_all(ref, start):
        blk = ref[0, pl.ds(start, tk), :]
        head = lax.broadcasted_iota(jnp.int32, blk.shape, 1) // SB_HEAD_DIM
        return jnp.concatenate(
            [jnp.where(head == h, blk, zero) for h in range(2 * npairs)], axis=0)

    def near_short():
        q = q_ref[0]
        nheads = 2 * npairs
        c = [jnp.zeros((tq, tk), F32)] * nheads
        acc = jnp.zeros((tq, q.shape[1]), F32)

        def visit(r0, r1, start, visible, off_diagonal):
            z = lax.dot_general(q[r0:r1, :], stacked_all(k_ref, start),
                                (((1,), (1,)), ((), ())), preferred_element_type=F32)
            ws, cs = [], []
            for p in range(npairs):
                w, p0, p1 = block_weights(z[:, p * 2 * tk:(p + 1) * 2 * tk],
                                          c[2 * p][r0:r1, :], c[2 * p + 1][r0:r1, :], visible)
                ws.append(w)
                cs += [p0, p1]
            w = jnp.concatenate(ws, axis=1)
            if off_diagonal:
                w = jnp.where(i > 0, w, zero)
            return cs, acc[r0:r1, :] + jnp.dot(w, stacked_all(v_ref, start),
                                               preferred_element_type=F32)

        for u in reversed(range(ratio)):
            r0 = u * tk
            row = lax.broadcasted_iota(jnp.int32, (tq - r0, tk), 0)
            col = lax.broadcasted_iota(jnp.int32, (tq - r0, tk), 1)
            cs, pacc = visit(r0, tq, pl.multiple_of(i * tq + r0, tk), col < row, False)
            if r0:
                c = [jnp.concatenate([old[:r0, :], new], axis=0) for old, new in zip(c, cs)]
                acc = jnp.concatenate([acc[:r0, :], pacc], axis=0)
            else:
                c, acc = cs, pacc
        enough = [spent(c[2 * p][2 * tk:, :], c[2 * p + 1][2 * tk:, :]) for p in range(npairs)]
        for b in (1, 2):
            rows = (3 - b) * tk
            c, pacc = visit(0, rows, pl.multiple_of(jnp.maximum(i * ratio - b, 0) * tk, tk),
                            None, True)
            acc = jnp.concatenate([pacc, acc[rows:, :]], axis=0)
            enough = [jnp.logical_and(e, spent(c[2 * p][rows - tk:, :], c[2 * p + 1][rows - tk:, :]))
                      for p, e in enumerate(enough)]
        for p in range(npairs):
            acc_ref[p] = acc[:, p * 2 * SB_HEAD_DIM:(p + 1) * 2 * SB_HEAD_DIM]
        return [jnp.logical_or(i == 0, e) for e in enough]

    def near(p):
        q = pair_query(p)
        c0, c1, acc = diagonal(p, q)
        z_near = scores(p, q, group_start(0), nb)
        w_near = [None] * nb
        for u in reversed(range(nb)):
            w_near[u], c0, c1 = block_weights(z_near[:, u * 2 * tk:(u + 1) * 2 * tk],
                                              c0, c1, None)
        w_near = jnp.where(i > 0, jnp.concatenate(w_near, axis=1), zero)
        acc = acc + attend(p, w_near, group_start(0), nb)
        c0_ref[p] = c0
        c1_ref[p] = c1
        acc_ref[p] = acc

    def far(p):
        q = pair_query(p)

        def stage(g, z_in, z_out, w_in, w_out):
            acc_ref[p] += attend(p, w_in[...], group_start(g - 1), nb)
            z_out[...] = scores(p, q, group_start(g + 1), nb)
            for u in reversed(range(nb)):
                cols = slice(u * 2 * tk, (u + 1) * 2 * tk)
                w, c0, c1 = block_weights(z_in[:, cols], c0_ref[p], c1_ref[p], None)
                c0_ref[p] = c0
                c1_ref[p] = c1
                w_out[:, cols] = w

        def any_weight_left():
            return jnp.maximum(jnp.max(c0_ref[p]), jnp.max(c1_ref[p])) > -UNDERFLOW_LOG2

        farther = jnp.logical_and(ngroups > 1, any_weight_left())

        @pl.when(farther)
        def _():
            wb_ref[...] = jnp.zeros_like(wb_ref)
            za_ref[...] = scores(p, q, group_start(1), nb)

        def cond(state):
            g, alive = state
            return jnp.logical_and(g < ngroups, alive)

        def body(state):
            g, _ = state
            stage(g, za_ref, zb_ref, wb_ref, wa_ref)
            more = jnp.logical_and(g + 1 < ngroups, any_weight_left())

            @pl.when(more)
            def _():
                stage(g + 1, zb_ref, za_ref, wa_ref, wb_ref)

            return g + 1 + more.astype(jnp.int32), any_weight_left()

        last, _ = lax.while_loop(cond, body, (jnp.int32(1), farther))

        stages = last - 1

        @pl.when(stages % 2 == 1)
        def _():
            acc_ref[p] += attend(p, wa_ref[...], group_start(last - 1), nb)

        @pl.when(jnp.logical_and(stages > 0, stages % 2 == 0))
        def _():
            acc_ref[p] += attend(p, wb_ref[...], group_start(last - 1), nb)

        o_ref[0, :, p * 2 * SB_HEAD_DIM:(p + 1) * 2 * SB_HEAD_DIM] = acc_ref[p].astype(BF16)

    enough = near_short()
    for p in range(npairs):
        @pl.when(enough[p])
        def _(p=p):
            o_ref[0, :, p * 2 * SB_HEAD_DIM:(p + 1) * 2 * SB_HEAD_DIM] = acc_ref[p].astype(BF16)

        @pl.when(jnp.logical_not(enough[p]))
        def _(p=p):
            near(p)
            far(p)


def _attention(q, k, v, batch, seq):
    tq = min(ATT_TQ, seq)
    tk = min(ATT_TK, tq)
    q3 = q.reshape(batch, seq, SB_WIDTH)
    k3 = k.reshape(batch, seq, SB_WIDTH)
    v3 = v.reshape(batch, seq, SB_WIDTH)
    nb = ATT_NB
    assert seq % tq == 0 and (tq // tk) % nb == 0 and tq // tk >= 3
    r = jnp.arange(2 * tk) % tk
    ccol = jnp.arange(2 * tk)
    m2 = -jnp.where(ccol[None, :] < tk, r[:, None] >= ccol[None, :], True).astype(BF16)
    width = ATT_PAIRS * 2 * SB_HEAD_DIM
    return pl.pallas_call(
        functools.partial(_attn_kernel, tq=tq, tk=tk, nb=nb),
        grid=(batch, SB_WIDTH // width, seq // tq),
        scratch_shapes=[
            pltpu.VMEM((tq, nb * 2 * tk), F32), pltpu.VMEM((tq, nb * 2 * tk), F32),
            pltpu.VMEM((tq, nb * 2 * tk), BF16), pltpu.VMEM((tq, nb * 2 * tk), BF16),
            pltpu.VMEM((ATT_PAIRS, tq, tk), F32), pltpu.VMEM((ATT_PAIRS, tq, tk), F32),
            pltpu.VMEM((ATT_PAIRS, tq, 2 * SB_HEAD_DIM), F32),
        ],
        in_specs=[
            pl.BlockSpec((1, tq, width), lambda b, p, i: (b, i, p)),
            pl.BlockSpec((1, seq, width), lambda b, p, i: (b, 0, p)),
            pl.BlockSpec((1, seq, width), lambda b, p, i: (b, 0, p)),
            pl.BlockSpec((2 * tk, 2 * tk), lambda b, p, i: (0, 0)),
        ],
        out_specs=pl.BlockSpec((1, tq, width), lambda b, p, i: (b, i, p)),
        out_shape=jax.ShapeDtypeStruct((batch, seq, SB_WIDTH), BF16),
        compiler_params=pltpu.CompilerParams(
            dimension_semantics=("arbitrary", "arbitrary", "arbitrary"),
            vmem_limit_bytes=VMEM_LIMIT),
        name="sb_attention",
    )(q3, k3, v3, m2)


def _even_out_kernel(xb_ref, halo_ref, oa_ref, x_ref, pw_ref, ps_ref, wo_ref, o_ref, *, tm):
    i = pl.program_id(1)
    xb = xb_ref[0]
    halo = jnp.where(i > 0, halo_ref[0], 0.0)
    ext = jnp.concatenate([halo, xb], axis=0)
    pos = (i * tm + 1 + lax.broadcasted_iota(jnp.int32, (tm, 1), 0)).astype(F32)
    y = jnp.dot(oa_ref[0], wo_ref[0:SB_WIDTH, :], preferred_element_type=F32)
    pooled_out = []
    for g, w in enumerate(POOL_WINDOWS):
        lanes = slice(g * POOL_GROUP_DIM, (g + 1) * POOL_GROUP_DIM)
        s = ext[:, lanes]
        sh = 1
        while sh < w:
            s = s + pltpu.roll(s, sh, axis=0)
            sh *= 2
        window_sum = s[POOL_HALO:, :]
        pooled = window_sum / jnp.minimum(pos, float(w)) - xb[:, lanes]
        ob = jnp.dot(pooled.astype(BF16), pw_ref[g], preferred_element_type=F32)
        pooled_out.append((ob * ps_ref[:, lanes]).astype(BF16))
    y = y + jnp.dot(jnp.concatenate(pooled_out, axis=1), wo_ref[SB_WIDTH:, :],
                    preferred_element_type=F32)
    o_ref[0] = x_ref[0] + y


def _even_out(xb, o_a, x, pool_w, pool_scale, w_out):
    batch, seq, _ = x.shape
    tm = min(OUT_TM, seq)
    xb3 = xb.reshape(batch, seq, POOL_WIDTH)
    hb = tm // POOL_HALO
    const2 = lambda b, i: (0, 0)
    tile = lambda b, i: (b, i, 0)
    return pl.pallas_call(
        functools.partial(_even_out_kernel, tm=tm),
        grid=(batch, seq // tm),
        in_specs=[
            pl.BlockSpec((1, tm, POOL_WIDTH), tile),
            pl.BlockSpec((1, POOL_HALO, POOL_WIDTH),
                         lambda b, i: (b, jnp.maximum(i * hb - 1, 0), 0)),
            pl.BlockSpec((1, tm, SB_WIDTH), tile),
            pl.BlockSpec((1, tm, D_MODEL), tile),
            pl.BlockSpec((len(POOL_WINDOWS), POOL_GROUP_DIM, POOL_GROUP_DIM),
                         lambda b, i: (0, 0, 0)),
            pl.BlockSpec((1, POOL_WIDTH), const2),
            pl.BlockSpec((D_MODEL, D_MODEL), const2),
        ],
        out_specs=pl.BlockSpec((1, tm, D_MODEL), tile),
        out_shape=jax.ShapeDtypeStruct((batch, seq, D_MODEL), F32),
        compiler_params=pltpu.CompilerParams(
            dimension_semantics=("arbitrary", "arbitrary"), vmem_limit_bytes=VMEM_LIMIT),
        name="even_out",
    )(xb3, xb3, o_a, x, pool_w.astype(BF16), pool_scale.reshape(1, POOL_WIDTH),
      w_out.astype(BF16))


def _ffn_kernel(x_ref, g_ref, wup_ref, cw_ref, wd_ref, o_ref,
                h_ref, carry_ref, u_ref, act_ref, *, tm):
    i = pl.program_id(1)
    nch = FFN_NCHUNK

    @pl.when(i == 0)
    def _():
        carry_ref[...] = jnp.zeros_like(carry_ref)

    x = x_ref[0]
    h_ref[...] = _rms(x, g_ref[...]).astype(BF16)

    def columns(c):
        return (slice(c * FFN_CHUNK, (c + 1) * FFN_CHUNK),
                slice(FFN_HIDDEN + c * FFN_CHUNK, FFN_HIDDEN + (c + 1) * FFN_CHUNK))

    def up(c):
        h = h_ref[...]
        u = jnp.concatenate([jnp.dot(h, wup_ref[:, cols], preferred_element_type=F32)
                             for cols in columns(c)], axis=1)
        u_ref[0:8, :] = carry_ref[c]
        u_ref[8:8 + tm, :] = u
        carry_ref[c] = u[tm - 8:, :]

    def activate(c):
        cw = jnp.concatenate([cw_ref[:, cols] for cols in columns(c)], axis=1)
        conv = (cw[3:4, :] + cw[2:3, :] * u_ref[8:8 + tm, :]
                + cw[1:2, :] * u_ref[7:7 + tm, :]
                + cw[0:1, :] * u_ref[6:6 + tm, :])
        a = conv[:, :FFN_CHUNK]
        gate = conv[:, FFN_CHUNK:]
        act_ref[:, columns(c)[0]] = (a * jax.nn.sigmoid(a) * gate).astype(BF16)

    for c in range(nch):
        up(c)
        activate(c)
    o_ref[0] = x + jnp.dot(act_ref[...], wd_ref[...], preferred_element_type=F32)


def _ffn(x, gain, w_up, conv_w, conv_b, w_down):
    batch, seq, _ = x.shape
    tm = min(FFN_TM, seq)
    ck, nch = FFN_CHUNK, FFN_NCHUNK

    taps = jnp.concatenate([conv_w, conv_b[None, :],
                            jnp.zeros((4, 2 * FFN_HIDDEN), F32)], axis=0)
    tile = lambda b, i: (b, i, 0)
    const2 = lambda b, i: (0, 0)
    const3 = lambda b, i: (0, 0, 0)
    return pl.pallas_call(
        functools.partial(_ffn_kernel, tm=tm),
        grid=(batch, seq // tm),
        in_specs=[
            pl.BlockSpec((1, tm, D_MODEL), tile),
            pl.BlockSpec((1, D_MODEL), const2),
            pl.BlockSpec((D_MODEL, 2 * FFN_HIDDEN), const2),
            pl.BlockSpec((8, 2 * FFN_HIDDEN), const2),
            pl.BlockSpec((FFN_HIDDEN, D_MODEL), const2),
        ],
        out_specs=pl.BlockSpec((1, tm, D_MODEL), tile),
        out_shape=jax.ShapeDtypeStruct((batch, seq, D_MODEL), F32),
        scratch_shapes=[
            pltpu.VMEM((tm, D_MODEL), BF16),
            pltpu.VMEM((nch, 8, 2 * ck), F32),
            pltpu.VMEM((tm + 8, 2 * ck), F32),
            pltpu.VMEM((tm, FFN_HIDDEN), BF16),
        ],
        compiler_params=pltpu.CompilerParams(
            dimension_semantics=("arbitrary", "arbitrary"), vmem_limit_bytes=VMEM_LIMIT),
        name="conv_ffn",
    )(x, gain.reshape(1, D_MODEL), w_up.astype(BF16), taps, w_down.astype(BF16))


def _gla_kernel(x_ref, g_ref, win_ref, wa2_ref, ba_ref, og_ref, wout_ref, tri_ref,
                o_ref, state_ref, *, tm):
    i = pl.program_id(1)

    @pl.when(i == 0)
    def _():
        state_ref[...] = jnp.zeros_like(state_ref)

    x = x_ref[0]
    h = _rms(x, g_ref[...]).astype(BF16)
    proj = jnp.dot(h, win_ref[...], preferred_element_type=F32)
    kw, vw = GLA_KEY_WIDTH, GLA_VALUE_WIDTH
    q = proj[:, 0:kw] * (GLA_KEY_DIM ** -0.5)
    k = proj[:, kw:2 * kw]
    v = proj[:, 2 * kw:2 * kw + vw]
    r = proj[:, 2 * kw + vw:2 * kw + 2 * vw]
    a_low = proj[:, 2 * kw + 2 * vw:]
    gate = jnp.dot(a_low.astype(BF16), wa2_ref[...], preferred_element_type=F32) + ba_ref[...]
    log_alpha = (jnp.minimum(gate, 0.0) - _log1pexp_neg_abs(gate)) * (1.0 / GLA_TAU)
    hi, lo = _split_bf16(log_alpha)
    tri = tri_ref[...]
    span = tri.shape[0]
    cum = jnp.concatenate(
        [jnp.dot(tri, hi[r0:r0 + span, :], preferred_element_type=F32)
         + jnp.dot(tri, lo[r0:r0 + span, :], preferred_element_type=F32)
         for r0 in range(0, tm, span)], axis=0)

    nchunk = tm // GLA_CHUNK
    klanes = [slice(hd * GLA_KEY_DIM, (hd + 1) * GLA_KEY_DIM) for hd in range(GLA_HEADS)]
    vlanes = [slice(hd * GLA_VALUE_DIM, (hd + 1) * GLA_VALUE_DIM) for hd in range(GLA_HEADS)]
    updates, decays = [], []
    for c in range(nchunk):
        rows = slice(c * GLA_CHUNK, (c + 1) * GLA_CHUNK)
        cum_c = cum[rows, :]
        total = cum_c[GLA_CHUNK - 1:GLA_CHUNK, :]
        k_dec = k[rows, :] * jnp.exp(total - cum_c)
        decay = jnp.broadcast_to(jnp.exp(total), (GLA_CHUNK, kw))
        kd = jnp.concatenate([k_dec, decay], axis=0)
        v_c = v[rows, :].astype(BF16)
        for hd in range(GLA_HEADS):
            kd_t = kd[:, klanes[hd]].T
            decays.append(kd_t[:, GLA_CHUNK:GLA_CHUNK + 1])
            updates.append(jnp.dot(kd_t[:, :GLA_CHUNK].astype(BF16), v_c[:, vlanes[hd]],
                                   preferred_element_type=F32))
    states = []
    for hd in range(GLA_HEADS):
        st = state_ref[hd]
        for c in range(nchunk):
            st = st * decays[c * GLA_HEADS + hd] + updates[c * GLA_HEADS + hd]
            states.append(st.astype(BF16))
        state_ref[hd] = st
    outs = []
    for c in range(nchunk):
        q_c = q[c * GLA_CHUNK:(c + 1) * GLA_CHUNK, :].astype(BF16)
        outs.append(jnp.concatenate(
            [jnp.dot(q_c[:, klanes[hd]], states[hd * nchunk + c], preferred_element_type=F32)
             for hd in range(GLA_HEADS)], axis=1))
    o = jnp.concatenate(outs, axis=0)
    normed = jnp.concatenate(
        [_rms(o[:, hd * GLA_VALUE_DIM:(hd + 1) * GLA_VALUE_DIM], og_ref[...])
         for hd in range(GLA_HEADS)], axis=1)
    gated = (normed * (r * jax.nn.sigmoid(r))).astype(BF16)
    o_ref[0] = x + jnp.dot(gated, wout_ref[...], preferred_element_type=F32)


def _gla_mixer(x, gain, w_in, w_a2, b_a, o_gain, w_out):
    batch, seq, _ = x.shape
    tm = min(GLA_TM, seq)
    in_width = 2 * GLA_KEY_WIDTH + 2 * GLA_VALUE_WIDTH + GLA_GATE_PAD
    pad = GLA_GATE_PAD - GLA_GATE_RANK
    w_in_p = jnp.pad(w_in, ((0, 0), (0, pad))).astype(BF16)
    w_a2_p = jnp.pad(w_a2, ((0, pad), (0, 0))).astype(BF16)
    span = min(GLA_TRI_SPAN, tm)
    t = jnp.arange(span)
    tri = ((t[:, None] // GLA_CHUNK == t[None, :] // GLA_CHUNK)
           & (t[:, None] >= t[None, :])).astype(BF16)
    tile = lambda b, i: (b, i, 0)
    const2 = lambda b, i: (0, 0)
    return pl.pallas_call(
        functools.partial(_gla_kernel, tm=tm),
        grid=(batch, seq // tm),
        in_specs=[
            pl.BlockSpec((1, tm, D_MODEL), tile),
            pl.BlockSpec((1, D_MODEL), const2),
            pl.BlockSpec((D_MODEL, in_width), const2),
            pl.BlockSpec((GLA_GATE_PAD, GLA_KEY_WIDTH), const2),
            pl.BlockSpec((1, GLA_KEY_WIDTH), const2),
            pl.BlockSpec((1, GLA_VALUE_DIM), const2),
            pl.BlockSpec((GLA_VALUE_WIDTH, D_MODEL), const2),
            pl.BlockSpec((span, span), const2),
        ],
        out_specs=pl.BlockSpec((1, tm, D_MODEL), tile),
        out_shape=jax.ShapeDtypeStruct((batch, seq, D_MODEL), F32),
        scratch_shapes=[pltpu.VMEM((GLA_HEADS, GLA_KEY_DIM, GLA_VALUE_DIM), F32)],
        compiler_params=pltpu.CompilerParams(
            dimension_semantics=("arbitrary", "arbitrary"), vmem_limit_bytes=VMEM_LIMIT),
        name="gla_mixer",
    )(x, gain.reshape(1, D_MODEL), w_in_p, w_a2_p, b_a.reshape(1, GLA_KEY_WIDTH),
      o_gain.reshape(1, GLA_VALUE_DIM), w_out.astype(BF16), tri)


def kernel(x, mix_norm_even, w_in_even, sb_q_gain, sb_k_gain, pool_w, pool_scale, w_out_even,
           mix_norm_odd, w_in_odd, gla_w_a2, gla_b_a, gla_o_gain, w_out_odd,
           ffn_norm, ffn_w_up, ffn_conv_w, ffn_conv_b, ffn_w_down):
    batch, seq, _ = x.shape
    depth = ffn_norm.shape[0]
    for layer in range(depth):
        i = layer // 2
        if layer % 2 == 0:
            q, k, v, xb = _even_in(x.reshape(batch * seq, D_MODEL), mix_norm_even[i],
                                   w_in_even[i], sb_q_gain[i], sb_k_gain[i])
            o_a = _attention(q, k, v, batch, seq)
            x = _even_out(xb, o_a, x, pool_w[i], pool_scale[i], w_out_even[i])
        else:
            x = _gla_mixer(x, mix_norm_odd[i], w_in_odd[i], gla_w_a2[i], gla_b_a[i],
                           gla_o_gain[i], w_out_odd[i])
        x = _ffn(x, ffn_norm[layer], ffn_w_up[layer], ffn_conv_w[layer],
                 ffn_conv_b[layer], ffn_w_down[layer])
    return x
```

```python
import functools

import jax
import jax.numpy as jnp
from jax import lax
from jax.experimental import pallas as pl
from jax.experimental.pallas import tpu as pltpu

F32 = jnp.float32
BF16 = jnp.bfloat16

D_MODEL = 1024
NORM_EPS = 1e-6
LOG2E = 1.4426950408889634
UNDERFLOW_LOG2 = 160.0

SB_HEADS = 8
SB_HEAD_DIM = 64
SB_WIDTH = SB_HEADS * SB_HEAD_DIM
POOL_WINDOWS = (2, 4, 8, 16)
POOL_GROUP_DIM = 128
POOL_WIDTH = 512
POOL_HALO = 16

GLA_HEADS = 4
GLA_KEY_WIDTH = 512
GLA_VALUE_WIDTH = 1024
GLA_KEY_DIM = 128
GLA_VALUE_DIM = 256
GLA_GATE_RANK = 16
GLA_TAU = 16.0
GLA_CHUNK = 64
GLA_GATE_PAD = 128

FFN_HIDDEN = 2816
FFN_CHUNK = 256
FFN_NCHUNK = FFN_HIDDEN // FFN_CHUNK

IN_TM = 1024
ATT_TQ = 512
ATT_TK = 128
ATT_PAIRS = 2
ATT_NB = 2
FFN_TM = 512
GLA_TM = 1024
GLA_TRI_SPAN = 256

VMEM_LIMIT = 56 * 1024 * 1024


def _rms(x, gain):
    ms = jnp.mean(x * x, axis=-1, keepdims=True)
    return x * lax.rsqrt(ms + NORM_EPS) * gain


def _log1pexp_neg_abs(z):
    return jnp.log(1.0 + jnp.exp(jnp.minimum(z, -z)))


def _split_bf16(x):
    hi = x.astype(BF16)
    lo = (x - hi.astype(F32)).astype(BF16)
    return hi, lo


def _even_in_kernel(x_ref, g_ref, w_ref, qg_ref, kg_ref, bd_ref,
                    q_ref, k_ref, v_ref, xb_ref):
    h = _rms(x_ref[...], g_ref[...]).astype(BF16)
    proj = jnp.dot(h, w_ref[...], preferred_element_type=F32)

    def head_norm(t, gain):
        ss = jnp.dot((t * t).astype(BF16), bd_ref[...], preferred_element_type=F32)
        return t * lax.rsqrt(ss * (1.0 / SB_HEAD_DIM) + NORM_EPS) * gain

    q = head_norm(proj[:, 0:SB_WIDTH], qg_ref[...])
    q_ref[...] = (q * (SB_HEAD_DIM ** -0.5 * LOG2E)).astype(BF16)
    k_ref[...] = head_norm(proj[:, SB_WIDTH:2 * SB_WIDTH], kg_ref[...]).astype(BF16)
    v_ref[...] = proj[:, 2 * SB_WIDTH:3 * SB_WIDTH].astype(BF16)
    xb_ref[...] = proj[:, 3 * SB_WIDTH:]


def _even_in(x2d, gain, w_in, q_gain, k_gain):
    n = x2d.shape[0]
    tm = min(IN_TM, n)
    head = jnp.arange(SB_WIDTH) // SB_HEAD_DIM
    blockdiag = (head[:, None] == head[None, :]).astype(BF16)
    width = w_in.shape[1]
    const = lambda i: (0, 0)
    tile = lambda i: (i, 0)
    return pl.pallas_call(
        _even_in_kernel,
        grid=(n // tm,),
        in_specs=[
            pl.BlockSpec((tm, D_MODEL), tile),
            pl.BlockSpec((1, D_MODEL), const),
            pl.BlockSpec((D_MODEL, width), const),
            pl.BlockSpec((1, SB_WIDTH), const),
            pl.BlockSpec((1, SB_WIDTH), const),
            pl.BlockSpec((SB_WIDTH, SB_WIDTH), const),
        ],
        out_specs=[
            pl.BlockSpec((tm, SB_WIDTH), tile),
            pl.BlockSpec((tm, SB_WIDTH), tile),
            pl.BlockSpec((tm, SB_WIDTH), tile),
            pl.BlockSpec((tm, POOL_WIDTH), tile),
        ],
        out_shape=[
            jax.ShapeDtypeStruct((n, SB_WIDTH), BF16),
            jax.ShapeDtypeStruct((n, SB_WIDTH), BF16),
            jax.ShapeDtypeStruct((n, SB_WIDTH), BF16),
            jax.ShapeDtypeStruct((n, POOL_WIDTH), F32),
        ],
        compiler_params=pltpu.CompilerParams(
            dimension_semantics=("arbitrary",), vmem_limit_bytes=VMEM_LIMIT),
        name="even_in",
    )(x2d, gain.reshape(1, D_MODEL), w_in.astype(BF16),
      jnp.tile(q_gain, SB_HEADS).reshape(1, SB_WIDTH),
      jnp.tile(k_gain, SB_HEADS).reshape(1, SB_WIDTH), blockdiag)


def _attn_kernel(q_ref, k_ref, v_ref, m2_ref, o_ref, za_ref, zb_ref, wa_ref, wb_ref,
                 c0_ref, c1_ref, acc_ref, *, tq, tk, nb):
    i = pl.program_id(2)
    npairs = q_ref.shape[2] // (2 * SB_HEAD_DIM)
    ratio = tq // tk
    ngroups = i * (ratio // nb)
    lane = lax.broadcasted_iota(jnp.int32, (tk, 2 * SB_HEAD_DIM), 1)
    first_head = lane < SB_HEAD_DIM
    zero = jnp.zeros((), BF16)

    def stack_heads(blk):
        return jnp.concatenate(
            [jnp.where(first_head, blk, zero), jnp.where(first_head, zero, blk)], axis=0)

    def stacked(ref, p, start, n):
        lanes = slice(p * 2 * SB_HEAD_DIM, (p + 1) * 2 * SB_HEAD_DIM)
        return jnp.concatenate(
            [stack_heads(ref[0, pl.ds(start + u * tk, tk), lanes]) for u in range(n)], axis=0)

    def scores(p, q_rows, start, n):
        return lax.dot_general(q_rows, stacked(k_ref, p, start, n), (((1,), (1,)), ((), ())),
                               preferred_element_type=F32)

    def attend(p, w, start, n):
        return jnp.dot(w, stacked(v_ref, p, start, n), preferred_element_type=F32)

    def group_start(g):
        return pl.multiple_of(jnp.maximum(i * ratio - (g + 1) * nb, 0) * tk, tk)

    def block_weights(z2, c0, c1, visible):
        ws, cs = [], []
        for hh, c in ((0, c0), (1, c1)):
            z = z2[:, hh * tk:(hh + 1) * tk]
            softplus = jnp.maximum(z, 0.0) + jnp.log2(1.0 + jnp.exp2(jnp.minimum(z, -z)))
            if visible is not None:
                softplus = jnp.where(visible, softplus, 0.0)
            hi, lo = _split_bf16(softplus)
            cum = jnp.dot(jnp.concatenate([hi, lo], axis=1), m2_ref[...],
                          preferred_element_type=F32)
            w = jnp.exp2(z + cum[:, :tk] + c)
            if visible is not None:
                w = jnp.where(visible, w, 0.0)
            ws.append(w.astype(BF16))
            cs.append(c + cum[:, tk:])
        return jnp.concatenate(ws, axis=1), cs[0], cs[1]

    def pair_query(p):
        return q_ref[0, :, p * 2 * SB_HEAD_DIM:(p + 1) * 2 * SB_HEAD_DIM]

    def diagonal(p, q):
        c0 = jnp.zeros((tq, tk), F32)
        c1 = jnp.zeros((tq, tk), F32)
        acc = jnp.zeros((tq, 2 * SB_HEAD_DIM), F32)
        for u in reversed(range(ratio)):
            r0 = u * tk
            start = pl.multiple_of(i * tq + r0, tk)
            row = lax.broadcasted_iota(jnp.int32, (tq - r0, tk), 0)
            col = lax.broadcasted_iota(jnp.int32, (tq - r0, tk), 1)
            w, p0, p1 = block_weights(scores(p, q[r0:, :], start, 1),
                                      c0[r0:, :], c1[r0:, :], col < row)
            pacc = acc[r0:, :] + attend(p, w, start, 1)
            if r0:
                c0 = jnp.concatenate([c0[:r0, :], p0], axis=0)
                c1 = jnp.concatenate([c1[:r0, :], p1], axis=0)
                acc = jnp.concatenate([acc[:r0, :], pacc], axis=0)
            else:
                c0, c1, acc = p0, p1, pacc
        return c0, c1, acc

    def spent(c0, c1):
        return jnp.maximum(jnp.max(c0), jnp.max(c1)) <= -UNDERFLOW_LOG2

    def stacked_all(ref, start):
        blk = ref[0, pl.ds(start, tk), :]
        head = lax.broadcasted_iota(jnp.int32, blk.shape, 1) // SB_HEAD_DIM
        return jnp.concatenate(
            [jnp.where(head == h, blk, zero) for h in range(2 * npairs)], axis=0)

    def near_short():
        q = q_ref[0]
        nheads = 2 * npairs
        c = [jnp.zeros((tq, tk), F32)] * nheads
        acc = jnp.zeros((tq, q.shape[1]), F32)

        def visit(r0, r1, start, visible, off_diagonal):
            z = lax.dot_general(q[r0:r1, :], stacked_all(k_ref, start),
                                (((1,), (1,)), ((), ())), preferred_element_type=F32)
            ws, cs = [], []
            for p in range(npairs):
                w, p0, p1 = block_weights(z[:, p * 2 * tk:(p + 1) * 2 * tk],
                                          c[2 * p][r0:r1, :], c[2 * p + 1][r0:r1, :], visible)
                ws.append(w)
                cs += [p0, p1]
            w = jnp.concatenate(ws, axis=1)
            if off_diagonal:
                w = jnp.where(i > 0, w, zero)
            return cs, acc[r0:r1, :] + jnp.dot(w, stacked_all(v_ref, start),
                                               preferred_element_type=F32)

        for u in reversed(range(ratio)):
            r0 = u * tk
            row = lax.broadcasted_iota(jnp.int32, (tq - r0, tk), 0)
            col = lax.broadcasted_iota(jnp.int32, (tq - r0, tk), 1)
            cs, pacc = visit(r0, tq, pl.multiple_of(i * tq + r0, tk), col < row, False)
            if r0:
                c = [jnp.concatenate([old[:r0, :], new], axis=0) for old, new in zip(c, cs)]
                acc = jnp.concatenate([acc[:r0, :], pacc], axis=0)
            else:
                c, acc = cs, pacc
        enough = [spent(c[2 * p][2 * tk:, :], c[2 * p + 1][2 * tk:, :]) for p in range(npairs)]
        for b in (1, 2):
            rows = (3 - b) * tk
            c, pacc = visit(0, rows, pl.multiple_of(jnp.maximum(i * ratio - b, 0) * tk, tk),
                            None, True)
            acc = jnp.concatenate([pacc, acc[rows:, :]], axis=0)
            enough = [jnp.logical_and(e, spent(c[2 * p][rows - tk:, :], c[2 * p + 1][rows - tk:, :]))
                      for p, e in enumerate(enough)]
        for p in range(npairs):
            acc_ref[p] = acc[:, p * 2 * SB_HEAD_DIM:(p + 1) * 2 * SB_HEAD_DIM]
        return [jnp.logical_or(i == 0, e) for e in enough]

    def near(p):
        q = pair_query(p)
        c0, c1, acc = diagonal(p, q)
        z_near = scores(p, q, group_start(0), nb)
        w_near = [None] * nb
        for u in reversed(range(nb)):
            w_near[u], c0, c1 = block_weights(z_near[:, u * 2 * tk:(u + 1) * 2 * tk],
                                              c0, c1, None)
        w_near = jnp.where(i > 0, jnp.concatenate(w_near, axis=1), zero)
        acc = acc + attend(p, w_near, group_start(0), nb)
        c0_ref[p] = c0
        c1_ref[p] = c1
        acc_ref[p] = acc

    def far(p):
        q = pair_query(p)

        def stage(g, z_in, z_out, w_in, w_out):
            acc_ref[p] += attend(p, w_in[...], group_start(g - 1), nb)
            z_out[...] = scores(p, q, group_start(g + 1), nb)
            for u in reversed(range(nb)):
                cols = slice(u * 2 * tk, (u + 1) * 2 * tk)
                w, c0, c1 = block_weights(z_in[:, cols], c0_ref[p], c1_ref[p], None)
                c0_ref[p] = c0
                c1_ref[p] = c1
                w_out[:, cols] = w

        def any_weight_left():
            return jnp.maximum(jnp.max(c0_ref[p]), jnp.max(c1_ref[p])) > -UNDERFLOW_LOG2

        farther = jnp.logical_and(ngroups > 1, any_weight_left())

        @pl.when(farther)
        def _():
            wb_ref[...] = jnp.zeros_like(wb_ref)
            za_ref[...] = scores(p, q, group_start(1), nb)

        def cond(state):
            g, alive = state
            return jnp.logical_and(g < ngroups, alive)

        def body(state):
            g, _ = state
            stage(g, za_ref, zb_ref, wb_ref, wa_ref)
            more = jnp.logical_and(g + 1 < ngroups, any_weight_left())

            @pl.when(more)
            def _():
                stage(g + 1, zb_ref, za_ref, wa_ref, wb_ref)

            return g + 1 + more.astype(jnp.int32), any_weight_left()

        last, _ = lax.while_loop(cond, body, (jnp.int32(1), farther))

        stages = last - 1

        @pl.when(stages % 2 == 1)
        def _():
            acc_ref[p] += attend(p, wa_ref[...], group_start(last - 1), nb)

        @pl.when(jnp.logical_and(stages > 0, stages % 2 == 0))
        def _():
            acc_ref[p] += attend(p, wb_ref[...], group_start(last - 1), nb)

        o_ref[0, :, p * 2 * SB_HEAD_DIM:(p + 1) * 2 * SB_HEAD_DIM] = acc_ref[p].astype(BF16)

    enough = near_short()
    for p in range(npairs):
        @pl.when(enough[p])
        def _(p=p):
            o_ref[0, :, p * 2 * SB_HEAD_DIM:(p + 1) * 2 * SB_HEAD_DIM] = acc_ref[p].astype(BF16)

        @pl.when(jnp.logical_not(enough[p]))
        def _(p=p):
            near(p)
            far(p)


def _attention(q, k, v, batch, seq):
    tq = min(ATT_TQ, seq)
    tk = min(ATT_TK, tq)
    q3 = q.reshape(batch, seq, SB_WIDTH)
    k3 = k.reshape(batch, seq, SB_WIDTH)
    v3 = v.reshape(batch, seq, SB_WIDTH)
    nb = ATT_NB
    assert seq % tq == 0 and (tq // tk) % nb == 0 and tq // tk >= 3
    r = jnp.arange(2 * tk) % tk
    ccol = jnp.arange(2 * tk)
    m2 = -jnp.where(ccol[None, :] < tk, r[:, None] >= ccol[None, :], True).astype(BF16)
    width = ATT_PAIRS * 2 * SB_HEAD_DIM
    return pl.pallas_call(
        functools.partial(_attn_kernel, tq=tq, tk=tk, nb=nb),
        grid=(batch, SB_WIDTH // width, seq // tq),
        scratch_shapes=[
            pltpu.VMEM((tq, nb * 2 * tk), F32), pltpu.VMEM((tq, nb * 2 * tk), F32),
            pltpu.VMEM((tq, nb * 2 * tk), BF16), pltpu.VMEM((tq, nb * 2 * tk), BF16),
            pltpu.VMEM((ATT_PAIRS, tq, tk), F32), pltpu.VMEM((ATT_PAIRS, tq, tk), F32),
            pltpu.VMEM((ATT_PAIRS, tq, 2 * SB_HEAD_DIM), F32),
        ],
        in_specs=[
            pl.BlockSpec((1, tq, width), lambda b, p, i: (b, i, p)),
            pl.BlockSpec((1, seq, width), lambda b, p, i: (b, 0, p)),
            pl.BlockSpec((1, seq, width), lambda b, p, i: (b, 0, p)),
            pl.BlockSpec((2 * tk, 2 * tk), lambda b, p, i: (0, 0)),
        ],
        out_specs=pl.BlockSpec((1, tq, width), lambda b, p, i: (b, i, p)),
        out_shape=jax.ShapeDtypeStruct((batch, seq, SB_WIDTH), BF16),
        compiler_params=pltpu.CompilerParams(
            dimension_semantics=("arbitrary", "arbitrary", "arbitrary"),
            vmem_limit_bytes=VMEM_LIMIT),
        name="sb_attention",
    )(q3, k3, v3, m2)


def _even_mixer_residual(i, tm, x_ref, xb_ref, halo_ref, oa_ref, pw_ref, ps_ref, wo_ref):
    xb = xb_ref[0]
    halo = jnp.where(i > 0, halo_ref[0], 0.0)
    ext = jnp.concatenate([halo, xb], axis=0)
    pos = (i * tm + 1 + lax.broadcasted_iota(jnp.int32, (tm, 1), 0)).astype(F32)
    y = jnp.dot(oa_ref[0], wo_ref[0:SB_WIDTH, :], preferred_element_type=F32)
    pooled_out = []
    for g, w in enumerate(POOL_WINDOWS):
        lanes = slice(g * POOL_GROUP_DIM, (g + 1) * POOL_GROUP_DIM)
        s = ext[:, lanes]
        sh = 1
        while sh < w:
            s = s + pltpu.roll(s, sh, axis=0)
            sh *= 2
        window_sum = s[POOL_HALO:, :]
        pooled = window_sum / jnp.minimum(pos, float(w)) - xb[:, lanes]
        ob = jnp.dot(pooled.astype(BF16), pw_ref[g], preferred_element_type=F32)
        pooled_out.append((ob * ps_ref[:, lanes]).astype(BF16))
    y = y + jnp.dot(jnp.concatenate(pooled_out, axis=1), wo_ref[SB_WIDTH:, :],
                    preferred_element_type=F32)
    return x_ref[0] + y


def _ffn_kernel(*refs, tm, with_even_mixer):
    n_mixer = 6 if with_even_mixer else 0
    x_ref, mixer_refs = refs[0], refs[1:1 + n_mixer]
    g_ref, wup_ref, cw_ref, wd_ref, o_ref, h_ref, carry_ref, u_ref, act_ref = refs[1 + n_mixer:]
    i = pl.program_id(1)
    nch = FFN_NCHUNK

    @pl.when(i == 0)
    def _():
        carry_ref[...] = jnp.zeros_like(carry_ref)

    if with_even_mixer:
        x = _even_mixer_residual(i, tm, x_ref, *mixer_refs)
    else:
        x = x_ref[0]
    h_ref[...] = _rms(x, g_ref[...]).astype(BF16)

    def columns(c):
        return (slice(c * FFN_CHUNK, (c + 1) * FFN_CHUNK),
                slice(FFN_HIDDEN + c * FFN_CHUNK, FFN_HIDDEN + (c + 1) * FFN_CHUNK))

    def up(c):
        h = h_ref[...]
        u = jnp.concatenate([jnp.dot(h, wup_ref[:, cols], preferred_element_type=F32)
                             for cols in columns(c)], axis=1)
        u_ref[0:8, :] = carry_ref[c]
        u_ref[8:8 + tm, :] = u
        carry_ref[c] = u[tm - 8:, :]

    def activate(c):
        cw = jnp.concatenate([cw_ref[:, cols] for cols in columns(c)], axis=1)
        conv = (cw[3:4, :] + cw[2:3, :] * u_ref[8:8 + tm, :]
                + cw[1:2, :] * u_ref[7:7 + tm, :]
                + cw[0:1, :] * u_ref[6:6 + tm, :])
        a = conv[:, :FFN_CHUNK]
        gate = conv[:, FFN_CHUNK:]
        act_ref[:, columns(c)[0]] = (a * jax.nn.sigmoid(a) * gate).astype(BF16)

    for c in range(nch):
        up(c)
        activate(c)
    o_ref[0] = x + jnp.dot(act_ref[...], wd_ref[...], preferred_element_type=F32)


def _ffn(x, gain, w_up, conv_w, conv_b, w_down, even_mixer=None):
    batch, seq, _ = x.shape
    tm = min(FFN_TM, seq)
    ck, nch = FFN_CHUNK, FFN_NCHUNK

    taps = jnp.concatenate([conv_w, conv_b[None, :],
                            jnp.zeros((4, 2 * FFN_HIDDEN), F32)], axis=0)
    tile = lambda b, i: (b, i, 0)
    const2 = lambda b, i: (0, 0)
    const3 = lambda b, i: (0, 0, 0)
    resident = dict(pipeline_mode=pl.Buffered(1))
    mixer_specs, mixer_args = [], []
    if even_mixer is not None:
        xb, o_a, pool_w, pool_scale, w_out = even_mixer
        xb3 = xb.reshape(batch, seq, POOL_WIDTH)
        hb = tm // POOL_HALO
        mixer_specs = [
            pl.BlockSpec((1, tm, POOL_WIDTH), tile),
            pl.BlockSpec((1, POOL_HALO, POOL_WIDTH),
                         lambda b, i: (b, jnp.maximum(i * hb - 1, 0), 0)),
            pl.BlockSpec((1, tm, SB_WIDTH), tile),
            pl.BlockSpec((len(POOL_WINDOWS), POOL_GROUP_DIM, POOL_GROUP_DIM), const3),
            pl.BlockSpec((1, POOL_WIDTH), const2),
            pl.BlockSpec((D_MODEL, D_MODEL), const2, **resident),
        ]
        mixer_args = [xb3, xb3, o_a, pool_w.astype(BF16), pool_scale.reshape(1, POOL_WIDTH),
                      w_out.astype(BF16)]
    return pl.pallas_call(
        functools.partial(_ffn_kernel, tm=tm, with_even_mixer=even_mixer is not None),
        grid=(batch, seq // tm),
        in_specs=[pl.BlockSpec((1, tm, D_MODEL), tile)] + mixer_specs + [
            pl.BlockSpec((1, D_MODEL), const2),
            pl.BlockSpec((D_MODEL, 2 * FFN_HIDDEN), const2, **resident),
            pl.BlockSpec((8, 2 * FFN_HIDDEN), const2),
            pl.BlockSpec((FFN_HIDDEN, D_MODEL), const2, **resident),
        ],
        out_specs=pl.BlockSpec((1, tm, D_MODEL), tile),
        out_shape=jax.ShapeDtypeStruct((batch, seq, D_MODEL), F32),
        scratch_shapes=[
            pltpu.VMEM((tm, D_MODEL), BF16),
            pltpu.VMEM((nch, 8, 2 * ck), F32),
            pltpu.VMEM((tm + 8, 2 * ck), F32),
            pltpu.VMEM((tm, FFN_HIDDEN), BF16),
        ],
        compiler_params=pltpu.CompilerParams(
            dimension_semantics=("arbitrary", "arbitrary"), vmem_limit_bytes=VMEM_LIMIT),
        name="conv_ffn",
    )(x, *mixer_args, gain.reshape(1, D_MODEL), w_up.astype(BF16), taps, w_down.astype(BF16))


def _gla_kernel(x_ref, g_ref, win_ref, wa2_ref, ba_ref, og_ref, wout_ref, tri_ref,
                o_ref, state_ref, *, tm):
    i = pl.program_id(1)

    @pl.when(i == 0)
    def _():
        state_ref[...] = jnp.zeros_like(state_ref)

    x = x_ref[0]
    h = _rms(x, g_ref[...]).astype(BF16)
    proj = jnp.dot(h, win_ref[...], preferred_element_type=F32)
    kw, vw = GLA_KEY_WIDTH, GLA_VALUE_WIDTH
    q = proj[:, 0:kw] * (GLA_KEY_DIM ** -0.5)
    k = proj[:, kw:2 * kw]
    v = proj[:, 2 * kw:2 * kw + vw]
    r = proj[:, 2 * kw + vw:2 * kw + 2 * vw]
    a_low = proj[:, 2 * kw + 2 * vw:]
    gate = jnp.dot(a_low.astype(BF16), wa2_ref[...], preferred_element_type=F32) + ba_ref[...]
    log_alpha = (jnp.minimum(gate, 0.0) - _log1pexp_neg_abs(gate)) * (1.0 / GLA_TAU)
    hi, lo = _split_bf16(log_alpha)
    tri = tri_ref[...]
    span = tri.shape[0]
    cum = jnp.concatenate(
        [jnp.dot(tri, hi[r0:r0 + span, :], preferred_element_type=F32)
         + jnp.dot(tri, lo[r0:r0 + span, :], preferred_element_type=F32)
         for r0 in range(0, tm, span)], axis=0)

    nchunk = tm // GLA_CHUNK
    klanes = [slice(hd * GLA_KEY_DIM, (hd + 1) * GLA_KEY_DIM) for hd in range(GLA_HEADS)]
    vlanes = [slice(hd * GLA_VALUE_DIM, (hd + 1) * GLA_VALUE_DIM) for hd in range(GLA_HEADS)]
    updates, decays = [], []
    for c in range(nchunk):
        rows = slice(c * GLA_CHUNK, (c + 1) * GLA_CHUNK)
        cum_c = cum[rows, :]
        total = cum_c[GLA_CHUNK - 1:GLA_CHUNK, :]
        k_dec = k[rows, :] * jnp.exp(total - cum_c)
        decay = jnp.broadcast_to(jnp.exp(total), (GLA_CHUNK, kw))
        kd = jnp.concatenate([k_dec, decay], axis=0)
        v_c = v[rows, :].astype(BF16)
        for hd in range(GLA_HEADS):
            kd_t = kd[:, klanes[hd]].T
            decays.append(kd_t[:, GLA_CHUNK:GLA_CHUNK + 1])
            updates.append(jnp.dot(kd_t[:, :GLA_CHUNK].astype(BF16), v_c[:, vlanes[hd]],
                                   preferred_element_type=F32))
    states = []
    for hd in range(GLA_HEADS):
        st = state_ref[hd]
        for c in range(nchunk):
            st = st * decays[c * GLA_HEADS + hd] + updates[c * GLA_HEADS + hd]
            states.append(st.astype(BF16))
        state_ref[hd] = st
    outs = []
    for c in range(nchunk):
        q_c = q[c * GLA_CHUNK:(c + 1) * GLA_CHUNK, :].astype(BF16)
        outs.append(jnp.concatenate(
            [jnp.dot(q_c[:, klanes[hd]], states[hd * nchunk + c], preferred_element_type=F32)
             for hd in range(GLA_HEADS)], axis=1))
    o = jnp.concatenate(outs, axis=0)
    normed = jnp.concatenate(
        [_rms(o[:, hd * GLA_VALUE_DIM:(hd + 1) * GLA_VALUE_DIM], og_ref[...])
         for hd in range(GLA_HEADS)], axis=1)
    gated = (normed * (r * jax.nn.sigmoid(r))).astype(BF16)
    o_ref[0] = x + jnp.dot(gated, wout_ref[...], preferred_element_type=F32)


def _gla_mixer(x, gain, w_in, w_a2, b_a, o_gain, w_out):
    batch, seq, _ = x.shape
    tm = min(GLA_TM, seq)
    in_width = 2 * GLA_KEY_WIDTH + 2 * GLA_VALUE_WIDTH + GLA_GATE_PAD
    pad = GLA_GATE_PAD - GLA_GATE_RANK
    w_in_p = jnp.pad(w_in, ((0, 0), (0, pad))).astype(BF16)
    w_a2_p = jnp.pad(w_a2, ((0, pad), (0, 0))).astype(BF16)
    span = min(GLA_TRI_SPAN, tm)
    t = jnp.arange(span)
    tri = ((t[:, None] // GLA_CHUNK == t[None, :] // GLA_CHUNK)
           & (t[:, None] >= t[None, :])).astype(BF16)
    tile = lambda b, i: (b, i, 0)
    const2 = lambda b, i: (0, 0)
    return pl.pallas_call(
        functools.partial(_gla_kernel, tm=tm),
        grid=(batch, seq // tm),
        in_specs=[
            pl.BlockSpec((1, tm, D_MODEL), tile),
            pl.BlockSpec((1, D_MODEL), const2),
            pl.BlockSpec((D_MODEL, in_width), const2),
            pl.BlockSpec((GLA_GATE_PAD, GLA_KEY_WIDTH), const2),
            pl.BlockSpec((1, GLA_KEY_WIDTH), const2),
            pl.BlockSpec((1, GLA_VALUE_DIM), const2),
            pl.BlockSpec((GLA_VALUE_WIDTH, D_MODEL), const2),
            pl.BlockSpec((span, span), const2),
        ],
        out_specs=pl.BlockSpec((1, tm, D_MODEL), tile),
        out_shape=jax.ShapeDtypeStruct((batch, seq, D_MODEL), F32),
        scratch_shapes=[pltpu.VMEM((GLA_HEADS, GLA_KEY_DIM, GLA_VALUE_DIM), F32)],
        compiler_params=pltpu.CompilerParams(
            dimension_semantics=("arbitrary", "arbitrary"), vmem_limit_bytes=VMEM_LIMIT),
        name="gla_mixer",
    )(x, gain.reshape(1, D_MODEL), w_in_p, w_a2_p, b_a.reshape(1, GLA_KEY_WIDTH),
      o_gain.reshape(1, GLA_VALUE_DIM), w_out.astype(BF16), tri)


def kernel(x, mix_norm_even, w_in_even, sb_q_gain, sb_k_gain, pool_w, pool_scale, w_out_even,
           mix_norm_odd, w_in_odd, gla_w_a2, gla_b_a, gla_o_gain, w_out_odd,
           ffn_norm, ffn_w_up, ffn_conv_w, ffn_conv_b, ffn_w_down):
    batch, seq, _ = x.shape
    depth = ffn_norm.shape[0]
    for layer in range(depth):
        i = layer // 2
        if layer % 2 == 0:
            q, k, v, xb = _even_in(x.reshape(batch * seq, D_MODEL), mix_norm_even[i],
                                   w_in_even[i], sb_q_gain[i], sb_k_gain[i])
            o_a = _attention(q, k, v, batch, seq)
            even_mixer = (xb, o_a, pool_w[i], pool_scale[i], w_out_even[i])
        else:
            x = _gla_mixer(x, mix_norm_odd[i], w_in_odd[i], gla_w_a2[i], gla_b_a[i],
                           gla_o_gain[i], w_out_odd[i])
            even_mixer = None
        x = _ffn(x, ffn_norm[layer], ffn_w_up[layer], ffn_conv_w[layer],
                 ffn_conv_b[layer], ffn_w_down[layer], even_mixer)
    return x
```

```python
import functools

import jax
import jax.numpy as jnp
from jax import lax
from jax.experimental import pallas as pl
from jax.experimental.pallas import tpu as pltpu

F32 = jnp.float32
BF16 = jnp.bfloat16

D_MODEL = 1024
NORM_EPS = 1e-6
LOG2E = 1.4426950408889634
UNDERFLOW_LOG2 = 160.0

SB_HEADS = 8
SB_HEAD_DIM = 64
SB_WIDTH = SB_HEADS * SB_HEAD_DIM
POOL_WINDOWS = (2, 4, 8, 16)
POOL_GROUP_DIM = 128
POOL_WIDTH = 512
POOL_HALO = 16

GLA_HEADS = 4
GLA_KEY_WIDTH = 512
GLA_VALUE_WIDTH = 1024
GLA_KEY_DIM = 128
GLA_VALUE_DIM = 256
GLA_GATE_RANK = 16
GLA_TAU = 16.0
GLA_CHUNK = 64
GLA_GATE_PAD = 128

FFN_HIDDEN = 2816
FFN_CHUNK = 256
FFN_NCHUNK = FFN_HIDDEN // FFN_CHUNK

IN_TM = 1024
ATT_TQ = 512
ATT_TK = 128
ATT_PAIRS = 2
ATT_NB = 2
FFN_TM = 512
GLA_TM = 1024
GLA_TRI_SPAN = 256

VMEM_LIMIT = 56 * 1024 * 1024


def _rms(x, gain):
    ms = jnp.mean(x * x, axis=-1, keepdims=True)
    return x * lax.rsqrt(ms + NORM_EPS) * gain


def _log1pexp_neg_abs(z):
    return jnp.log(1.0 + jnp.exp(jnp.minimum(z, -z)))


def _split_bf16(x):
    hi = x.astype(BF16)
    lo = (x - hi.astype(F32)).astype(BF16)
    return hi, lo


def _even_in_kernel(x_ref, g_ref, w_ref, qg_ref, kg_ref, bd_ref,
                    q_ref, k_ref, v_ref, xb_ref):
    h = _rms(x_ref[...], g_ref[...]).astype(BF16)
    proj = jnp.dot(h, w_ref[...], preferred_element_type=F32)

    def head_norm(t, gain):
        ss = jnp.dot((t * t).astype(BF16), bd_ref[...], preferred_element_type=F32)
        return t * lax.rsqrt(ss * (1.0 / SB_HEAD_DIM) + NORM_EPS) * gain

    q = head_norm(proj[:, 0:SB_WIDTH], qg_ref[...])
    q_ref[...] = (q * (SB_HEAD_DIM ** -0.5 * LOG2E)).astype(BF16)
    k_ref[...] = head_norm(proj[:, SB_WIDTH:2 * SB_WIDTH], kg_ref[...]).astype(BF16)
    v_ref[...] = proj[:, 2 * SB_WIDTH:3 * SB_WIDTH].astype(BF16)
    xb_ref[...] = proj[:, 3 * SB_WIDTH:]


def _even_in(x2d, gain, w_in, q_gain, k_gain):
    n = x2d.shape[0]
    tm = min(IN_TM, n)
    head = jnp.arange(SB_WIDTH) // SB_HEAD_DIM
    blockdiag = (head[:, None] == head[None, :]).astype(BF16)
    width = w_in.shape[1]
    const = lambda i: (0, 0)
    tile = lambda i: (i, 0)
    return pl.pallas_call(
        _even_in_kernel,
        grid=(n // tm,),
        in_specs=[
            pl.BlockSpec((tm, D_MODEL), tile),
            pl.BlockSpec((1, D_MODEL), const),
            pl.BlockSpec((D_MODEL, width), const),
            pl.BlockSpec((1, SB_WIDTH), const),
            pl.BlockSpec((1, SB_WIDTH), const),
            pl.BlockSpec((SB_WIDTH, SB_WIDTH), const),
        ],
        out_specs=[
            pl.BlockSpec((tm, SB_WIDTH), tile),
            pl.BlockSpec((tm, SB_WIDTH), tile),
            pl.BlockSpec((tm, SB_WIDTH), tile),
            pl.BlockSpec((tm, POOL_WIDTH), tile),
        ],
        out_shape=[
            jax.ShapeDtypeStruct((n, SB_WIDTH), BF16),
            jax.ShapeDtypeStruct((n, SB_WIDTH), BF16),
            jax.ShapeDtypeStruct((n, SB_WIDTH), BF16),
            jax.ShapeDtypeStruct((n, POOL_WIDTH), F32),
        ],
        compiler_params=pltpu.CompilerParams(
            dimension_semantics=("arbitrary",), vmem_limit_bytes=VMEM_LIMIT),
        name="even_in",
    )(x2d, gain.reshape(1, D_MODEL), w_in.astype(BF16),
      jnp.tile(q_gain, SB_HEADS).reshape(1, SB_WIDTH),
      jnp.tile(k_gain, SB_HEADS).reshape(1, SB_WIDTH), blockdiag)


def _attn_kernel(q_ref, k_ref, v_ref, m2_ref, o_ref, za_ref, zb_ref, wa_ref, wb_ref,
                 c0_ref, c1_ref, acc_ref, *, tq, tk, nb):
    i = pl.program_id(2)
    npairs = q_ref.shape[2] // (2 * SB_HEAD_DIM)
    ratio = tq // tk
    ngroups = i * (ratio // nb)
    lane = lax.broadcasted_iota(jnp.int32, (tk, 2 * SB_HEAD_DIM), 1)
    first_head = lane < SB_HEAD_DIM
    zero = jnp.zeros((), BF16)

    def stack_heads(blk):
        return jnp.concatenate(
            [jnp.where(first_head, blk, zero), jnp.where(first_head, zero, blk)], axis=0)

    def stacked(ref, p, start, n):
        lanes = slice(p * 2 * SB_HEAD_DIM, (p + 1) * 2 * SB_HEAD_DIM)
        return jnp.concatenate(
            [stack_heads(ref[0, pl.ds(start + u * tk, tk), lanes]) for u in range(n)], axis=0)

    def scores(p, q_rows, start, n):
        return lax.dot_general(q_rows, stacked(k_ref, p, start, n), (((1,), (1,)), ((), ())),
                               preferred_element_type=F32)

    def attend(p, w, start, n):
        return jnp.dot(w, stacked(v_ref, p, start, n), preferred_element_type=F32)

    def group_start(g):
        return pl.multiple_of(jnp.maximum(i * ratio - (g + 1) * nb, 0) * tk, tk)

    def block_weights(z2, c0, c1, visible):
        ws, cs = [], []
        for hh, c in ((0, c0), (1, c1)):
            z = z2[:, hh * tk:(hh + 1) * tk]
            softplus = jnp.maximum(z, 0.0) + jnp.log2(1.0 + jnp.exp2(jnp.minimum(z, -z)))
            if visible is not None:
                softplus = jnp.where(visible, softplus, 0.0)
            hi, lo = _split_bf16(softplus)
            cum = jnp.dot(jnp.concatenate([hi, lo], axis=1), m2_ref[...],
                          preferred_element_type=F32)
            w = jnp.exp2(z + cum[:, :tk] + c)
            if visible is not None:
                w = jnp.where(visible, w, 0.0)
            ws.append(w.astype(BF16))
            cs.append(c + cum[:, tk:])
        return jnp.concatenate(ws, axis=1), cs[0], cs[1]

    def pair_query(p):
        return q_ref[0, :, p * 2 * SB_HEAD_DIM:(p + 1) * 2 * SB_HEAD_DIM]

    def diagonal(p, q):
        c0 = jnp.zeros((tq, tk), F32)
        c1 = jnp.zeros((tq, tk), F32)
        acc = jnp.zeros((tq, 2 * SB_HEAD_DIM), F32)
        for u in reversed(range(ratio)):
            r0 = u * tk
            start = pl.multiple_of(i * tq + r0, tk)
            row = lax.broadcasted_iota(jnp.int32, (tq - r0, tk), 0)
            col = lax.broadcasted_iota(jnp.int32, (tq - r0, tk), 1)
            w, p0, p1 = block_weights(scores(p, q[r0:, :], start, 1),
                                      c0[r0:, :], c1[r0:, :], col < row)
            pacc = acc[r0:, :] + attend(p, w, start, 1)
            if r0:
                c0 = jnp.concatenate([c0[:r0, :], p0], axis=0)
                c1 = jnp.concatenate([c1[:r0, :], p1], axis=0)
                acc = jnp.concatenate([acc[:r0, :], pacc], axis=0)
            else:
                c0, c1, acc = p0, p1, pacc
        return c0, c1, acc

    def spent(c0, c1):
        return jnp.maximum(jnp.max(c0), jnp.max(c1)) <= -UNDERFLOW_LOG2

    def stacked_all(ref, start):
        blk = ref[0, pl.ds(start, tk), :]
        head = lax.broadcasted_iota(jnp.int32, blk.shape, 1) // SB_HEAD_DIM
        return jnp.concatenate(
            [jnp.where(head == h, blk, zero) for h in range(2 * npairs)], axis=0)

    def near_short():
        q = q_ref[0]
        nheads = 2 * npairs
        c = [jnp.zeros((tq, tk), F32)] * nheads
        acc = jnp.zeros((tq, q.shape[1]), F32)

        def visit(r0, r1, start, visible, off_diagonal):
            z = lax.dot_general(q[r0:r1, :], stacked_all(k_ref, start),
                                (((1,), (1,)), ((), ())), preferred_element_type=F32)
            ws, cs = [], []
            for p in range(npairs):
                w, p0, p1 = block_weights(z[:, p * 2 * tk:(p + 1) * 2 * tk],
                                          c[2 * p][r0:r1, :], c[2 * p + 1][r0:r1, :], visible)
                ws.append(w)
                cs += [p0, p1]
            w = jnp.concatenate(ws, axis=1)
            if off_diagonal:
                w = jnp.where(i > 0, w, zero)
            return cs, acc[r0:r1, :] + jnp.dot(w, stacked_all(v_ref, start),
                                               preferred_element_type=F32)

        for u in reversed(range(ratio)):
            r0 = u * tk
            row = lax.broadcasted_iota(jnp.int32, (tq - r0, tk), 0)
            col = lax.broadcasted_iota(jnp.int32, (tq - r0, tk), 1)
            cs, pacc = visit(r0, tq, pl.multiple_of(i * tq + r0, tk), col < row, False)
            if r0:
                c = [jnp.concatenate([old[:r0, :], new], axis=0) for old, new in zip(c, cs)]
                acc = jnp.concatenate([acc[:r0, :], pacc], axis=0)
            else:
                c, acc = cs, pacc
        enough = [spent(c[2 * p][2 * tk:, :], c[2 * p + 1][2 * tk:, :]) for p in range(npairs)]
        for b in (1, 2):
            rows = (3 - b) * tk
            c, pacc = visit(0, rows, pl.multiple_of(jnp.maximum(i * ratio - b, 0) * tk, tk),
                            None, True)
            acc = jnp.concatenate([pacc, acc[rows:, :]], axis=0)
            enough = [jnp.logical_and(e, spent(c[2 * p][rows - tk:, :], c[2 * p + 1][rows - tk:, :]))
                      for p, e in enumerate(enough)]
        for p in range(npairs):
            acc_ref[p] = acc[:, p * 2 * SB_HEAD_DIM:(p + 1) * 2 * SB_HEAD_DIM]
        return [jnp.logical_or(i == 0, e) for e in enough]

    def near(p):
        q = pair_query(p)
        c0, c1, acc = diagonal(p, q)
        z_near = scores(p, q, group_start(0), nb)
        w_near = [None] * nb
        for u in reversed(range(nb)):
            w_near[u], c0, c1 = block_weights(z_near[:, u * 2 * tk:(u + 1) * 2 * tk],
                                              c0, c1, None)
        w_near = jnp.where(i > 0, jnp.concatenate(w_near, axis=1), zero)
        acc = acc + attend(p, w_near, group_start(0), nb)
        c0_ref[p] = c0
        c1_ref[p] = c1
        acc_ref[p] = acc

    def far(p):
        q = pair_query(p)

        def stage(g, z_in, z_out, w_in, w_out):
            acc_ref[p] += attend(p, w_in[...], group_start(g - 1), nb)
            z_out[...] = scores(p, q, group_start(g + 1), nb)
            for u in reversed(range(nb)):
                cols = slice(u * 2 * tk, (u + 1) * 2 * tk)
                w, c0, c1 = block_weights(z_in[:, cols], c0_ref[p], c1_ref[p], None)
                c0_ref[p] = c0
                c1_ref[p] = c1
                w_out[:, cols] = w

        def any_weight_left():
            return jnp.maximum(jnp.max(c0_ref[p]), jnp.max(c1_ref[p])) > -UNDERFLOW_LOG2

        farther = jnp.logical_and(ngroups > 1, any_weight_left())

        @pl.when(farther)
        def _():
            wb_ref[...] = jnp.zeros_like(wb_ref)
            za_ref[...] = scores(p, q, group_start(1), nb)

        def cond(state):
            g, alive = state
            return jnp.logical_and(g < ngroups, alive)

        def body(state):
            g, _ = state
            stage(g, za_ref, zb_ref, wb_ref, wa_ref)
            more = jnp.logical_and(g + 1 < ngroups, any_weight_left())

            @pl.when(more)
            def _():
                stage(g + 1, zb_ref, za_ref, wa_ref, wb_ref)

            return g + 1 + more.astype(jnp.int32), any_weight_left()

        last, _ = lax.while_loop(cond, body, (jnp.int32(1), farther))

        stages = last - 1

        @pl.when(stages % 2 == 1)
        def _():
            acc_ref[p] += attend(p, wa_ref[...], group_start(last - 1), nb)

        @pl.when(jnp.logical_and(stages > 0, stages % 2 == 0))
        def _():
            acc_ref[p] += attend(p, wb_ref[...], group_start(last - 1), nb)

        o_ref[0, :, p * 2 * SB_HEAD_DIM:(p + 1) * 2 * SB_HEAD_DIM] = acc_ref[p].astype(BF16)

    enough = near_short()
    for p in range(npairs):
        @pl.when(enough[p])
        def _(p=p):
            o_ref[0, :, p * 2 * SB_HEAD_DIM:(p + 1) * 2 * SB_HEAD_DIM] = acc_ref[p].astype(BF16)

        @pl.when(jnp.logical_not(enough[p]))
        def _(p=p):
            near(p)
            far(p)


def _attention(q, k, v, batch, seq):
    tq = min(ATT_TQ, seq)
    tk = min(ATT_TK, tq)
    q3 = q.reshape(batch, seq, SB_WIDTH)
    k3 = k.reshape(batch, seq, SB_WIDTH)
    v3 = v.reshape(batch, seq, SB_WIDTH)
    nb = ATT_NB
    assert seq % tq == 0 and (tq // tk) % nb == 0 and tq // tk >= 3
    r = jnp.arange(2 * tk) % tk
    ccol = jnp.arange(2 * tk)
    m2 = -jnp.where(ccol[None, :] < tk, r[:, None] >= ccol[None, :], True).astype(BF16)
    width = ATT_PAIRS * 2 * SB_HEAD_DIM
    return pl.pallas_call(
        functools.partial(_attn_kernel, tq=tq, tk=tk, nb=nb),
        grid=(batch, SB_WIDTH // width, seq // tq),
        scratch_shapes=[
            pltpu.VMEM((tq, nb * 2 * tk), F32), pltpu.VMEM((tq, nb * 2 * tk), F32),
            pltpu.VMEM((tq, nb * 2 * tk), BF16), pltpu.VMEM((tq, nb * 2 * tk), BF16),
            pltpu.VMEM((ATT_PAIRS, tq, tk), F32), pltpu.VMEM((ATT_PAIRS, tq, tk), F32),
            pltpu.VMEM((ATT_PAIRS, tq, 2 * SB_HEAD_DIM), F32),
        ],
        in_specs=[
            pl.BlockSpec((1, tq, width), lambda b, p, i: (b, i, p)),
            pl.BlockSpec((1, seq, width), lambda b, p, i: (b, 0, p)),
            pl.BlockSpec((1, seq, width), lambda b, p, i: (b, 0, p)),
            pl.BlockSpec((2 * tk, 2 * tk), lambda b, p, i: (0, 0)),
        ],
        out_specs=pl.BlockSpec((1, tq, width), lambda b, p, i: (b, i, p)),
        out_shape=jax.ShapeDtypeStruct((batch, seq, SB_WIDTH), BF16),
        compiler_params=pltpu.CompilerParams(
            dimension_semantics=("arbitrary", "arbitrary", "arbitrary"),
            vmem_limit_bytes=VMEM_LIMIT),
        name="sb_attention",
    )(q3, k3, v3, m2)


def _even_mixer_residual(i, tm, x_ref, xb_ref, halo_ref, oa_ref, pw_ref, ps_ref, wo_ref):
    xb = xb_ref[0]
    halo = jnp.where(i > 0, halo_ref[0], 0.0)
    ext = jnp.concatenate([halo, xb], axis=0)
    pos = (i * tm + 1 + lax.broadcasted_iota(jnp.int32, (tm, 1), 0)).astype(F32)
    y = jnp.dot(oa_ref[0], wo_ref[0:SB_WIDTH, :], preferred_element_type=F32)
    pooled_out = []
    for g, w in enumerate(POOL_WINDOWS):
        lanes = slice(g * POOL_GROUP_DIM, (g + 1) * POOL_GROUP_DIM)
        s = ext[:, lanes]
        sh = 1
        while sh < w:
            s = s + pltpu.roll(s, sh, axis=0)
            sh *= 2
        window_sum = s[POOL_HALO:, :]
        pooled = window_sum / jnp.minimum(pos, float(w)) - xb[:, lanes]
        ob = jnp.dot(pooled.astype(BF16), pw_ref[g], preferred_element_type=F32)
        pooled_out.append((ob * ps_ref[:, lanes]).astype(BF16))
    y = y + jnp.dot(jnp.concatenate(pooled_out, axis=1), wo_ref[SB_WIDTH:, :],
                    preferred_element_type=F32)
    return x_ref[0] + y


def _ffn_kernel(*refs, tm, with_even_mixer):
    n_mixer = 6 if with_even_mixer else 0
    x_ref, mixer_refs = refs[0], refs[1:1 + n_mixer]
    g_ref, wup_ref, cw_ref, wd_ref, o_ref, h_ref, carry_ref, u_ref, act_ref = refs[1 + n_mixer:]
    i = pl.program_id(1)
    nch = FFN_NCHUNK

    @pl.when(i == 0)
    def _():
        carry_ref[...] = jnp.zeros_like(carry_ref)

    if with_even_mixer:
        x = _even_mixer_residual(i, tm, x_ref, *mixer_refs)
    else:
        x = x_ref[0]
    h_ref[...] = _rms(x, g_ref[...]).astype(BF16)

    def columns(c):
        return (slice(c * FFN_CHUNK, (c + 1) * FFN_CHUNK),
                slice(FFN_HIDDEN + c * FFN_CHUNK, FFN_HIDDEN + (c + 1) * FFN_CHUNK))

    def up(c):
        h = h_ref[...]
        u = jnp.concatenate([jnp.dot(h, wup_ref[:, cols], preferred_element_type=F32)
                             for cols in columns(c)], axis=1)
        u_ref[0:8, :] = carry_ref[c]
        u_ref[8:8 + tm, :] = u
        carry_ref[c] = u[tm - 8:, :]

    def activate(c):
        cw = jnp.concatenate([cw_ref[:, cols] for cols in columns(c)], axis=1)
        conv = (cw[3:4, :] + cw[2:3, :] * u_ref[8:8 + tm, :]
                + cw[1:2, :] * u_ref[7:7 + tm, :]
                + cw[0:1, :] * u_ref[6:6 + tm, :])
        a = conv[:, :FFN_CHUNK]
        gate = conv[:, FFN_CHUNK:]
        act_ref[:, columns(c)[0]] = (a * jax.nn.sigmoid(a) * gate).astype(BF16)

    for c in range(nch):
        up(c)
        activate(c)
    o_ref[0] = x + jnp.dot(act_ref[...], wd_ref[...], preferred_element_type=F32)


def _ffn_weights(w_up, conv_w, conv_b, w_down):
    depth = w_up.shape[0]
    taps = jnp.concatenate([conv_w, conv_b[:, None, :],
                            jnp.zeros((depth, 4, 2 * FFN_HIDDEN), F32)], axis=1)
    return w_up.astype(BF16), taps, w_down.astype(BF16)


def _ffn(x, gain, weights, layer, even_mixer=None):
    batch, seq, _ = x.shape
    tm = min(FFN_TM, seq)
    ck, nch = FFN_CHUNK, FFN_NCHUNK
    w_up, taps, w_down = weights
    tile = lambda b, i: (b, i, 0)
    const2 = lambda b, i: (0, 0)
    const3 = lambda b, i: (0, 0, 0)
    this_layer = lambda b, i: (layer, 0, 0)
    resident = dict(pipeline_mode=pl.Buffered(1))
    mixer_specs, mixer_args = [], []
    if even_mixer is not None:
        xb, o_a, pool_w, pool_scale, w_out = even_mixer
        xb3 = xb.reshape(batch, seq, POOL_WIDTH)
        hb = tm // POOL_HALO
        mixer_specs = [
            pl.BlockSpec((1, tm, POOL_WIDTH), tile),
            pl.BlockSpec((1, POOL_HALO, POOL_WIDTH),
                         lambda b, i: (b, jnp.maximum(i * hb - 1, 0), 0)),
            pl.BlockSpec((1, tm, SB_WIDTH), tile),
            pl.BlockSpec((len(POOL_WINDOWS), POOL_GROUP_DIM, POOL_GROUP_DIM), const3),
            pl.BlockSpec((1, POOL_WIDTH), const2),
            pl.BlockSpec((D_MODEL, D_MODEL), const2, **resident),
        ]
        mixer_args = [xb3, xb3, o_a, pool_w.astype(BF16), pool_scale.reshape(1, POOL_WIDTH),
                      w_out.astype(BF16)]
    return pl.pallas_call(
        functools.partial(_ffn_kernel, tm=tm, with_even_mixer=even_mixer is not None),
        grid=(batch, seq // tm),
        in_specs=[pl.BlockSpec((1, tm, D_MODEL), tile)] + mixer_specs + [
            pl.BlockSpec((1, D_MODEL), const2),
            pl.BlockSpec((None, D_MODEL, 2 * FFN_HIDDEN), this_layer, **resident),
            pl.BlockSpec((None, 8, 2 * FFN_HIDDEN), this_layer),
            pl.BlockSpec((None, FFN_HIDDEN, D_MODEL), this_layer, **resident),
        ],
        out_specs=pl.BlockSpec((1, tm, D_MODEL), tile),
        out_shape=jax.ShapeDtypeStruct((batch, seq, D_MODEL), F32),
        scratch_shapes=[
            pltpu.VMEM((tm, D_MODEL), BF16),
            pltpu.VMEM((nch, 8, 2 * ck), F32),
            pltpu.VMEM((tm + 8, 2 * ck), F32),
            pltpu.VMEM((tm, FFN_HIDDEN), BF16),
        ],
        compiler_params=pltpu.CompilerParams(
            dimension_semantics=("arbitrary", "arbitrary"), vmem_limit_bytes=VMEM_LIMIT),
        name="conv_ffn",
    )(x, *mixer_args, gain.reshape(1, D_MODEL), w_up, taps, w_down)


def _gla_kernel(x_ref, g_ref, win_ref, wa2_ref, ba_ref, og_ref, wout_ref, tri_ref,
                o_ref, state_ref, *, tm):
    i = pl.program_id(1)

    @pl.when(i == 0)
    def _():
        state_ref[...] = jnp.zeros_like(state_ref)

    x = x_ref[0]
    h = _rms(x, g_ref[...]).astype(BF16)
    proj = jnp.dot(h, win_ref[...], preferred_element_type=F32)
    kw, vw = GLA_KEY_WIDTH, GLA_VALUE_WIDTH
    q = proj[:, 0:kw] * (GLA_KEY_DIM ** -0.5)
    k = proj[:, kw:2 * kw]
    v = proj[:, 2 * kw:2 * kw + vw]
    r = proj[:, 2 * kw + vw:2 * kw + 2 * vw]
    a_low = proj[:, 2 * kw + 2 * vw:]
    gate = jnp.dot(a_low.astype(BF16), wa2_ref[...], preferred_element_type=F32) + ba_ref[...]
    log_alpha = (jnp.minimum(gate, 0.0) - _log1pexp_neg_abs(gate)) * (1.0 / GLA_TAU)
    hi, lo = _split_bf16(log_alpha)
    tri = tri_ref[...]
    span = tri.shape[0]
    cum = jnp.concatenate(
        [jnp.dot(tri, hi[r0:r0 + span, :], preferred_element_type=F32)
         + jnp.dot(tri, lo[r0:r0 + span, :], preferred_element_type=F32)
         for r0 in range(0, tm, span)], axis=0)

    nchunk = tm // GLA_CHUNK
    klanes = [slice(hd * GLA_KEY_DIM, (hd + 1) * GLA_KEY_DIM) for hd in range(GLA_HEADS)]
    vlanes = [slice(hd * GLA_VALUE_DIM, (hd + 1) * GLA_VALUE_DIM) for hd in range(GLA_HEADS)]
    updates, decays = [], []
    for c in range(nchunk):
        rows = slice(c * GLA_CHUNK, (c + 1) * GLA_CHUNK)
        cum_c = cum[rows, :]
        total = cum_c[GLA_CHUNK - 1:GLA_CHUNK, :]
        k_dec = k[rows, :] * jnp.exp(total - cum_c)
        decay = jnp.broadcast_to(jnp.exp(total), (GLA_CHUNK, kw))
        kd = jnp.concatenate([k_dec, decay], axis=0)
        v_c = v[rows, :].astype(BF16)
        for hd in range(GLA_HEADS):
            kd_t = kd[:, klanes[hd]].T
            decays.append(kd_t[:, GLA_CHUNK:GLA_CHUNK + 1])
            updates.append(jnp.dot(kd_t[:, :GLA_CHUNK].astype(BF16), v_c[:, vlanes[hd]],
                                   preferred_element_type=F32))
    states = []
    for hd in range(GLA_HEADS):
        st = state_ref[hd]
        for c in range(nchunk):
            st = st * decays[c * GLA_HEADS + hd] + updates[c * GLA_HEADS + hd]
            states.append(st.astype(BF16))
        state_ref[hd] = st
    outs = []
    for c in range(nchunk):
        q_c = q[c * GLA_CHUNK:(c + 1) * GLA_CHUNK, :].astype(BF16)
        outs.append(jnp.concatenate(
            [jnp.dot(q_c[:, klanes[hd]], states[hd * nchunk + c], preferred_element_type=F32)
             for hd in range(GLA_HEADS)], axis=1))
    o = jnp.concatenate(outs, axis=0)
    normed = jnp.concatenate(
        [_rms(o[:, hd * GLA_VALUE_DIM:(hd + 1) * GLA_VALUE_DIM], og_ref[...])
         for hd in range(GLA_HEADS)], axis=1)
    gated = (normed * (r * jax.nn.sigmoid(r))).astype(BF16)
    o_ref[0] = x + jnp.dot(gated, wout_ref[...], preferred_element_type=F32)


def _gla_mixer(x, gain, w_in, w_a2, b_a, o_gain, w_out):
    batch, seq, _ = x.shape
    tm = min(GLA_TM, seq)
    in_width = 2 * GLA_KEY_WIDTH + 2 * GLA_VALUE_WIDTH + GLA_GATE_PAD
    pad = GLA_GATE_PAD - GLA_GATE_RANK
    w_in_p = jnp.pad(w_in, ((0, 0), (0, pad))).astype(BF16)
    w_a2_p = jnp.pad(w_a2, ((0, pad), (0, 0))).astype(BF16)
    span = min(GLA_TRI_SPAN, tm)
    t = jnp.arange(span)
    tri = ((t[:, None] // GLA_CHUNK == t[None, :] // GLA_CHUNK)
           & (t[:, None] >= t[None, :])).astype(BF16)
    tile = lambda b, i: (b, i, 0)
    const2 = lambda b, i: (0, 0)
    return pl.pallas_call(
        functools.partial(_gla_kernel, tm=tm),
        grid=(batch, seq // tm),
        in_specs=[
            pl.BlockSpec((1, tm, D_MODEL), tile),
            pl.BlockSpec((1, D_MODEL), const2),
            pl.BlockSpec((D_MODEL, in_width), const2),
            pl.BlockSpec((GLA_GATE_PAD, GLA_KEY_WIDTH), const2),
            pl.BlockSpec((1, GLA_KEY_WIDTH), const2),
            pl.BlockSpec((1, GLA_VALUE_DIM), const2),
            pl.BlockSpec((GLA_VALUE_WIDTH, D_MODEL), const2),
            pl.BlockSpec((span, span), const2),
        ],
        out_specs=pl.BlockSpec((1, tm, D_MODEL), tile),
        out_shape=jax.ShapeDtypeStruct((batch, seq, D_MODEL), F32),
        scratch_shapes=[pltpu.VMEM((GLA_HEADS, GLA_KEY_DIM, GLA_VALUE_DIM), F32)],
        compiler_params=pltpu.CompilerParams(
            dimension_semantics=("arbitrary", "arbitrary"), vmem_limit_bytes=VMEM_LIMIT),
        name="gla_mixer",
    )(x, gain.reshape(1, D_MODEL), w_in_p, w_a2_p, b_a.reshape(1, GLA_KEY_WIDTH),
      o_gain.reshape(1, GLA_VALUE_DIM), w_out.astype(BF16), tri)


def kernel(x, mix_norm_even, w_in_even, sb_q_gain, sb_k_gain, pool_w, pool_scale, w_out_even,
           mix_norm_odd, w_in_odd, gla_w_a2, gla_b_a, gla_o_gain, w_out_odd,
           ffn_norm, ffn_w_up, ffn_conv_w, ffn_conv_b, ffn_w_down):
    batch, seq, _ = x.shape
    depth = ffn_norm.shape[0]
    ffn_weights = _ffn_weights(ffn_w_up, ffn_conv_w, ffn_conv_b, ffn_w_down)
    for layer in range(depth):
        i = layer // 2
        if layer % 2 == 0:
            q, k, v, xb = _even_in(x.reshape(batch * seq, D_MODEL), mix_norm_even[i],
                                   w_in_even[i], sb_q_gain[i], sb_k_gain[i])
            o_a = _attention(q, k, v, batch, seq)
            even_mixer = (xb, o_a, pool_w[i], pool_scale[i], w_out_even[i])
        else:
            x = _gla_mixer(x, mix_norm_odd[i], w_in_odd[i], gla_w_a2[i], gla_b_a[i],
                           gla_o_gain[i], w_out_odd[i])
            even_mixer = None
        x = _ffn(x, ffn_norm[layer], ffn_weights, layer, even_mixer)
    return x
```

```python
import functools

import jax
import jax.numpy as jnp
from jax import lax
from jax.experimental import pallas as pl
from jax.experimental.pallas import tpu as pltpu

F32 = jnp.float32
BF16 = jnp.bfloat16

D_MODEL = 1024
NORM_EPS = 1e-6
LOG2E = 1.4426950408889634
UNDERFLOW_LOG2 = 160.0

SB_HEADS = 8
SB_HEAD_DIM = 64
SB_WIDTH = SB_HEADS * SB_HEAD_DIM
POOL_WINDOWS = (2, 4, 8, 16)
POOL_GROUP_DIM = 128
POOL_WIDTH = 512
POOL_HALO = 16

GLA_HEADS = 4
GLA_KEY_WIDTH = 512
GLA_VALUE_WIDTH = 1024
GLA_KEY_DIM = 128
GLA_VALUE_DIM = 256
GLA_GATE_RANK = 16
GLA_TAU = 16.0
GLA_CHUNK = 64
GLA_GATE_PAD = 128

FFN_HIDDEN = 2816
FFN_CHUNK = 256
FFN_NCHUNK = FFN_HIDDEN // FFN_CHUNK

IN_TM = 1024
ATT_TQ = 512
ATT_TK = 128
ATT_PAIRS = 2
ATT_NB = 2
FFN_TM = 512
GLA_TM = 1024
GLA_TRI_SPAN = 256

VMEM_LIMIT = 56 * 1024 * 1024


def _rms(x, gain):
    ms = jnp.mean(x * x, axis=-1, keepdims=True)
    return x * lax.rsqrt(ms + NORM_EPS) * gain


def _log1pexp_neg_abs(z):
    return jnp.log(1.0 + jnp.exp(jnp.minimum(z, -z)))


def _split_bf16(x):
    hi = x.astype(BF16)
    lo = (x - hi.astype(F32)).astype(BF16)
    return hi, lo


def _even_in_kernel(x_ref, g_ref, w_ref, qg_ref, kg_ref, bd_ref,
                    q_ref, k_ref, v_ref, xb_ref):
    h = _rms(x_ref[...], g_ref[...]).astype(BF16)
    proj = jnp.dot(h, w_ref[...], preferred_element_type=F32)

    def head_norm(t, gain):
        ss = jnp.dot((t * t).astype(BF16), bd_ref[...], preferred_element_type=F32)
        return t * lax.rsqrt(ss * (1.0 / SB_HEAD_DIM) + NORM_EPS) * gain

    q = head_norm(proj[:, 0:SB_WIDTH], qg_ref[...])
    q_ref[...] = (q * (SB_HEAD_DIM ** -0.5 * LOG2E)).astype(BF16)
    k_ref[...] = head_norm(proj[:, SB_WIDTH:2 * SB_WIDTH], kg_ref[...]).astype(BF16)
    v_ref[...] = proj[:, 2 * SB_WIDTH:3 * SB_WIDTH].astype(BF16)
    xb_ref[...] = proj[:, 3 * SB_WIDTH:]


def _even_in(x2d, gain, w_in, q_gain, k_gain):
    n = x2d.shape[0]
    tm = min(IN_TM, n)
    head = jnp.arange(SB_WIDTH) // SB_HEAD_DIM
    blockdiag = (head[:, None] == head[None, :]).astype(BF16)
    width = w_in.shape[1]
    const = lambda i: (0, 0)
    tile = lambda i: (i, 0)
    return pl.pallas_call(
        _even_in_kernel,
        grid=(n // tm,),
        in_specs=[
            pl.BlockSpec((tm, D_MODEL), tile),
            pl.BlockSpec((1, D_MODEL), const),
            pl.BlockSpec((D_MODEL, width), const),
            pl.BlockSpec((1, SB_WIDTH), const),
            pl.BlockSpec((1, SB_WIDTH), const),
            pl.BlockSpec((SB_WIDTH, SB_WIDTH), const),
        ],
        out_specs=[
            pl.BlockSpec((tm, SB_WIDTH), tile),
            pl.BlockSpec((tm, SB_WIDTH), tile),
            pl.BlockSpec((tm, SB_WIDTH), tile),
            pl.BlockSpec((tm, POOL_WIDTH), tile),
        ],
        out_shape=[
            jax.ShapeDtypeStruct((n, SB_WIDTH), BF16),
            jax.ShapeDtypeStruct((n, SB_WIDTH), BF16),
            jax.ShapeDtypeStruct((n, SB_WIDTH), BF16),
            jax.ShapeDtypeStruct((n, POOL_WIDTH), F32),
        ],
        compiler_params=pltpu.CompilerParams(
            dimension_semantics=("arbitrary",), vmem_limit_bytes=VMEM_LIMIT),
        name="even_in",
    )(x2d, gain.reshape(1, D_MODEL), w_in.astype(BF16),
      jnp.tile(q_gain, SB_HEADS).reshape(1, SB_WIDTH),
      jnp.tile(k_gain, SB_HEADS).reshape(1, SB_WIDTH), blockdiag)


def _attn_kernel(q_ref, k_ref, v_ref, m2_ref, o_ref, za_ref, zb_ref, wa_ref, wb_ref,
                 c0_ref, c1_ref, acc_ref, *, tq, tk, nb):
    i = pl.program_id(2)
    npairs = q_ref.shape[2] // (2 * SB_HEAD_DIM)
    ratio = tq // tk
    ngroups = i * (ratio // nb)
    lane = lax.broadcasted_iota(jnp.int32, (tk, 2 * SB_HEAD_DIM), 1)
    first_head = lane < SB_HEAD_DIM
    zero = jnp.zeros((), BF16)

    def stack_heads(blk):
        return jnp.concatenate(
            [jnp.where(first_head, blk, zero), jnp.where(first_head, zero, blk)], axis=0)

    def stacked(ref, p, start, n):
        lanes = slice(p * 2 * SB_HEAD_DIM, (p + 1) * 2 * SB_HEAD_DIM)
        return jnp.concatenate(
            [stack_heads(ref[0, pl.ds(start + u * tk, tk), lanes]) for u in range(n)], axis=0)

    def scores(p, q_rows, start, n):
        return lax.dot_general(q_rows, stacked(k_ref, p, start, n), (((1,), (1,)), ((), ())),
                               preferred_element_type=F32)

    def attend(p, w, start, n):
        return jnp.dot(w, stacked(v_ref, p, start, n), preferred_element_type=F32)

    def group_start(g):
        return pl.multiple_of(jnp.maximum(i * ratio - (g + 1) * nb, 0) * tk, tk)

    def block_weights(z2, c0, c1, visible):
        ws, cs = [], []
        for hh, c in ((0, c0), (1, c1)):
            z = z2[:, hh * tk:(hh + 1) * tk]
            softplus = jnp.maximum(z, 0.0) + jnp.log2(1.0 + jnp.exp2(jnp.minimum(z, -z)))
            if visible is not None:
                softplus = jnp.where(visible, softplus, 0.0)
            hi, lo = _split_bf16(softplus)
            cum = jnp.dot(jnp.concatenate([hi, lo], axis=1), m2_ref[...],
                          preferred_element_type=F32)
            w = jnp.exp2(z + cum[:, :tk] + c)
            if visible is not None:
                w = jnp.where(visible, w, 0.0)
            ws.append(w.astype(BF16))
            cs.append(c + cum[:, tk:])
        return jnp.concatenate(ws, axis=1), cs[0], cs[1]

    def pair_query(p):
        return q_ref[0, :, p * 2 * SB_HEAD_DIM:(p + 1) * 2 * SB_HEAD_DIM]

    def diagonal(p, q):
        c0 = jnp.zeros((tq, tk), F32)
        c1 = jnp.zeros((tq, tk), F32)
        acc = jnp.zeros((tq, 2 * SB_HEAD_DIM), F32)
        for u in reversed(range(ratio)):
            r0 = u * tk
            start = pl.multiple_of(i * tq + r0, tk)
            row = lax.broadcasted_iota(jnp.int32, (tq - r0, tk), 0)
            col = lax.broadcasted_iota(jnp.int32, (tq - r0, tk), 1)
            w, p0, p1 = block_weights(scores(p, q[r0:, :], start, 1),
                                      c0[r0:, :], c1[r0:, :], col < row)
            pacc = acc[r0:, :] + attend(p, w, start, 1)
            if r0:
                c0 = jnp.concatenate([c0[:r0, :], p0], axis=0)
                c1 = jnp.concatenate([c1[:r0, :], p1], axis=0)
                acc = jnp.concatenate([acc[:r0, :], pacc], axis=0)
            else:
                c0, c1, acc = p0, p1, pacc
        return c0, c1, acc

    def spent(c0, c1):
        return jnp.maximum(jnp.max(c0), jnp.max(c1)) <= -UNDERFLOW_LOG2

    def stacked_all(ref, start):
        blk = ref[0, pl.ds(start, tk), :]
        head = lax.broadcasted_iota(jnp.int32, blk.shape, 1) // SB_HEAD_DIM
        return jnp.concatenate(
            [jnp.where(head == h, blk, zero) for h in range(2 * npairs)], axis=0)

    def near_short():
        q = q_ref[0]
        nheads = 2 * npairs
        c = [jnp.zeros((tq, tk), F32)] * nheads
        acc = jnp.zeros((tq, q.shape[1]), F32)

        def visit(r0, r1, start, visible, off_diagonal):
            z = lax.dot_general(q[r0:r1, :], stacked_all(k_ref, start),
                                (((1,), (1,)), ((), ())), preferred_element_type=F32)
            ws, cs = [], []
            for p in range(npairs):
                w, p0, p1 = block_weights(z[:, p * 2 * tk:(p + 1) * 2 * tk],
                                          c[2 * p][r0:r1, :], c[2 * p + 1][r0:r1, :], visible)
                ws.append(w)
                cs += [p0, p1]
            w = jnp.concatenate(ws, axis=1)
            if off_diagonal:
                w = jnp.where(i > 0, w, zero)
            return cs, acc[r0:r1, :] + jnp.dot(w, stacked_all(v_ref, start),
                                               preferred_element_type=F32)

        for u in reversed(range(ratio)):
            r0 = u * tk
            row = lax.broadcasted_iota(jnp.int32, (tq - r0, tk), 0)
            col = lax.broadcasted_iota(jnp.int32, (tq - r0, tk), 1)
            cs, pacc = visit(r0, tq, pl.multiple_of(i * tq + r0, tk), col < row, False)
            if r0:
                c = [jnp.concatenate([old[:r0, :], new], axis=0) for old, new in zip(c, cs)]
                acc = jnp.concatenate([acc[:r0, :], pacc], axis=0)
            else:
                c, acc = cs, pacc
        enough = [spent(c[2 * p][2 * tk:, :], c[2 * p + 1][2 * tk:, :]) for p in range(npairs)]
        for b in (1, 2):
            rows = (3 - b) * tk
            c, pacc = visit(0, rows, pl.multiple_of(jnp.maximum(i * ratio - b, 0) * tk, tk),
                            None, True)
            acc = jnp.concatenate([pacc, acc[rows:, :]], axis=0)
            enough = [jnp.logical_and(e, spent(c[2 * p][rows - tk:, :], c[2 * p + 1][rows - tk:, :]))
                      for p, e in enumerate(enough)]
        for p in range(npairs):
            acc_ref[p] = acc[:, p * 2 * SB_HEAD_DIM:(p + 1) * 2 * SB_HEAD_DIM]
        return [jnp.logical_or(i == 0, e) for e in enough]

    def near(p):
        q = pair_query(p)
        c0, c1, acc = diagonal(p, q)
        z_near = scores(p, q, group_start(0), nb)
        w_near = [None] * nb
        for u in reversed(range(nb)):
            w_near[u], c0, c1 = block_weights(z_near[:, u * 2 * tk:(u + 1) * 2 * tk],
                                              c0, c1, None)
        w_near = jnp.where(i > 0, jnp.concatenate(w_near, axis=1), zero)
        acc = acc + attend(p, w_near, group_start(0), nb)
        c0_ref[p] = c0
        c1_ref[p] = c1
        acc_ref[p] = acc

    def far(p):
        q = pair_query(p)

        def stage(g, z_in, z_out, w_in, w_out):
            acc_ref[p] += attend(p, w_in[...], group_start(g - 1), nb)
            z_out[...] = scores(p, q, group_start(g + 1), nb)
            for u in reversed(range(nb)):
                cols = slice(u * 2 * tk, (u + 1) * 2 * tk)
                w, c0, c1 = block_weights(z_in[:, cols], c0_ref[p], c1_ref[p], None)
                c0_ref[p] = c0
                c1_ref[p] = c1
                w_out[:, cols] = w

        def any_weight_left():
            return jnp.maximum(jnp.max(c0_ref[p]), jnp.max(c1_ref[p])) > -UNDERFLOW_LOG2

        farther = jnp.logical_and(ngroups > 1, any_weight_left())

        @pl.when(farther)
        def _():
            wb_ref[...] = jnp.zeros_like(wb_ref)
            za_ref[...] = scores(p, q, group_start(1), nb)

        def cond(state):
            g, alive = state
            return jnp.logical_and(g < ngroups, alive)

        def body(state):
            g, _ = state
            stage(g, za_ref, zb_ref, wb_ref, wa_ref)
            more = jnp.logical_and(g + 1 < ngroups, any_weight_left())

            @pl.when(more)
            def _():
                stage(g + 1, zb_ref, za_ref, wa_ref, wb_ref)

            return g + 1 + more.astype(jnp.int32), any_weight_left()

        last, _ = lax.while_loop(cond, body, (jnp.int32(1), farther))

        stages = last - 1

        @pl.when(stages % 2 == 1)
        def _():
            acc_ref[p] += attend(p, wa_ref[...], group_start(last - 1), nb)

        @pl.when(jnp.logical_and(stages > 0, stages % 2 == 0))
        def _():
            acc_ref[p] += attend(p, wb_ref[...], group_start(last - 1), nb)

        o_ref[0, :, p * 2 * SB_HEAD_DIM:(p + 1) * 2 * SB_HEAD_DIM] = acc_ref[p].astype(BF16)

    enough = near_short()
    for p in range(npairs):
        @pl.when(enough[p])
        def _(p=p):
            o_ref[0, :, p * 2 * SB_HEAD_DIM:(p + 1) * 2 * SB_HEAD_DIM] = acc_ref[p].astype(BF16)

        @pl.when(jnp.logical_not(enough[p]))
        def _(p=p):
            near(p)
            far(p)


def _attention(q, k, v, batch, seq):
    tq = min(ATT_TQ, seq)
    tk = min(ATT_TK, tq)
    q3 = q.reshape(batch, seq, SB_WIDTH)
    k3 = k.reshape(batch, seq, SB_WIDTH)
    v3 = v.reshape(batch, seq, SB_WIDTH)
    nb = ATT_NB
    assert seq % tq == 0 and (tq // tk) % nb == 0 and tq // tk >= 3
    r = jnp.arange(2 * tk) % tk
    ccol = jnp.arange(2 * tk)
    m2 = -jnp.where(ccol[None, :] < tk, r[:, None] >= ccol[None, :], True).astype(BF16)
    width = ATT_PAIRS * 2 * SB_HEAD_DIM
    return pl.pallas_call(
        functools.partial(_attn_kernel, tq=tq, tk=tk, nb=nb),
        grid=(batch, SB_WIDTH // width, seq // tq),
        scratch_shapes=[
            pltpu.VMEM((tq, nb * 2 * tk), F32), pltpu.VMEM((tq, nb * 2 * tk), F32),
            pltpu.VMEM((tq, nb * 2 * tk), BF16), pltpu.VMEM((tq, nb * 2 * tk), BF16),
            pltpu.VMEM((ATT_PAIRS, tq, tk), F32), pltpu.VMEM((ATT_PAIRS, tq, tk), F32),
            pltpu.VMEM((ATT_PAIRS, tq, 2 * SB_HEAD_DIM), F32),
        ],
        in_specs=[
            pl.BlockSpec((1, tq, width), lambda b, p, i: (b, i, p)),
            pl.BlockSpec((1, seq, width), lambda b, p, i: (b, 0, p)),
            pl.BlockSpec((1, seq, width), lambda b, p, i: (b, 0, p)),
            pl.BlockSpec((2 * tk, 2 * tk), lambda b, p, i: (0, 0)),
        ],
        out_specs=pl.BlockSpec((1, tq, width), lambda b, p, i: (b, i, p)),
        out_shape=jax.ShapeDtypeStruct((batch, seq, SB_WIDTH), BF16),
        compiler_params=pltpu.CompilerParams(
            dimension_semantics=("arbitrary", "arbitrary", "arbitrary"),
            vmem_limit_bytes=VMEM_LIMIT),
        name="sb_attention",
    )(q3, k3, v3, m2)


def _even_mixer_residual(i, tm, x_ref, xb_ref, halo_ref, oa_ref, pw_ref, ps_ref, wo_ref):
    xb = xb_ref[0]
    halo = jnp.where(i > 0, halo_ref[0], 0.0)
    ext = jnp.concatenate([halo, xb], axis=0)
    pos = (i * tm + 1 + lax.broadcasted_iota(jnp.int32, (tm, 1), 0)).astype(F32)
    y = jnp.dot(oa_ref[0], wo_ref[0:SB_WIDTH, :], preferred_element_type=F32)
    pooled_out = []
    for g, w in enumerate(POOL_WINDOWS):
        lanes = slice(g * POOL_GROUP_DIM, (g + 1) * POOL_GROUP_DIM)
        s = ext[:, lanes]
        sh = 1
        while sh < w:
            s = s + pltpu.roll(s, sh, axis=0)
            sh *= 2
        window_sum = s[POOL_HALO:, :]
        pooled = window_sum / jnp.minimum(pos, float(w)) - xb[:, lanes]
        ob = jnp.dot(pooled.astype(BF16), pw_ref[g], preferred_element_type=F32)
        pooled_out.append((ob * ps_ref[:, lanes]).astype(BF16))
    y = y + jnp.dot(jnp.concatenate(pooled_out, axis=1), wo_ref[SB_WIDTH:, :],
                    preferred_element_type=F32)
    return x_ref[0] + y


def _ffn_kernel(*refs, tm, with_even_mixer):
    n_mixer = 6 if with_even_mixer else 0
    x_ref, mixer_refs = refs[0], refs[1:1 + n_mixer]
    g_ref, wup_ref, cw_ref, wd_ref, o_ref, h_ref, carry_ref, u_ref, act_ref = refs[1 + n_mixer:]
    i = pl.program_id(1)
    nch = FFN_NCHUNK

    @pl.when(i == 0)
    def _():
        carry_ref[...] = jnp.zeros_like(carry_ref)

    if with_even_mixer:
        x = _even_mixer_residual(i, tm, x_ref, *mixer_refs)
    else:
        x = x_ref[0]
    h_ref[...] = _rms(x, g_ref[...]).astype(BF16)

    def columns(c):
        return (slice(c * FFN_CHUNK, (c + 1) * FFN_CHUNK),
                slice(FFN_HIDDEN + c * FFN_CHUNK, FFN_HIDDEN + (c + 1) * FFN_CHUNK))

    def up(c):
        h = h_ref[...]
        u = jnp.concatenate([jnp.dot(h, wup_ref[:, cols], preferred_element_type=F32)
                             for cols in columns(c)], axis=1)
        u_ref[0:8, :] = carry_ref[c]
        u_ref[8:8 + tm, :] = u
        carry_ref[c] = u[tm - 8:, :]

    def activate(c):
        cw = jnp.concatenate([cw_ref[:, cols] for cols in columns(c)], axis=1)
        conv = (cw[3:4, :] + cw[2:3, :] * u_ref[8:8 + tm, :]
                + cw[1:2, :] * u_ref[7:7 + tm, :]
                + cw[0:1, :] * u_ref[6:6 + tm, :])
        a = conv[:, :FFN_CHUNK]
        gate = conv[:, FFN_CHUNK:]
        act_ref[:, columns(c)[0]] = (a * jax.nn.sigmoid(a) * gate).astype(BF16)

    for c in range(nch):
        up(c)
        activate(c)
    o_ref[0] = x + jnp.dot(act_ref[...], wd_ref[...], preferred_element_type=F32)


def _ffn_weights(w_up, conv_w, conv_b, w_down):
    depth = w_up.shape[0]
    taps = jnp.concatenate([conv_w, conv_b[:, None, :],
                            jnp.zeros((depth, 4, 2 * FFN_HIDDEN), F32)], axis=1)
    return w_up.astype(BF16), taps, w_down.astype(BF16)


def _ffn(x, gain, weights, layer, even_mixer=None):
    batch, seq, _ = x.shape
    tm = min(FFN_TM, seq)
    ck, nch = FFN_CHUNK, FFN_NCHUNK
    w_up, taps, w_down = weights
    tile = lambda b, i: (b, i, 0)
    const2 = lambda b, i: (0, 0)
    const3 = lambda b, i: (0, 0, 0)
    this_layer = lambda b, i: (layer, 0, 0)
    resident = dict(pipeline_mode=pl.Buffered(1))
    mixer_specs, mixer_args = [], []
    if even_mixer is not None:
        xb, o_a, pool_w, pool_scale, w_out = even_mixer
        xb3 = xb.reshape(batch, seq, POOL_WIDTH)
        hb = tm // POOL_HALO
        mixer_specs = [
            pl.BlockSpec((1, tm, POOL_WIDTH), tile),
            pl.BlockSpec((1, POOL_HALO, POOL_WIDTH),
                         lambda b, i: (b, jnp.maximum(i * hb - 1, 0), 0)),
            pl.BlockSpec((1, tm, SB_WIDTH), tile),
            pl.BlockSpec((len(POOL_WINDOWS), POOL_GROUP_DIM, POOL_GROUP_DIM), const3),
            pl.BlockSpec((1, POOL_WIDTH), const2),
            pl.BlockSpec((D_MODEL, D_MODEL), const2, **resident),
        ]
        mixer_args = [xb3, xb3, o_a, pool_w.astype(BF16), pool_scale.reshape(1, POOL_WIDTH),
                      w_out.astype(BF16)]
    return pl.pallas_call(
        functools.partial(_ffn_kernel, tm=tm, with_even_mixer=even_mixer is not None),
        grid=(batch, seq // tm),
        in_specs=[pl.BlockSpec((1, tm, D_MODEL), tile)] + mixer_specs + [
            pl.BlockSpec((1, D_MODEL), const2),
            pl.BlockSpec((None, D_MODEL, 2 * FFN_HIDDEN), this_layer, **resident),
            pl.BlockSpec((None, 8, 2 * FFN_HIDDEN), this_layer),
            pl.BlockSpec((None, FFN_HIDDEN, D_MODEL), this_layer, **resident),
        ],
        out_specs=pl.BlockSpec((1, tm, D_MODEL), tile),
        out_shape=jax.ShapeDtypeStruct((batch, seq, D_MODEL), F32),
        scratch_shapes=[
            pltpu.VMEM((tm, D_MODEL), BF16),
            pltpu.VMEM((nch, 8, 2 * ck), F32),
            pltpu.VMEM((tm + 8, 2 * ck), F32),
            pltpu.VMEM((tm, FFN_HIDDEN), BF16),
        ],
        compiler_params=pltpu.CompilerParams(
            dimension_semantics=("arbitrary", "arbitrary"), vmem_limit_bytes=VMEM_LIMIT),
        name="conv_ffn",
    )(x, *mixer_args, gain.reshape(1, D_MODEL), w_up, taps, w_down)


def _gla_kernel(x_ref, g_ref, win_ref, wa2_ref, ba_ref, og_ref, wout_ref, tri_ref,
                o_ref, state_ref, *, tm):
    i = pl.program_id(1)

    @pl.when(i == 0)
    def _():
        state_ref[...] = jnp.zeros_like(state_ref)

    x = x_ref[0]
    h = _rms(x, g_ref[...]).astype(BF16)
    proj = jnp.dot(h, win_ref[...], preferred_element_type=F32)
    kw, vw = GLA_KEY_WIDTH, GLA_VALUE_WIDTH
    q = proj[:, 0:kw] * (GLA_KEY_DIM ** -0.5)
    k = proj[:, kw:2 * kw]
    v = proj[:, 2 * kw:2 * kw + vw]
    r = proj[:, 2 * kw + vw:2 * kw + 2 * vw]
    a_low = proj[:, 2 * kw + 2 * vw:]
    gate = jnp.dot(a_low.astype(BF16), wa2_ref[...], preferred_element_type=F32) + ba_ref[...]
    log_alpha = (jnp.minimum(gate, 0.0) - _log1pexp_neg_abs(gate)) * (1.0 / GLA_TAU)
    hi, lo = _split_bf16(log_alpha)
    tri = tri_ref[...]
    span = tri.shape[0]
    cum = jnp.concatenate(
        [jnp.dot(tri, hi[r0:r0 + span, :], preferred_element_type=F32)
         + jnp.dot(tri, lo[r0:r0 + span, :], preferred_element_type=F32)
         for r0 in range(0, tm, span)], axis=0)

    nchunk = tm // GLA_CHUNK
    klanes = [slice(hd * GLA_KEY_DIM, (hd + 1) * GLA_KEY_DIM) for hd in range(GLA_HEADS)]
    vlanes = [slice(hd * GLA_VALUE_DIM, (hd + 1) * GLA_VALUE_DIM) for hd in range(GLA_HEADS)]
    updates, decays = [], []
    for c in range(nchunk):
        rows = slice(c * GLA_CHUNK, (c + 1) * GLA_CHUNK)
        cum_c = cum[rows, :]
        total = cum_c[GLA_CHUNK - 1:GLA_CHUNK, :]
        k_dec = k[rows, :] * jnp.exp(total - cum_c)
        decay = jnp.broadcast_to(jnp.exp(total), (GLA_CHUNK, kw))
        kd = jnp.concatenate([k_dec, decay], axis=0)
        v_c = v[rows, :].astype(BF16)
        for hd in range(GLA_HEADS):
            kd_t = kd[:, klanes[hd]].T
            decays.append(kd_t[:, GLA_CHUNK:GLA_CHUNK + 1])
            updates.append(jnp.dot(kd_t[:, :GLA_CHUNK].astype(BF16), v_c[:, vlanes[hd]],
                                   preferred_element_type=F32))
    states = []
    for hd in range(GLA_HEADS):
        st = state_ref[hd]
        for c in range(nchunk):
            st = st * decays[c * GLA_HEADS + hd] + updates[c * GLA_HEADS + hd]
            states.append(st.astype(BF16))
        state_ref[hd] = st
    outs = []
    for c in range(nchunk):
        q_c = q[c * GLA_CHUNK:(c + 1) * GLA_CHUNK, :].astype(BF16)
        outs.append(jnp.concatenate(
            [jnp.dot(q_c[:, klanes[hd]], states[hd * nchunk + c], preferred_element_type=F32)
             for hd in range(GLA_HEADS)], axis=1))
    o = jnp.concatenate(outs, axis=0)
    normed = jnp.concatenate(
        [_rms(o[:, hd * GLA_VALUE_DIM:(hd + 1) * GLA_VALUE_DIM], og_ref[...])
         for hd in range(GLA_HEADS)], axis=1)
    gated = (normed * (r * jax.nn.sigmoid(r))).astype(BF16)
    o_ref[0] = x + jnp.dot(gated, wout_ref[...], preferred_element_type=F32)


def _gla_mixer(x, gain, w_in, w_a2, b_a, o_gain, w_out):
    batch, seq, _ = x.shape
    tm = min(GLA_TM, seq)
    in_width = 2 * GLA_KEY_WIDTH + 2 * GLA_VALUE_WIDTH + GLA_GATE_PAD
    pad = GLA_GATE_PAD - GLA_GATE_RANK
    w_in_p = jnp.pad(w_in.astype(BF16), ((0, 0), (0, pad)))
    w_a2_p = jnp.pad(w_a2.astype(BF16), ((0, pad), (0, 0)))
    span = min(GLA_TRI_SPAN, tm)
    t = jnp.arange(span)
    tri = ((t[:, None] // GLA_CHUNK == t[None, :] // GLA_CHUNK)
           & (t[:, None] >= t[None, :])).astype(BF16)
    tile = lambda b, i: (b, i, 0)
    const2 = lambda b, i: (0, 0)
    return pl.pallas_call(
        functools.partial(_gla_kernel, tm=tm),
        grid=(batch, seq // tm),
        in_specs=[
            pl.BlockSpec((1, tm, D_MODEL), tile),
            pl.BlockSpec((1, D_MODEL), const2),
            pl.BlockSpec((D_MODEL, in_width), const2),
            pl.BlockSpec((GLA_GATE_PAD, GLA_KEY_WIDTH), const2),
            pl.BlockSpec((1, GLA_KEY_WIDTH), const2),
            pl.BlockSpec((1, GLA_VALUE_DIM), const2),
            pl.BlockSpec((GLA_VALUE_WIDTH, D_MODEL), const2),
            pl.BlockSpec((span, span), const2),
        ],
        out_specs=pl.BlockSpec((1, tm, D_MODEL), tile),
        out_shape=jax.ShapeDtypeStruct((batch, seq, D_MODEL), F32),
        scratch_shapes=[pltpu.VMEM((GLA_HEADS, GLA_KEY_DIM, GLA_VALUE_DIM), F32)],
        compiler_params=pltpu.CompilerParams(
            dimension_semantics=("arbitrary", "arbitrary"), vmem_limit_bytes=VMEM_LIMIT),
        name="gla_mixer",
    )(x, gain.reshape(1, D_MODEL), w_in_p, w_a2_p, b_a.reshape(1, GLA_KEY_WIDTH),
      o_gain.reshape(1, GLA_VALUE_DIM), w_out.astype(BF16), tri)


def kernel(x, mix_norm_even, w_in_even, sb_q_gain, sb_k_gain, pool_w, pool_scale, w_out_even,
           mix_norm_odd, w_in_odd, gla_w_a2, gla_b_a, gla_o_gain, w_out_odd,
           ffn_norm, ffn_w_up, ffn_conv_w, ffn_conv_b, ffn_w_down):
    batch, seq, _ = x.shape
    depth = ffn_norm.shape[0]
    ffn_weights = _ffn_weights(ffn_w_up, ffn_conv_w, ffn_conv_b, ffn_w_down)
    for layer in range(depth):
        i = layer // 2
        if layer % 2 == 0:
            q, k, v, xb = _even_in(x.reshape(batch * seq, D_MODEL), mix_norm_even[i],
                                   w_in_even[i], sb_q_gain[i], sb_k_gain[i])
            o_a = _attention(q, k, v, batch, seq)
            even_mixer = (xb, o_a, pool_w[i], pool_scale[i], w_out_even[i])
        else:
            x = _gla_mixer(x, mix_norm_odd[i], w_in_odd[i], gla_w_a2[i], gla_b_a[i],
                           gla_o_gain[i], w_out_odd[i])
            even_mixer = None
        x = _ffn(x, ffn_norm[layer], ffn_weights, layer, even_mixer)
    return x
```

```python
import functools

import jax
import jax.numpy as jnp
from jax import lax
from jax.experimental import pallas as pl
from jax.experimental.pallas import tpu as pltpu

F32 = jnp.float32
BF16 = jnp.bfloat16

D_MODEL = 1024
NORM_EPS = 1e-6
LOG2E = 1.4426950408889634
UNDERFLOW_LOG2 = 160.0

SB_HEADS = 8
SB_HEAD_DIM = 64
SB_WIDTH = SB_HEADS * SB_HEAD_DIM
POOL_WINDOWS = (2, 4, 8, 16)
POOL_GROUP_DIM = 128
POOL_WIDTH = 512
POOL_HALO = 16

GLA_HEADS = 4
GLA_KEY_WIDTH = 512
GLA_VALUE_WIDTH = 1024
GLA_KEY_DIM = 128
GLA_VALUE_DIM = 256
GLA_GATE_RANK = 16
GLA_TAU = 16.0
GLA_CHUNK = 64
GLA_GATE_PAD = 128

FFN_HIDDEN = 2816
FFN_CHUNK = 256
FFN_NCHUNK = FFN_HIDDEN // FFN_CHUNK

IN_TM = 1024
ATT_TQ = 512
ATT_TK = 128
ATT_PAIRS = 2
ATT_NB = 2
FFN_TM = 512
GLA_TM = 1024
GLA_TRI_SPAN = 256

VMEM_LIMIT = 56 * 1024 * 1024


def _rms(x, gain):
    ms = jnp.mean(x * x, axis=-1, keepdims=True)
    return x * lax.rsqrt(ms + NORM_EPS) * gain


def _log1pexp_neg_abs(z):
    return jnp.log(1.0 + jnp.exp(jnp.minimum(z, -z)))


def _split_bf16(x):
    hi = x.astype(BF16)
    lo = (x - hi.astype(F32)).astype(BF16)
    return hi, lo


def _even_in_kernel(x_ref, g_ref, w_ref, qg_ref, kg_ref, bd_ref,
                    q_ref, k_ref, v_ref, xb_ref):
    h = _rms(x_ref[...], g_ref[...]).astype(BF16)
    proj = jnp.dot(h, w_ref[...], preferred_element_type=F32)

    def head_norm(t, gain):
        ss = jnp.dot((t * t).astype(BF16), bd_ref[...], preferred_element_type=F32)
        return t * lax.rsqrt(ss * (1.0 / SB_HEAD_DIM) + NORM_EPS) * gain

    q = head_norm(proj[:, 0:SB_WIDTH], qg_ref[...])
    q_ref[...] = (q * (SB_HEAD_DIM ** -0.5 * LOG2E)).astype(BF16)
    k_ref[...] = head_norm(proj[:, SB_WIDTH:2 * SB_WIDTH], kg_ref[...]).astype(BF16)
    v_ref[...] = proj[:, 2 * SB_WIDTH:3 * SB_WIDTH].astype(BF16)
    xb_ref[...] = proj[:, 3 * SB_WIDTH:]


def _even_in(x2d, gain, w_in, q_gain, k_gain):
    n = x2d.shape[0]
    tm = min(IN_TM, n)
    head = jnp.arange(SB_WIDTH) // SB_HEAD_DIM
    blockdiag = (head[:, None] == head[None, :]).astype(BF16)
    width = w_in.shape[1]
    const = lambda i: (0, 0)
    tile = lambda i: (i, 0)
    return pl.pallas_call(
        _even_in_kernel,
        grid=(n // tm,),
        in_specs=[
            pl.BlockSpec((tm, D_MODEL), tile),
            pl.BlockSpec((1, D_MODEL), const),
            pl.BlockSpec((D_MODEL, width), const),
            pl.BlockSpec((1, SB_WIDTH), const),
            pl.BlockSpec((1, SB_WIDTH), const),
            pl.BlockSpec((SB_WIDTH, SB_WIDTH), const),
        ],
        out_specs=[
            pl.BlockSpec((tm, SB_WIDTH), tile),
            pl.BlockSpec((tm, SB_WIDTH), tile),
            pl.BlockSpec((tm, SB_WIDTH), tile),
            pl.BlockSpec((tm, POOL_WIDTH), tile),
        ],
        out_shape=[
            jax.ShapeDtypeStruct((n, SB_WIDTH), BF16),
            jax.ShapeDtypeStruct((n, SB_WIDTH), BF16),
            jax.ShapeDtypeStruct((n, SB_WIDTH), BF16),
            jax.ShapeDtypeStruct((n, POOL_WIDTH), F32),
        ],
        compiler_params=pltpu.CompilerParams(
            dimension_semantics=("arbitrary",), vmem_limit_bytes=VMEM_LIMIT),
        name="even_in",
    )(x2d, gain.reshape(1, D_MODEL), w_in.astype(BF16),
      jnp.tile(q_gain, SB_HEADS).reshape(1, SB_WIDTH),
      jnp.tile(k_gain, SB_HEADS).reshape(1, SB_WIDTH), blockdiag)


def _attn_kernel(q_ref, k_ref, v_ref, m2_ref, o_ref, za_ref, zb_ref, wa_ref, wb_ref,
                 c0_ref, c1_ref, acc_ref, *, tq, tk, nb):
    i = pl.program_id(2)
    npairs = q_ref.shape[2] // (2 * SB_HEAD_DIM)
    ratio = tq // tk
    ngroups = i * (ratio // nb)
    lane = lax.broadcasted_iota(jnp.int32, (tk, 2 * SB_HEAD_DIM), 1)
    first_head = lane < SB_HEAD_DIM
    zero = jnp.zeros((), BF16)

    def stack_heads(blk):
        return jnp.concatenate(
            [jnp.where(first_head, blk, zero), jnp.where(first_head, zero, blk)], axis=0)

    def stacked(ref, p, start, n):
        lanes = slice(p * 2 * SB_HEAD_DIM, (p + 1) * 2 * SB_HEAD_DIM)
        return jnp.concatenate(
            [stack_heads(ref[0, pl.ds(start + u * tk, tk), lanes]) for u in range(n)], axis=0)

    def scores(p, q_rows, start, n):
        return lax.dot_general(q_rows, stacked(k_ref, p, start, n), (((1,), (1,)), ((), ())),
                               preferred_element_type=F32)

    def attend(p, w, start, n):
        return jnp.dot(w, stacked(v_ref, p, start, n), preferred_element_type=F32)

    def group_start(g):
        return pl.multiple_of(jnp.maximum(i * ratio - (g + 1) * nb, 0) * tk, tk)

    def block_weights(z2, c0, c1, visible):
        ws, cs = [], []
        for hh, c in ((0, c0), (1, c1)):
            z = z2[:, hh * tk:(hh + 1) * tk]
            softplus = jnp.maximum(z, 0.0) + jnp.log2(1.0 + jnp.exp2(jnp.minimum(z, -z)))
            if visible is not None:
                softplus = jnp.where(visible, softplus, 0.0)
            hi, lo = _split_bf16(softplus)
            cum = jnp.dot(jnp.concatenate([hi, lo], axis=1), m2_ref[...],
                          preferred_element_type=F32)
            w = jnp.exp2(z + cum[:, :tk] + c)
            if visible is not None:
                w = jnp.where(visible, w, 0.0)
            ws.append(w.astype(BF16))
            cs.append(c + cum[:, tk:])
        return jnp.concatenate(ws, axis=1), cs[0], cs[1]

    def pair_query(p):
        return q_ref[0, :, p * 2 * SB_HEAD_DIM:(p + 1) * 2 * SB_HEAD_DIM]

    def diagonal(p, q):
        c0 = jnp.zeros((tq, tk), F32)
        c1 = jnp.zeros((tq, tk), F32)
        acc = jnp.zeros((tq, 2 * SB_HEAD_DIM), F32)
        for u in reversed(range(ratio)):
            r0 = u * tk
            start = pl.multiple_of(i * tq + r0, tk)
            row = lax.broadcasted_iota(jnp.int32, (tq - r0, tk), 0)
            col = lax.broadcasted_iota(jnp.int32, (tq - r0, tk), 1)
            w, p0, p1 = block_weights(scores(p, q[r0:, :], start, 1),
                                      c0[r0:, :], c1[r0:, :], col < row)
            pacc = acc[r0:, :] + attend(p, w, start, 1)
            if r0:
                c0 = jnp.concatenate([c0[:r0, :], p0], axis=0)
                c1 = jnp.concatenate([c1[:r0, :], p1], axis=0)
                acc = jnp.concatenate([acc[:r0, :], pacc], axis=0)
            else:
                c0, c1, acc = p0, p1, pacc
        return c0, c1, acc

    def spent(c0, c1):
        return jnp.maximum(jnp.max(c0), jnp.max(c1)) <= -UNDERFLOW_LOG2

    def stacked_all(ref, start):
        blk = ref[0, pl.ds(start, tk), :]
        head = lax.broadcasted_iota(jnp.int32, blk.shape, 1) // SB_HEAD_DIM
        return jnp.concatenate(
            [jnp.where(head == h, blk, zero) for h in range(2 * npairs)], axis=0)

    def near_short():
        q = q_ref[0]
        nheads = 2 * npairs
        c = [jnp.zeros((tq, tk), F32)] * nheads
        acc = jnp.zeros((tq, q.shape[1]), F32)

        def visit(r0, r1, start, visible, off_diagonal):
            z = lax.dot_general(q[r0:r1, :], stacked_all(k_ref, start),
                                (((1,), (1,)), ((), ())), preferred_element_type=F32)
            ws, cs = [], []
            for p in range(npairs):
                w, p0, p1 = block_weights(z[:, p * 2 * tk:(p + 1) * 2 * tk],
                                          c[2 * p][r0:r1, :], c[2 * p + 1][r0:r1, :], visible)
                ws.append(w)
                cs += [p0, p1]
            w = jnp.concatenate(ws, axis=1)
            if off_diagonal:
                w = jnp.where(i > 0, w, zero)
            return cs, acc[r0:r1, :] + jnp.dot(w, stacked_all(v_ref, start),
                                               preferred_element_type=F32)

        for u in reversed(range(ratio)):
            r0, r1 = u * tk, min(tq, (u + 3) * tk)
            row = lax.broadcasted_iota(jnp.int32, (r1 - r0, tk), 0)
            col = lax.broadcasted_iota(jnp.int32, (r1 - r0, tk), 1)
            cs, pacc = visit(r0, r1, pl.multiple_of(i * tq + r0, tk), col < row, False)
            def replace_rows(old, new):
                parts = [old[:r0, :]] * (r0 > 0) + [new] + [old[r1:, :]] * (r1 < tq)
                return jnp.concatenate(parts, axis=0) if len(parts) > 1 else new

            c = [replace_rows(old, new) for old, new in zip(c, cs)]
            acc = replace_rows(acc, pacc)
        skipped_ok = [spent(c[2 * p][3 * tk:, :], c[2 * p + 1][3 * tk:, :]) if ratio > 3 else True
                      for p in range(npairs)]
        enough = [spent(c[2 * p][2 * tk:, :], c[2 * p + 1][2 * tk:, :]) for p in range(npairs)]
        for b in (1, 2):
            rows = (3 - b) * tk
            c, pacc = visit(0, rows, pl.multiple_of(jnp.maximum(i * ratio - b, 0) * tk, tk),
                            None, True)
            acc = jnp.concatenate([pacc, acc[rows:, :]], axis=0)
            enough = [jnp.logical_and(e, spent(c[2 * p][rows - tk:, :], c[2 * p + 1][rows - tk:, :]))
                      for p, e in enumerate(enough)]
        for p in range(npairs):
            acc_ref[p] = acc[:, p * 2 * SB_HEAD_DIM:(p + 1) * 2 * SB_HEAD_DIM]
        return [jnp.logical_and(s, jnp.logical_or(i == 0, e)) for s, e in zip(skipped_ok, enough)]

    def near(p):
        q = pair_query(p)
        c0, c1, acc = diagonal(p, q)
        z_near = scores(p, q, group_start(0), nb)
        w_near = [None] * nb
        for u in reversed(range(nb)):
            w_near[u], c0, c1 = block_weights(z_near[:, u * 2 * tk:(u + 1) * 2 * tk],
                                              c0, c1, None)
        w_near = jnp.where(i > 0, jnp.concatenate(w_near, axis=1), zero)
        acc = acc + attend(p, w_near, group_start(0), nb)
        c0_ref[p] = c0
        c1_ref[p] = c1
        acc_ref[p] = acc

    def far(p):
        q = pair_query(p)

        def stage(g, z_in, z_out, w_in, w_out):
            acc_ref[p] += attend(p, w_in[...], group_start(g - 1), nb)
            z_out[...] = scores(p, q, group_start(g + 1), nb)
            for u in reversed(range(nb)):
                cols = slice(u * 2 * tk, (u + 1) * 2 * tk)
                w, c0, c1 = block_weights(z_in[:, cols], c0_ref[p], c1_ref[p], None)
                c0_ref[p] = c0
                c1_ref[p] = c1
                w_out[:, cols] = w

        def any_weight_left():
            return jnp.maximum(jnp.max(c0_ref[p]), jnp.max(c1_ref[p])) > -UNDERFLOW_LOG2

        farther = jnp.logical_and(ngroups > 1, any_weight_left())

        @pl.when(farther)
        def _():
            wb_ref[...] = jnp.zeros_like(wb_ref)
            za_ref[...] = scores(p, q, group_start(1), nb)

        def cond(state):
            g, alive = state
            return jnp.logical_and(g < ngroups, alive)

        def body(state):
            g, _ = state
            stage(g, za_ref, zb_ref, wb_ref, wa_ref)
            more = jnp.logical_and(g + 1 < ngroups, any_weight_left())

            @pl.when(more)
            def _():
                stage(g + 1, zb_ref, za_ref, wa_ref, wb_ref)

            return g + 1 + more.astype(jnp.int32), any_weight_left()

        last, _ = lax.while_loop(cond, body, (jnp.int32(1), farther))

        stages = last - 1

        @pl.when(stages % 2 == 1)
        def _():
            acc_ref[p] += attend(p, wa_ref[...], group_start(last - 1), nb)

        @pl.when(jnp.logical_and(stages > 0, stages % 2 == 0))
        def _():
            acc_ref[p] += attend(p, wb_ref[...], group_start(last - 1), nb)

        o_ref[0, :, p * 2 * SB_HEAD_DIM:(p + 1) * 2 * SB_HEAD_DIM] = acc_ref[p].astype(BF16)

    enough = near_short()
    for p in range(npairs):
        @pl.when(enough[p])
        def _(p=p):
            o_ref[0, :, p * 2 * SB_HEAD_DIM:(p + 1) * 2 * SB_HEAD_DIM] = acc_ref[p].astype(BF16)

        @pl.when(jnp.logical_not(enough[p]))
        def _(p=p):
            near(p)
            far(p)


def _attention(q, k, v, batch, seq):
    tq = min(ATT_TQ, seq)
    tk = min(ATT_TK, tq)
    q3 = q.reshape(batch, seq, SB_WIDTH)
    k3 = k.reshape(batch, seq, SB_WIDTH)
    v3 = v.reshape(batch, seq, SB_WIDTH)
    nb = ATT_NB
    assert seq % tq == 0 and (tq // tk) % nb == 0 and tq // tk >= 3
    r = jnp.arange(2 * tk) % tk
    ccol = jnp.arange(2 * tk)
    m2 = -jnp.where(ccol[None, :] < tk, r[:, None] >= ccol[None, :], True).astype(BF16)
    width = ATT_PAIRS * 2 * SB_HEAD_DIM
    return pl.pallas_call(
        functools.partial(_attn_kernel, tq=tq, tk=tk, nb=nb),
        grid=(batch, SB_WIDTH // width, seq // tq),
        scratch_shapes=[
            pltpu.VMEM((tq, nb * 2 * tk), F32), pltpu.VMEM((tq, nb * 2 * tk), F32),
            pltpu.VMEM((tq, nb * 2 * tk), BF16), pltpu.VMEM((tq, nb * 2 * tk), BF16),
            pltpu.VMEM((ATT_PAIRS, tq, tk), F32), pltpu.VMEM((ATT_PAIRS, tq, tk), F32),
            pltpu.VMEM((ATT_PAIRS, tq, 2 * SB_HEAD_DIM), F32),
        ],
        in_specs=[
            pl.BlockSpec((1, tq, width), lambda b, p, i: (b, i, p)),
            pl.BlockSpec((1, seq, width), lambda b, p, i: (b, 0, p)),
            pl.BlockSpec((1, seq, width), lambda b, p, i: (b, 0, p)),
            pl.BlockSpec((2 * tk, 2 * tk), lambda b, p, i: (0, 0)),
        ],
        out_specs=pl.BlockSpec((1, tq, width), lambda b, p, i: (b, i, p)),
        out_shape=jax.ShapeDtypeStruct((batch, seq, SB_WIDTH), BF16),
        compiler_params=pltpu.CompilerParams(
            dimension_semantics=("arbitrary", "arbitrary", "arbitrary"),
            vmem_limit_bytes=VMEM_LIMIT),
        name="sb_attention",
    )(q3, k3, v3, m2)


def _even_mixer_residual(i, tm, x_ref, xb_ref, halo_ref, oa_ref, pw_ref, ps_ref, wo_ref):
    xb = xb_ref[0]
    halo = jnp.where(i > 0, halo_ref[0], 0.0)
    ext = jnp.concatenate([halo, xb], axis=0)
    pos = (i * tm + 1 + lax.broadcasted_iota(jnp.int32, (tm, 1), 0)).astype(F32)
    y = jnp.dot(oa_ref[0], wo_ref[0:SB_WIDTH, :], preferred_element_type=F32)
    pooled_out = []
    for g, w in enumerate(POOL_WINDOWS):
        lanes = slice(g * POOL_GROUP_DIM, (g + 1) * POOL_GROUP_DIM)
        s = ext[:, lanes]
        sh = 1
        while sh < w:
            s = s + pltpu.roll(s, sh, axis=0)
            sh *= 2
        window_sum = s[POOL_HALO:, :]
        pooled = window_sum / jnp.minimum(pos, float(w)) - xb[:, lanes]
        ob = jnp.dot(pooled.astype(BF16), pw_ref[g], preferred_element_type=F32)
        pooled_out.append((ob * ps_ref[:, lanes]).astype(BF16))
    y = y + jnp.dot(jnp.concatenate(pooled_out, axis=1), wo_ref[SB_WIDTH:, :],
                    preferred_element_type=F32)
    return x_ref[0] + y


def _ffn_kernel(*refs, tm, with_even_mixer):
    n_mixer = 6 if with_even_mixer else 0
    x_ref, mixer_refs = refs[0], refs[1:1 + n_mixer]
    g_ref, wup_ref, cw_ref, wd_ref, o_ref, h_ref, carry_ref, u_ref, act_ref = refs[1 + n_mixer:]
    i = pl.program_id(1)
    nch = FFN_NCHUNK

    @pl.when(i == 0)
    def _():
        carry_ref[...] = jnp.zeros_like(carry_ref)

    if with_even_mixer:
        x = _even_mixer_residual(i, tm, x_ref, *mixer_refs)
    else:
        x = x_ref[0]
    h_ref[...] = _rms(x, g_ref[...]).astype(BF16)

    def columns(c):
        return (slice(c * FFN_CHUNK, (c + 1) * FFN_CHUNK),
                slice(FFN_HIDDEN + c * FFN_CHUNK, FFN_HIDDEN + (c + 1) * FFN_CHUNK))

    def up(c):
        h = h_ref[...]
        u = jnp.concatenate([jnp.dot(h, wup_ref[:, cols], preferred_element_type=F32)
                             for cols in columns(c)], axis=1)
        u_ref[0:8, :] = carry_ref[c]
        u_ref[8:8 + tm, :] = u
        carry_ref[c] = u[tm - 8:, :]

    def activate(c):
        cw = jnp.concatenate([cw_ref[:, cols] for cols in columns(c)], axis=1)
        conv = (cw[3:4, :] + cw[2:3, :] * u_ref[8:8 + tm, :]
                + cw[1:2, :] * u_ref[7:7 + tm, :]
                + cw[0:1, :] * u_ref[6:6 + tm, :])
        a = conv[:, :FFN_CHUNK]
        gate = conv[:, FFN_CHUNK:]
        act_ref[:, columns(c)[0]] = (a * jax.nn.sigmoid(a) * gate).astype(BF16)

    for c in range(nch):
        up(c)
        activate(c)
    o_ref[0] = x + jnp.dot(act_ref[...], wd_ref[...], preferred_element_type=F32)


def _ffn_weights(w_up, conv_w, conv_b, w_down):
    depth = w_up.shape[0]
    taps = jnp.concatenate([conv_w, conv_b[:, None, :],
                            jnp.zeros((depth, 4, 2 * FFN_HIDDEN), F32)], axis=1)
    return w_up.astype(BF16), taps, w_down.astype(BF16)


def _ffn(x, gain, weights, layer, even_mixer=None):
    batch, seq, _ = x.shape
    tm = min(FFN_TM, seq)
    ck, nch = FFN_CHUNK, FFN_NCHUNK
    w_up, taps, w_down = weights
    tile = lambda b, i: (b, i, 0)
    const2 = lambda b, i: (0, 0)
    const3 = lambda b, i: (0, 0, 0)
    this_layer = lambda b, i: (layer, 0, 0)
    resident = dict(pipeline_mode=pl.Buffered(1))
    mixer_specs, mixer_args = [], []
    if even_mixer is not None:
        xb, o_a, pool_w, pool_scale, w_out = even_mixer
        xb3 = xb.reshape(batch, seq, POOL_WIDTH)
        hb = tm // POOL_HALO
        mixer_specs = [
            pl.BlockSpec((1, tm, POOL_WIDTH), tile),
            pl.BlockSpec((1, POOL_HALO, POOL_WIDTH),
                         lambda b, i: (b, jnp.maximum(i * hb - 1, 0), 0)),
            pl.BlockSpec((1, tm, SB_WIDTH), tile),
            pl.BlockSpec((len(POOL_WINDOWS), POOL_GROUP_DIM, POOL_GROUP_DIM), const3),
            pl.BlockSpec((1, POOL_WIDTH), const2),
            pl.BlockSpec((D_MODEL, D_MODEL), const2, **resident),
        ]
        mixer_args = [xb3, xb3, o_a, pool_w.astype(BF16), pool_scale.reshape(1, POOL_WIDTH),
                      w_out.astype(BF16)]
    return pl.pallas_call(
        functools.partial(_ffn_kernel, tm=tm, with_even_mixer=even_mixer is not None),
        grid=(batch, seq // tm),
        in_specs=[pl.BlockSpec((1, tm, D_MODEL), tile)] + mixer_specs + [
            pl.BlockSpec((1, D_MODEL), const2),
            pl.BlockSpec((None, D_MODEL, 2 * FFN_HIDDEN), this_layer, **resident),
            pl.BlockSpec((None, 8, 2 * FFN_HIDDEN), this_layer),
            pl.BlockSpec((None, FFN_HIDDEN, D_MODEL), this_layer, **resident),
        ],
        out_specs=pl.BlockSpec((1, tm, D_MODEL), tile),
        out_shape=jax.ShapeDtypeStruct((batch, seq, D_MODEL), F32),
        scratch_shapes=[
            pltpu.VMEM((tm, D_MODEL), BF16),
            pltpu.VMEM((nch, 8, 2 * ck), F32),
            pltpu.VMEM((tm + 8, 2 * ck), F32),
            pltpu.VMEM((tm, FFN_HIDDEN), BF16),
        ],
        compiler_params=pltpu.CompilerParams(
            dimension_semantics=("arbitrary", "arbitrary"), vmem_limit_bytes=VMEM_LIMIT),
        name="conv_ffn",
    )(x, *mixer_args, gain.reshape(1, D_MODEL), w_up, taps, w_down)


def _gla_kernel(x_ref, g_ref, win_ref, wa2_ref, ba_ref, og_ref, wout_ref, tri_ref,
                o_ref, state_ref, *, tm):
    i = pl.program_id(1)

    @pl.when(i == 0)
    def _():
        state_ref[...] = jnp.zeros_like(state_ref)

    x = x_ref[0]
    h = _rms(x, g_ref[...]).astype(BF16)
    proj = jnp.dot(h, win_ref[...], preferred_element_type=F32)
    kw, vw = GLA_KEY_WIDTH, GLA_VALUE_WIDTH
    q = proj[:, 0:kw] * (GLA_KEY_DIM ** -0.5)
    k = proj[:, kw:2 * kw]
    v = proj[:, 2 * kw:2 * kw + vw]
    r = proj[:, 2 * kw + vw:2 * kw + 2 * vw]
    a_low = proj[:, 2 * kw + 2 * vw:]
    gate = jnp.dot(a_low.astype(BF16), wa2_ref[...], preferred_element_type=F32) + ba_ref[...]
    log_alpha = (jnp.minimum(gate, 0.0) - _log1pexp_neg_abs(gate)) * (1.0 / GLA_TAU)
    hi, lo = _split_bf16(log_alpha)
    tri = tri_ref[...]
    span = tri.shape[0]
    cum = jnp.concatenate(
        [jnp.dot(tri, hi[r0:r0 + span, :], preferred_element_type=F32)
         + jnp.dot(tri, lo[r0:r0 + span, :], preferred_element_type=F32)
         for r0 in range(0, tm, span)], axis=0)

    nchunk = tm // GLA_CHUNK
    klanes = [slice(hd * GLA_KEY_DIM, (hd + 1) * GLA_KEY_DIM) for hd in range(GLA_HEADS)]
    vlanes = [slice(hd * GLA_VALUE_DIM, (hd + 1) * GLA_VALUE_DIM) for hd in range(GLA_HEADS)]
    updates, decays = [], []
    for c in range(nchunk):
        rows = slice(c * GLA_CHUNK, (c + 1) * GLA_CHUNK)
        cum_c = cum[rows, :]
        total = cum_c[GLA_CHUNK - 1:GLA_CHUNK, :]
        k_dec = k[rows, :] * jnp.exp(total - cum_c)
        decay = jnp.broadcast_to(jnp.exp(total), (GLA_CHUNK, kw))
        kd = jnp.concatenate([k_dec, decay], axis=0)
        v_c = v[rows, :].astype(BF16)
        for hd in range(GLA_HEADS):
            kd_t = kd[:, klanes[hd]].T
            decays.append(kd_t[:, GLA_CHUNK:GLA_CHUNK + 1])
            updates.append(jnp.dot(kd_t[:, :GLA_CHUNK].astype(BF16), v_c[:, vlanes[hd]],
                                   preferred_element_type=F32))
    states = []
    for hd in range(GLA_HEADS):
        st = state_ref[hd]
        for c in range(nchunk):
            st = st * decays[c * GLA_HEADS + hd] + updates[c * GLA_HEADS + hd]
            states.append(st.astype(BF16))
        state_ref[hd] = st
    outs = []
    for c in range(nchunk):
        q_c = q[c * GLA_CHUNK:(c + 1) * GLA_CHUNK, :].astype(BF16)
        outs.append(jnp.concatenate(
            [jnp.dot(q_c[:, klanes[hd]], states[hd * nchunk + c], preferred_element_type=F32)
             for hd in range(GLA_HEADS)], axis=1))
    o = jnp.concatenate(outs, axis=0)
    normed = jnp.concatenate(
        [_rms(o[:, hd * GLA_VALUE_DIM:(hd + 1) * GLA_VALUE_DIM], og_ref[...])
         for hd in range(GLA_HEADS)], axis=1)
    gated = (normed * (r * jax.nn.sigmoid(r))).astype(BF16)
    o_ref[0] = x + jnp.dot(gated, wout_ref[...], preferred_element_type=F32)


def _gla_mixer(x, gain, w_in, w_a2, b_a, o_gain, w_out):
    batch, seq, _ = x.shape
    tm = min(GLA_TM, seq)
    in_width = 2 * GLA_KEY_WIDTH + 2 * GLA_VALUE_WIDTH + GLA_GATE_PAD
    pad = GLA_GATE_PAD - GLA_GATE_RANK
    w_in_p = jnp.pad(w_in, ((0, 0), (0, pad))).astype(BF16)
    w_a2_p = jnp.pad(w_a2, ((0, pad), (0, 0))).astype(BF16)
    span = min(GLA_TRI_SPAN, tm)
    t = jnp.arange(span)
    tri = ((t[:, None] // GLA_CHUNK == t[None, :] // GLA_CHUNK)
           & (t[:, None] >= t[None, :])).astype(BF16)
    tile = lambda b, i: (b, i, 0)
    const2 = lambda b, i: (0, 0)
    return pl.pallas_call(
        functools.partial(_gla_kernel, tm=tm),
        grid=(batch, seq // tm),
        in_specs=[
            pl.BlockSpec((1, tm, D_MODEL), tile),
            pl.BlockSpec((1, D_MODEL), const2),
            pl.BlockSpec((D_MODEL, in_width), const2),
            pl.BlockSpec((GLA_GATE_PAD, GLA_KEY_WIDTH), const2),
            pl.BlockSpec((1, GLA_KEY_WIDTH), const2),
            pl.BlockSpec((1, GLA_VALUE_DIM), const2),
            pl.BlockSpec((GLA_VALUE_WIDTH, D_MODEL), const2),
            pl.BlockSpec((span, span), const2),
        ],
        out_specs=pl.BlockSpec((1, tm, D_MODEL), tile),
        out_shape=jax.ShapeDtypeStruct((batch, seq, D_MODEL), F32),
        scratch_shapes=[pltpu.VMEM((GLA_HEADS, GLA_KEY_DIM, GLA_VALUE_DIM), F32)],
        compiler_params=pltpu.CompilerParams(
            dimension_semantics=("arbitrary", "arbitrary"), vmem_limit_bytes=VMEM_LIMIT),
        name="gla_mixer",
    )(x, gain.reshape(1, D_MODEL), w_in_p, w_a2_p, b_a.reshape(1, GLA_KEY_WIDTH),
      o_gain.reshape(1, GLA_VALUE_DIM), w_out.astype(BF16), tri)


def kernel(x, mix_norm_even, w_in_even, sb_q_gain, sb_k_gain, pool_w, pool_scale, w_out_even,
           mix_norm_odd, w_in_odd, gla_w_a2, gla_b_a, gla_o_gain, w_out_odd,
           ffn_norm, ffn_w_up, ffn_conv_w, ffn_conv_b, ffn_w_down):
    batch, seq, _ = x.shape
    depth = ffn_norm.shape[0]
    ffn_weights = _ffn_weights(ffn_w_up, ffn_conv_w, ffn_conv_b, ffn_w_down)
    for layer in range(depth):
        i = layer // 2
        if layer % 2 == 0:
            q, k, v, xb = _even_in(x.reshape(batch * seq, D_MODEL), mix_norm_even[i],
                                   w_in_even[i], sb_q_gain[i], sb_k_gain[i])
            o_a = _attention(q, k, v, batch, seq)
            even_mixer = (xb, o_a, pool_w[i], pool_scale[i], w_out_even[i])
        else:
            x = _gla_mixer(x, mix_norm_odd[i], w_in_odd[i], gla_w_a2[i], gla_b_a[i],
                           gla_o_gain[i], w_out_odd[i])
            even_mixer = None
        x = _ffn(x, ffn_norm[layer], ffn_weights, layer, even_mixer)
    return x
```

```python
import functools

import jax
import jax.numpy as jnp
from jax import lax
from jax.experimental import pallas as pl
from jax.experimental.pallas import tpu as pltpu

F32 = jnp.float32
BF16 = jnp.bfloat16

D_MODEL = 1024
NORM_EPS = 1e-6
LOG2E = 1.4426950408889634
UNDERFLOW_LOG2 = 160.0

SB_HEADS = 8
SB_HEAD_DIM = 64
SB_WIDTH = SB_HEADS * SB_HEAD_DIM
POOL_WINDOWS = (2, 4, 8, 16)
POOL_GROUP_DIM = 128
POOL_WIDTH = 512
POOL_HALO = 16

GLA_HEADS = 4
GLA_KEY_WIDTH = 512
GLA_VALUE_WIDTH = 1024
GLA_KEY_DIM = 128
GLA_VALUE_DIM = 256
GLA_GATE_RANK = 16
GLA_TAU = 16.0
GLA_CHUNK = 64
GLA_GATE_PAD = 128

FFN_HIDDEN = 2816
FFN_CHUNK = 256
FFN_NCHUNK = FFN_HIDDEN // FFN_CHUNK

IN_TM = 1024
ATT_TQ = 512
ATT_TK = 128
ATT_PAIRS = 2
ATT_NB = 2
FFN_TM = 512
GLA_TM = 1024
GLA_TRI_SPAN = 256

VMEM_LIMIT = 56 * 1024 * 1024


def _rms(x, gain):
    ms = jnp.mean(x * x, axis=-1, keepdims=True)
    return x * lax.rsqrt(ms + NORM_EPS) * gain


def _log1pexp_neg_abs(z):
    return jnp.log(1.0 + jnp.exp(jnp.minimum(z, -z)))


def _split_bf16(x):
    hi = x.astype(BF16)
    lo = (x - hi.astype(F32)).astype(BF16)
    return hi, lo


def _even_in_kernel(x_ref, g_ref, w_ref, qg_ref, kg_ref, bd_ref,
                    q_ref, k_ref, v_ref, xb_ref):
    h = _rms(x_ref[...], g_ref[...]).astype(BF16)
    proj = jnp.dot(h, w_ref[...], preferred_element_type=F32)

    def head_norm(t, gain):
        ss = jnp.dot((t * t).astype(BF16), bd_ref[...], preferred_element_type=F32)
        return t * lax.rsqrt(ss * (1.0 / SB_HEAD_DIM) + NORM_EPS) * gain

    q = head_norm(proj[:, 0:SB_WIDTH], qg_ref[...])
    q_ref[...] = (q * (SB_HEAD_DIM ** -0.5 * LOG2E)).astype(BF16)
    k_ref[...] = head_norm(proj[:, SB_WIDTH:2 * SB_WIDTH], kg_ref[...]).astype(BF16)
    v_ref[...] = proj[:, 2 * SB_WIDTH:3 * SB_WIDTH].astype(BF16)
    xb_ref[...] = proj[:, 3 * SB_WIDTH:]


def _even_in(x2d, gain, w_in, q_gain, k_gain):
    n = x2d.shape[0]
    tm = min(IN_TM, n)
    head = jnp.arange(SB_WIDTH) // SB_HEAD_DIM
    blockdiag = (head[:, None] == head[None, :]).astype(BF16)
    width = w_in.shape[1]
    const = lambda i: (0, 0)
    tile = lambda i: (i, 0)
    return pl.pallas_call(
        _even_in_kernel,
        grid=(n // tm,),
        in_specs=[
            pl.BlockSpec((tm, D_MODEL), tile),
            pl.BlockSpec((1, D_MODEL), const),
            pl.BlockSpec((D_MODEL, width), const),
            pl.BlockSpec((1, SB_WIDTH), const),
            pl.BlockSpec((1, SB_WIDTH), const),
            pl.BlockSpec((SB_WIDTH, SB_WIDTH), const),
        ],
        out_specs=[
            pl.BlockSpec((tm, SB_WIDTH), tile),
            pl.BlockSpec((tm, SB_WIDTH), tile),
            pl.BlockSpec((tm, SB_WIDTH), tile),
            pl.BlockSpec((tm, POOL_WIDTH), tile),
        ],
        out_shape=[
            jax.ShapeDtypeStruct((n, SB_WIDTH), BF16),
            jax.ShapeDtypeStruct((n, SB_WIDTH), BF16),
            jax.ShapeDtypeStruct((n, SB_WIDTH), BF16),
            jax.ShapeDtypeStruct((n, POOL_WIDTH), F32),
        ],
        compiler_params=pltpu.CompilerParams(
            dimension_semantics=("arbitrary",), vmem_limit_bytes=VMEM_LIMIT),
        name="even_in",
    )(x2d, gain.reshape(1, D_MODEL), w_in.astype(BF16),
      jnp.tile(q_gain, SB_HEADS).reshape(1, SB_WIDTH),
      jnp.tile(k_gain, SB_HEADS).reshape(1, SB_WIDTH), blockdiag)


def _attn_kernel(q_ref, k_ref, v_ref, m2_ref, o_ref, za_ref, zb_ref, wa_ref, wb_ref,
                 c0_ref, c1_ref, acc_ref, *, tq, tk, nb):
    i = pl.program_id(2)
    npairs = q_ref.shape[2] // (2 * SB_HEAD_DIM)
    ratio = tq // tk
    ngroups = i * (ratio // nb)
    lane = lax.broadcasted_iota(jnp.int32, (tk, 2 * SB_HEAD_DIM), 1)
    first_head = lane < SB_HEAD_DIM
    zero = jnp.zeros((), BF16)

    def stack_heads(blk):
        return jnp.concatenate(
            [jnp.where(first_head, blk, zero), jnp.where(first_head, zero, blk)], axis=0)

    def stacked(ref, p, start, n):
        lanes = slice(p * 2 * SB_HEAD_DIM, (p + 1) * 2 * SB_HEAD_DIM)
        return jnp.concatenate(
            [stack_heads(ref[0, pl.ds(start + u * tk, tk), lanes]) for u in range(n)], axis=0)

    def scores(p, q_rows, start, n):
        return lax.dot_general(q_rows, stacked(k_ref, p, start, n), (((1,), (1,)), ((), ())),
                               preferred_element_type=F32)

    def attend(p, w, start, n):
        return jnp.dot(w, stacked(v_ref, p, start, n), preferred_element_type=F32)

    def group_start(g):
        return pl.multiple_of(jnp.maximum(i * ratio - (g + 1) * nb, 0) * tk, tk)

    def block_weights(z2, c0, c1, visible):
        ws, cs = [], []
        for hh, c in ((0, c0), (1, c1)):
            z = z2[:, hh * tk:(hh + 1) * tk]
            softplus = jnp.maximum(z, 0.0) + jnp.log2(1.0 + jnp.exp2(jnp.minimum(z, -z)))
            if visible is not None:
                softplus = jnp.where(visible, softplus, 0.0)
            hi, lo = _split_bf16(softplus)
            cum = jnp.dot(jnp.concatenate([hi, lo], axis=1), m2_ref[...],
                          preferred_element_type=F32)
            w = jnp.exp2(z + cum[:, :tk] + c)
            if visible is not None:
                w = jnp.where(visible, w, 0.0)
            ws.append(w.astype(BF16))
            cs.append(c + cum[:, tk:])
        return jnp.concatenate(ws, axis=1), cs[0], cs[1]

    def pair_query(p):
        return q_ref[0, :, p * 2 * SB_HEAD_DIM:(p + 1) * 2 * SB_HEAD_DIM]

    def diagonal(p, q):
        c0 = jnp.zeros((tq, tk), F32)
        c1 = jnp.zeros((tq, tk), F32)
        acc = jnp.zeros((tq, 2 * SB_HEAD_DIM), F32)
        for u in reversed(range(ratio)):
            r0 = u * tk
            start = pl.multiple_of(i * tq + r0, tk)
            row = lax.broadcasted_iota(jnp.int32, (tq - r0, tk), 0)
            col = lax.broadcasted_iota(jnp.int32, (tq - r0, tk), 1)
            w, p0, p1 = block_weights(scores(p, q[r0:, :], start, 1),
                                      c0[r0:, :], c1[r0:, :], col < row)
            pacc = acc[r0:, :] + attend(p, w, start, 1)
            if r0:
                c0 = jnp.concatenate([c0[:r0, :], p0], axis=0)
                c1 = jnp.concatenate([c1[:r0, :], p1], axis=0)
                acc = jnp.concatenate([acc[:r0, :], pacc], axis=0)
            else:
                c0, c1, acc = p0, p1, pacc
        return c0, c1, acc

    def spent(c0, c1):
        return jnp.maximum(jnp.max(c0), jnp.max(c1)) <= -UNDERFLOW_LOG2

    def stacked_all(ref, start):
        blk = ref[0, pl.ds(start, tk), :]
        head = lax.broadcasted_iota(jnp.int32, blk.shape, 1) // SB_HEAD_DIM
        return jnp.concatenate(
            [jnp.where(head == h, blk, zero) for h in range(2 * npairs)], axis=0)

    def near_short():
        q = q_ref[0]
        nheads = 2 * npairs
        c = [jnp.zeros((tq, tk), F32)] * nheads
        acc = jnp.zeros((tq, q.shape[1]), F32)

        def visit(r0, r1, start, visible, off_diagonal):
            z = lax.dot_general(q[r0:r1, :], stacked_all(k_ref, start),
                                (((1,), (1,)), ((), ())), preferred_element_type=F32)
            ws, cs = [], []
            for p in range(npairs):
                w, p0, p1 = block_weights(z[:, p * 2 * tk:(p + 1) * 2 * tk],
                                          c[2 * p][r0:r1, :], c[2 * p + 1][r0:r1, :], visible)
                ws.append(w)
                cs += [p0, p1]
            w = jnp.concatenate(ws, axis=1)
            if off_diagonal:
                w = jnp.where(i > 0, w, zero)
            return cs, acc[r0:r1, :] + jnp.dot(w, stacked_all(v_ref, start),
                                               preferred_element_type=F32)

        for u in reversed(range(ratio)):
            r0, r1 = u * tk, min(tq, (u + 3) * tk)
            row = lax.broadcasted_iota(jnp.int32, (r1 - r0, tk), 0)
            col = lax.broadcasted_iota(jnp.int32, (r1 - r0, tk), 1)
            cs, pacc = visit(r0, r1, pl.multiple_of(i * tq + r0, tk), col < row, False)
            def replace_rows(old, new):
                parts = [old[:r0, :]] * (r0 > 0) + [new] + [old[r1:, :]] * (r1 < tq)
                return jnp.concatenate(parts, axis=0) if len(parts) > 1 else new

            c = [replace_rows(old, new) for old, new in zip(c, cs)]
            acc = replace_rows(acc, pacc)
        skipped_ok = [spent(c[2 * p][3 * tk:, :], c[2 * p + 1][3 * tk:, :]) if ratio > 3 else True
                      for p in range(npairs)]
        enough = [spent(c[2 * p][2 * tk:, :], c[2 * p + 1][2 * tk:, :]) for p in range(npairs)]
        for b in (1, 2):
            rows = (3 - b) * tk
            c, pacc = visit(0, rows, pl.multiple_of(jnp.maximum(i * ratio - b, 0) * tk, tk),
                            None, True)
            acc = jnp.concatenate([pacc, acc[rows:, :]], axis=0)
            enough = [jnp.logical_and(e, spent(c[2 * p][rows - tk:, :], c[2 * p + 1][rows - tk:, :]))
                      for p, e in enumerate(enough)]
        for p in range(npairs):
            acc_ref[p] = acc[:, p * 2 * SB_HEAD_DIM:(p + 1) * 2 * SB_HEAD_DIM]
        return [jnp.logical_and(s, jnp.logical_or(i == 0, e)) for s, e in zip(skipped_ok, enough)]

    def near(p):
        q = pair_query(p)
        c0, c1, acc = diagonal(p, q)
        z_near = scores(p, q, group_start(0), nb)
        w_near = [None] * nb
        for u in reversed(range(nb)):
            w_near[u], c0, c1 = block_weights(z_near[:, u * 2 * tk:(u + 1) * 2 * tk],
                                              c0, c1, None)
        w_near = jnp.where(i > 0, jnp.concatenate(w_near, axis=1), zero)
        acc = acc + attend(p, w_near, group_start(0), nb)
        c0_ref[p] = c0
        c1_ref[p] = c1
        acc_ref[p] = acc

    def far(p):
        q = pair_query(p)

        def stage(g, z_in, z_out, w_in, w_out):
            acc_ref[p] += attend(p, w_in[...], group_start(g - 1), nb)
            z_out[...] = scores(p, q, group_start(g + 1), nb)
            for u in reversed(range(nb)):
                cols = slice(u * 2 * tk, (u + 1) * 2 * tk)
                w, c0, c1 = block_weights(z_in[:, cols], c0_ref[p], c1_ref[p], None)
                c0_ref[p] = c0
                c1_ref[p] = c1
                w_out[:, cols] = w

        def any_weight_left():
            return jnp.maximum(jnp.max(c0_ref[p]), jnp.max(c1_ref[p])) > -UNDERFLOW_LOG2

        farther = jnp.logical_and(ngroups > 1, any_weight_left())

        @pl.when(farther)
        def _():
            wb_ref[...] = jnp.zeros_like(wb_ref)
            za_ref[...] = scores(p, q, group_start(1), nb)

        def cond(state):
            g, alive = state
            return jnp.logical_and(g < ngroups, alive)

        def body(state):
            g, _ = state
            stage(g, za_ref, zb_ref, wb_ref, wa_ref)
            more = jnp.logical_and(g + 1 < ngroups, any_weight_left())

            @pl.when(more)
            def _():
                stage(g + 1, zb_ref, za_ref, wa_ref, wb_ref)

            return g + 1 + more.astype(jnp.int32), any_weight_left()

        last, _ = lax.while_loop(cond, body, (jnp.int32(1), farther))

        stages = last - 1

        @pl.when(stages % 2 == 1)
        def _():
            acc_ref[p] += attend(p, wa_ref[...], group_start(last - 1), nb)

        @pl.when(jnp.logical_and(stages > 0, stages % 2 == 0))
        def _():
            acc_ref[p] += attend(p, wb_ref[...], group_start(last - 1), nb)

        o_ref[0, :, p * 2 * SB_HEAD_DIM:(p + 1) * 2 * SB_HEAD_DIM] = acc_ref[p].astype(BF16)

    enough = near_short()
    for p in range(npairs):
        @pl.when(enough[p])
        def _(p=p):
            o_ref[0, :, p * 2 * SB_HEAD_DIM:(p + 1) * 2 * SB_HEAD_DIM] = acc_ref[p].astype(BF16)

        @pl.when(jnp.logical_not(enough[p]))
        def _(p=p):
            near(p)
            far(p)


def _attention(q, k, v, batch, seq):
    tq = min(ATT_TQ, seq)
    tk = min(ATT_TK, tq)
    q3 = q.reshape(batch, seq, SB_WIDTH)
    k3 = k.reshape(batch, seq, SB_WIDTH)
    v3 = v.reshape(batch, seq, SB_WIDTH)
    nb = ATT_NB
    assert seq % tq == 0 and (tq // tk) % nb == 0 and tq // tk >= 3
    r = jnp.arange(2 * tk) % tk
    ccol = jnp.arange(2 * tk)
    m2 = -jnp.where(ccol[None, :] < tk, r[:, None] >= ccol[None, :], True).astype(BF16)
    width = ATT_PAIRS * 2 * SB_HEAD_DIM
    return pl.pallas_call(
        functools.partial(_attn_kernel, tq=tq, tk=tk, nb=nb),
        grid=(batch, SB_WIDTH // width, seq // tq),
        scratch_shapes=[
            pltpu.VMEM((tq, nb * 2 * tk), F32), pltpu.VMEM((tq, nb * 2 * tk), F32),
            pltpu.VMEM((tq, nb * 2 * tk), BF16), pltpu.VMEM((tq, nb * 2 * tk), BF16),
            pltpu.VMEM((ATT_PAIRS, tq, tk), F32), pltpu.VMEM((ATT_PAIRS, tq, tk), F32),
            pltpu.VMEM((ATT_PAIRS, tq, 2 * SB_HEAD_DIM), F32),
        ],
        in_specs=[
            pl.BlockSpec((1, tq, width), lambda b, p, i: (b, i, p)),
            pl.BlockSpec((1, seq, width), lambda b, p, i: (b, 0, p)),
            pl.BlockSpec((1, seq, width), lambda b, p, i: (b, 0, p)),
            pl.BlockSpec((2 * tk, 2 * tk), lambda b, p, i: (0, 0)),
        ],
        out_specs=pl.BlockSpec((1, tq, width), lambda b, p, i: (b, i, p)),
        out_shape=jax.ShapeDtypeStruct((batch, seq, SB_WIDTH), BF16),
        compiler_params=pltpu.CompilerParams(
            dimension_semantics=("arbitrary", "arbitrary", "arbitrary"),
            vmem_limit_bytes=VMEM_LIMIT),
        name="sb_attention",
    )(q3, k3, v3, m2)


def _even_mixer_residual(i, tm, x_ref, xb_ref, halo_ref, oa_ref, pw_ref, ps_ref, wo_ref):
    xb = xb_ref[0]
    halo = jnp.where(i > 0, halo_ref[0], 0.0)
    ext = jnp.concatenate([halo, xb], axis=0)
    pos = (i * tm + 1 + lax.broadcasted_iota(jnp.int32, (tm, 1), 0)).astype(F32)
    y = jnp.dot(oa_ref[0], wo_ref[0:SB_WIDTH, :], preferred_element_type=F32)
    pooled_out = []
    for g, w in enumerate(POOL_WINDOWS):
        lanes = slice(g * POOL_GROUP_DIM, (g + 1) * POOL_GROUP_DIM)
        s = ext[:, lanes]
        sh = 1
        while sh < w:
            s = s + pltpu.roll(s, sh, axis=0)
            sh *= 2
        window_sum = s[POOL_HALO:, :]
        pooled = window_sum / jnp.minimum(pos, float(w)) - xb[:, lanes]
        ob = jnp.dot(pooled.astype(BF16), pw_ref[g], preferred_element_type=F32)
        pooled_out.append((ob * ps_ref[:, lanes]).astype(BF16))
    y = y + jnp.dot(jnp.concatenate(pooled_out, axis=1), wo_ref[SB_WIDTH:, :],
                    preferred_element_type=F32)
    return x_ref[0] + y


def _ffn_kernel(*refs, tm, with_even_mixer):
    n_mixer = 6 if with_even_mixer else 0
    x_ref, mixer_refs = refs[0], refs[1:1 + n_mixer]
    g_ref, wup_ref, cw_ref, wd_ref, o_ref, h_ref, carry_ref, u_ref, act_ref = refs[1 + n_mixer:]
    i = pl.program_id(1)
    nch = FFN_NCHUNK

    @pl.when(i == 0)
    def _():
        carry_ref[...] = jnp.zeros_like(carry_ref)

    if with_even_mixer:
        x = _even_mixer_residual(i, tm, x_ref, *mixer_refs)
    else:
        x = x_ref[0]
    h_ref[...] = _rms(x, g_ref[...]).astype(BF16)

    def columns(c):
        return (slice(c * FFN_CHUNK, (c + 1) * FFN_CHUNK),
                slice(FFN_HIDDEN + c * FFN_CHUNK, FFN_HIDDEN + (c + 1) * FFN_CHUNK))

    def up(c):
        h = h_ref[...]
        u = jnp.concatenate([jnp.dot(h, wup_ref[:, cols], preferred_element_type=F32)
                             for cols in columns(c)], axis=1)
        u_ref[0:8, :] = carry_ref[c]
        u_ref[8:8 + tm, :] = u
        carry_ref[c] = u[tm - 8:, :]

    def activate(c):
        cw = jnp.concatenate([cw_ref[:, cols] for cols in columns(c)], axis=1)
        conv = (cw[3:4, :] + cw[2:3, :] * u_ref[8:8 + tm, :]
                + cw[1:2, :] * u_ref[7:7 + tm, :]
                + cw[0:1, :] * u_ref[6:6 + tm, :])
        a = conv[:, :FFN_CHUNK]
        gate = conv[:, FFN_CHUNK:]
        act_ref[:, columns(c)[0]] = (a * jax.nn.sigmoid(a) * gate).astype(BF16)

    for c in range(nch):
        up(c)
        activate(c)
    o_ref[0] = x + jnp.dot(act_ref[...], wd_ref[...], preferred_element_type=F32)


def _ffn_weights(w_up, conv_w, conv_b, w_down):
    depth = w_up.shape[0]
    taps = jnp.concatenate([conv_w, conv_b[:, None, :],
                            jnp.zeros((depth, 4, 2 * FFN_HIDDEN), F32)], axis=1)
    return w_up.astype(BF16), taps, w_down.astype(BF16)


def _ffn(x, gain, weights, layer, even_mixer=None):
    batch, seq, _ = x.shape
    tm = min(FFN_TM, seq)
    ck, nch = FFN_CHUNK, FFN_NCHUNK
    w_up, taps, w_down = weights
    tile = lambda b, i: (b, i, 0)
    const2 = lambda b, i: (0, 0)
    const3 = lambda b, i: (0, 0, 0)
    this_layer = lambda b, i: (layer, 0, 0)
    resident = dict(pipeline_mode=pl.Buffered(1))
    mixer_specs, mixer_args = [], []
    if even_mixer is not None:
        xb, o_a, pool_w, pool_scale, w_out = even_mixer
        xb3 = xb.reshape(batch, seq, POOL_WIDTH)
        hb = tm // POOL_HALO
        mixer_specs = [
            pl.BlockSpec((1, tm, POOL_WIDTH), tile),
            pl.BlockSpec((1, POOL_HALO, POOL_WIDTH),
                         lambda b, i: (b, jnp.maximum(i * hb - 1, 0), 0)),
            pl.BlockSpec((1, tm, SB_WIDTH), tile),
            pl.BlockSpec((len(POOL_WINDOWS), POOL_GROUP_DIM, POOL_GROUP_DIM), const3),
            pl.BlockSpec((1, POOL_WIDTH), const2),
            pl.BlockSpec((D_MODEL, D_MODEL), const2, **resident),
        ]
        mixer_args = [xb3, xb3, o_a, pool_w.astype(BF16), pool_scale.reshape(1, POOL_WIDTH),
                      w_out.astype(BF16)]
    return pl.pallas_call(
        functools.partial(_ffn_kernel, tm=tm, with_even_mixer=even_mixer is not None),
        grid=(batch, seq // tm),
        in_specs=[pl.BlockSpec((1, tm, D_MODEL), tile)] + mixer_specs + [
            pl.BlockSpec((1, D_MODEL), const2),
            pl.BlockSpec((None, D_MODEL, 2 * FFN_HIDDEN), this_layer, **resident),
            pl.BlockSpec((None, 8, 2 * FFN_HIDDEN), this_layer),
            pl.BlockSpec((None, FFN_HIDDEN, D_MODEL), this_layer, **resident),
        ],
        out_specs=pl.BlockSpec((1, tm, D_MODEL), tile),
        out_shape=jax.ShapeDtypeStruct((batch, seq, D_MODEL), F32),
        scratch_shapes=[
            pltpu.VMEM((tm, D_MODEL), BF16),
            pltpu.VMEM((nch, 8, 2 * ck), F32),
            pltpu.VMEM((tm + 8, 2 * ck), F32),
            pltpu.VMEM((tm, FFN_HIDDEN), BF16),
        ],
        compiler_params=pltpu.CompilerParams(
            dimension_semantics=("arbitrary", "arbitrary"), vmem_limit_bytes=VMEM_LIMIT),
        name="conv_ffn",
    )(x, *mixer_args, gain.reshape(1, D_MODEL), w_up, taps, w_down)


def _gla_kernel(x_ref, g_ref, win_ref, wa2_ref, ba_ref, og_ref, wout_ref, tri_ref,
                o_ref, state_ref, *, tm):
    i = pl.program_id(1)

    @pl.when(i == 0)
    def _():
        state_ref[...] = jnp.zeros_like(state_ref)

    x = x_ref[0]
    h = _rms(x, g_ref[...]).astype(BF16)
    proj = jnp.dot(h, win_ref[...], preferred_element_type=F32)
    kw, vw = GLA_KEY_WIDTH, GLA_VALUE_WIDTH
    q = proj[:, 0:kw] * (GLA_KEY_DIM ** -0.5)
    k = proj[:, kw:2 * kw]
    v = proj[:, 2 * kw:2 * kw + vw]
    r = proj[:, 2 * kw + vw:2 * kw + 2 * vw]
    a_low = proj[:, 2 * kw + 2 * vw:]
    gate = jnp.dot(a_low.astype(BF16), wa2_ref[...], preferred_element_type=F32) + ba_ref[...]
    log_alpha = (jnp.minimum(gate, 0.0) - _log1pexp_neg_abs(gate)) * (1.0 / GLA_TAU)
    hi, lo = _split_bf16(log_alpha)
    tri = tri_ref[...]
    span = tri.shape[0]
    tri2 = jnp.concatenate([tri, tri], axis=1)
    cum = jnp.concatenate(
        [jnp.dot(tri2, jnp.concatenate([hi[r0:r0 + span, :], lo[r0:r0 + span, :]], axis=0),
                 preferred_element_type=F32)
         for r0 in range(0, tm, span)], axis=0)

    nchunk = tm // GLA_CHUNK
    klanes = [slice(hd * GLA_KEY_DIM, (hd + 1) * GLA_KEY_DIM) for hd in range(GLA_HEADS)]
    vlanes = [slice(hd * GLA_VALUE_DIM, (hd + 1) * GLA_VALUE_DIM) for hd in range(GLA_HEADS)]
    updates, decays = [], []
    for c in range(nchunk):
        rows = slice(c * GLA_CHUNK, (c + 1) * GLA_CHUNK)
        cum_c = cum[rows, :]
        total = cum_c[GLA_CHUNK - 1:GLA_CHUNK, :]
        k_dec = k[rows, :] * jnp.exp(total - cum_c)
        decay = jnp.broadcast_to(jnp.exp(total), (GLA_CHUNK, kw))
        kd = jnp.concatenate([k_dec, decay], axis=0)
        v_c = v[rows, :].astype(BF16)
        for hd in range(GLA_HEADS):
            kd_t = kd[:, klanes[hd]].T
            decays.append(kd_t[:, GLA_CHUNK:GLA_CHUNK + 1])
            updates.append(jnp.dot(kd_t[:, :GLA_CHUNK].astype(BF16), v_c[:, vlanes[hd]],
                                   preferred_element_type=F32))
    states = []
    for hd in range(GLA_HEADS):
        st = state_ref[hd]
        for c in range(nchunk):
            st = st * decays[c * GLA_HEADS + hd] + updates[c * GLA_HEADS + hd]
            states.append(st.astype(BF16))
        state_ref[hd] = st
    outs = []
    for c in range(nchunk):
        q_c = q[c * GLA_CHUNK:(c + 1) * GLA_CHUNK, :].astype(BF16)
        outs.append(jnp.concatenate(
            [jnp.dot(q_c[:, klanes[hd]], states[hd * nchunk + c], preferred_element_type=F32)
             for hd in range(GLA_HEADS)], axis=1))
    o = jnp.concatenate(outs, axis=0)
    normed = jnp.concatenate(
        [_rms(o[:, hd * GLA_VALUE_DIM:(hd + 1) * GLA_VALUE_DIM], og_ref[...])
         for hd in range(GLA_HEADS)], axis=1)
    gated = (normed * (r * jax.nn.sigmoid(r))).astype(BF16)
    o_ref[0] = x + jnp.dot(gated, wout_ref[...], preferred_element_type=F32)


def _gla_mixer(x, gain, w_in, w_a2, b_a, o_gain, w_out):
    batch, seq, _ = x.shape
    tm = min(GLA_TM, seq)
    in_width = 2 * GLA_KEY_WIDTH + 2 * GLA_VALUE_WIDTH + GLA_GATE_PAD
    pad = GLA_GATE_PAD - GLA_GATE_RANK
    w_in_p = jnp.pad(w_in, ((0, 0), (0, pad))).astype(BF16)
    w_a2_p = jnp.pad(w_a2, ((0, pad), (0, 0))).astype(BF16)
    span = min(GLA_TRI_SPAN, tm)
    t = jnp.arange(span)
    tri = ((t[:, None] // GLA_CHUNK == t[None, :] // GLA_CHUNK)
           & (t[:, None] >= t[None, :])).astype(BF16)
    tile = lambda b, i: (b, i, 0)
    const2 = lambda b, i: (0, 0)
    return pl.pallas_call(
        functools.partial(_gla_kernel, tm=tm),
        grid=(batch, seq // tm),
        in_specs=[
            pl.BlockSpec((1, tm, D_MODEL), tile),
            pl.BlockSpec((1, D_MODEL), const2),
            pl.BlockSpec((D_MODEL, in_width), const2),
            pl.BlockSpec((GLA_GATE_PAD, GLA_KEY_WIDTH), const2),
            pl.BlockSpec((1, GLA_KEY_WIDTH), const2),
            pl.BlockSpec((1, GLA_VALUE_DIM), const2),
            pl.BlockSpec((GLA_VALUE_WIDTH, D_MODEL), const2),
            pl.BlockSpec((span, span), const2),
        ],
        out_specs=pl.BlockSpec((1, tm, D_MODEL), tile),
        out_shape=jax.ShapeDtypeStruct((batch, seq, D_MODEL), F32),
        scratch_shapes=[pltpu.VMEM((GLA_HEADS, GLA_KEY_DIM, GLA_VALUE_DIM), F32)],
        compiler_params=pltpu.CompilerParams(
            dimension_semantics=("arbitrary", "arbitrary"), vmem_limit_bytes=VMEM_LIMIT),
        name="gla_mixer",
    )(x, gain.reshape(1, D_MODEL), w_in_p, w_a2_p, b_a.reshape(1, GLA_KEY_WIDTH),
      o_gain.reshape(1, GLA_VALUE_DIM), w_out.astype(BF16), tri)


def kernel(x, mix_norm_even, w_in_even, sb_q_gain, sb_k_gain, pool_w, pool_scale, w_out_even,
           mix_norm_odd, w_in_odd, gla_w_a2, gla_b_a, gla_o_gain, w_out_odd,
           ffn_norm, ffn_w_up, ffn_conv_w, ffn_conv_b, ffn_w_down):
    batch, seq, _ = x.shape
    depth = ffn_norm.shape[0]
    ffn_weights = _ffn_weights(ffn_w_up, ffn_conv_w, ffn_conv_b, ffn_w_down)
    for layer in range(depth):
        i = layer // 2
        if layer % 2 == 0:
            q, k, v, xb = _even_in(x.reshape(batch * seq, D_MODEL), mix_norm_even[i],
                                   w_in_even[i], sb_q_gain[i], sb_k_gain[i])
            o_a = _attention(q, k, v, batch, seq)
            even_mixer = (xb, o_a, pool_w[i], pool_scale[i], w_out_even[i])
        else:
            x = _gla_mixer(x, mix_norm_odd[i], w_in_odd[i], gla_w_a2[i], gla_b_a[i],
                           gla_o_gain[i], w_out_odd[i])
            even_mixer = None
        x = _ffn(x, ffn_norm[layer], ffn_weights, layer, even_mixer)
    return x
```
